```python
import math
import jax, jax.numpy as jnp
from jax import lax
import numpy as np

D_MODEL = 2048
BATCH = 2
SEQ = 8192
DEPTH = 4
DEC_BATCH = 32
DEC_SEQ = 32
PAST_LEN = 1024

CHUNK = 64
Q_BLOCK = 128
HA = 8
DA = 64
DVA = 2 * DA
ROT_A = DA // 4
ROPE_THETA = 500000.0
HB = 8
DKB = 64
DVB = 128
RET_THETA = 10000.0
HC = 16
NC = 64
LORA_W = 64
LORA_A = 64
LORA_G = 160
WA = HA * DVA
WB = HB * DVB
WC = HC * NC
BRANCH_W = 1024
N_BRANCH = 3
A_COLS = 2 * HA * 2 * DA + HA * DVA
B_COLS = 2 * HB * DKB + HB * DVB + WB
C_COLS = 3 * WC + LORA_W + LORA_A + LORA_G
IN_COLS = A_COLS + B_COLS + C_COLS
D_FF = 5504
LN_EPS = 1e-5
SUBLN_EPS = 1e-5
RET_EPS = 1e-6
GN_EPS_C = 64e-5
DEEPNORM_ALPHA = (2 * DEPTH) ** 0.25
DEEPNORM_BETA = (8 * DEPTH) ** -0.25

kernel_name = 'hybrid_streaming_encoder_step'

F32 = jnp.float32


def layer_norm(x, g, b, eps=LN_EPS):
    xf = x.astype(F32)
    mu = jnp.mean(xf, axis=-1, keepdims=True)
    var = jnp.mean(jnp.square(xf - mu), axis=-1, keepdims=True)
    return ((xf - mu) * lax.rsqrt(var + eps) * g + b).astype(x.dtype)


def rms_norm(x, g, eps):
    xf = x.astype(F32)
    y = xf * lax.rsqrt(jnp.mean(xf * xf, axis=-1, keepdims=True) + eps)
    return y if g is None else y * g


def swiglu(x, w_up, w_down):
    a, b = jnp.split(x @ w_up, 2, axis=-1)
    return (jax.nn.silu(a) * b) @ w_down


def rope_partial(x, pos, rot, theta):
    half = rot // 2
    freq = jnp.power(theta, -jnp.arange(half, dtype=F32) / half)
    ang = pos.astype(F32)[:, None] * freq[None, :]
    cos = jnp.cos(ang)[:, None, :]
    sin = jnp.sin(ang)[:, None, :]
    xr = x[..., :rot].astype(F32)
    x1, x2 = xr[..., :half], xr[..., half:]
    rotated = jnp.concatenate([x1 * cos - x2 * sin, x2 * cos + x1 * sin], axis=-1).astype(x.dtype)
    return jnp.concatenate([rotated, x[..., rot:]], axis=-1)


def diff_attn_core(q, k, v, lam, key_ok):
    qm = q.astype(F32).reshape(*q.shape[:3], 2, DA)
    km = k.astype(F32).reshape(*k.shape[:3], 2, DA)
    s = jnp.einsum('bqhmd,bkhmd->bmhqk', qm, km) * (DA ** -0.5)
    if key_ok is not None:
        s = jnp.where(key_ok, s, -1e30)
    p = jax.nn.softmax(s, axis=-1)
    att = p[:, 0] - lam * p[:, 1]
    return jnp.einsum('bhqk,bkhe->bqhe', att, v.astype(F32))


def diff_attn_prompt(q, k, v, lam):
    B, T = q.shape[:2]
    nb = T // Q_BLOCK
    qb = q.reshape(B, nb, Q_BLOCK, HA, 2 * DA).swapaxes(0, 1)
    key_chunk = jnp.arange(T) // CHUNK

    def block(args):
        qi, i = args
        q_chunk = (i * Q_BLOCK + jnp.arange(Q_BLOCK)) // CHUNK
        ok = key_chunk[None, :] <= q_chunk[:, None]
        return diff_attn_core(qi, k, v, lam, ok)

    o = lax.map(block, (qb, jnp.arange(nb)))
    return o.swapaxes(0, 1).reshape(B, T, HA, DVA)


def retention_chunk(q, k, v, R, log_g):
    L = q.shape[1]
    n = jnp.arange(L, dtype=F32)
    diff = n[:, None] - n[None, :]
    dec = jnp.where(diff >= 0, jnp.exp(jnp.maximum(diff, 0.0)[None] * log_g[:, None, None]), 0.0)
    s = jnp.einsum('blhd,bmhd->bhlm', q, k) * dec
    intra = jnp.einsum('bhlm,bmhe->blhe', s, v)
    qdec = jnp.exp((n + 1.0)[:, None] * log_g[None, :])
    cross = jnp.einsum('blhd,bhde->blhe', q, R) * qdec[None, :, :, None]
    kdec = jnp.exp((L - 1.0 - n)[:, None] * log_g[None, :])
    R_new = jnp.exp(L * log_g)[None, :, None, None] * R + jnp.einsum('blhd,lh,blhe->bhde', k, kdec, v)
    return intra + cross, R_new


def retention_prompt(q, k, v, log_g):
    B, T = q.shape[:2]
    nc = T // CHUNK

    def resh(t):
        return t.reshape(B, nc, CHUNK, *t.shape[2:]).swapaxes(0, 1)

    def step(R, inp):
        qc, kc, vc = inp
        o, R = retention_chunk(qc, kc, vc, R, log_g)
        return R, o

    R0 = jnp.zeros((B, HB, DKB, DVB), F32)
    R, o = lax.scan(step, R0, (resh(q), resh(k), resh(v)))
    return o.swapaxes(0, 1).reshape(B, T, HB, DVB), R


def wkv_scan(r, w, k, v, a_in, b_in, S0):
    def step(S, inp):
        r_t, w_t, k_t, v_t, a_t, b_t = inp
        sa = jnp.einsum('bhij,bhj->bhi', S, a_t)
        S = S * w_t[:, :, None, :] + sa[..., None] * b_t[:, :, None, :] + v_t[..., None] * k_t[:, :, None, :]
        y = jnp.einsum('bhij,bhj->bhi', S, r_t)
        return S, y

    xs = tuple(jnp.moveaxis(t, 1, 0) for t in (r, w, k, v, a_in, b_in))
    S, ys = lax.scan(step, S0, xs)
    return jnp.moveaxis(ys, 0, 1), S


def head_group_norm(y, g, b, eps):
    mu = jnp.mean(y, axis=-1, keepdims=True)
    var = jnp.mean(jnp.square(y - mu), axis=-1, keepdims=True)
    yn = (y - mu) * lax.rsqrt(var + eps)
    return yn.reshape(*y.shape[:2], -1) * g + b


def token_mix(h, pos, l, past, W):
    B, T, _ = h.shape
    proj = h @ W['w_in'][l]
    pa, pb, pc = jnp.split(proj, [A_COLS, A_COLS + B_COLS], axis=-1)

    qa, ka, va = jnp.split(pa, [HA * 2 * DA, 2 * HA * 2 * DA], axis=-1)
    qa = rope_partial(qa.reshape(B, T, 2 * HA, DA), pos, ROT_A, ROPE_THETA).reshape(B, T, HA, 2 * DA)
    ka = rope_partial(ka.reshape(B, T, 2 * HA, DA), pos, ROT_A, ROPE_THETA).reshape(B, T, HA, 2 * DA)
    va = va.reshape(B, T, HA, DVA)
    lv = W['diff_lambda'][l].astype(F32)
    lam_init = 0.8 - 0.6 * math.exp(-0.3 * l)
    lam = jnp.exp(jnp.sum(lv[0] * lv[1])) - jnp.exp(jnp.sum(lv[2] * lv[3])) + lam_init
    if past is None:
        oa = diff_attn_prompt(qa, ka, va, lam)
    else:
        k_all = jnp.concatenate([past[0].astype(ka.dtype), ka], axis=1)
        v_all = jnp.concatenate([past[1].astype(va.dtype), va], axis=1)
        oa = diff_attn_core(qa, k_all, v_all, lam, None)
    oa = (rms_norm(oa, W['subln_g'][l], SUBLN_EPS) * (1.0 - lam_init)).reshape(B, T, WA).astype(h.dtype)

    qb, kb, vb, gb = jnp.split(pb, [HB * DKB, 2 * HB * DKB, 2 * HB * DKB + WB], axis=-1)
    qb = rope_partial(qb.reshape(B, T, HB, DKB), pos, DKB, RET_THETA).astype(F32)
    kb = rope_partial(kb.reshape(B, T, HB, DKB), pos, DKB, RET_THETA).astype(F32) * (DKB ** -0.5)
    vb = vb.reshape(B, T, HB, DVB).astype(F32)
    log_g = jnp.log1p(-jnp.exp2(-5.0 - jnp.arange(HB, dtype=F32)))
    if past is None:
        ob, R = retention_prompt(qb, kb, vb, log_g)
    else:
        ob, R = retention_chunk(qb, kb, vb, past[2].astype(F32), log_g)
    ob = (rms_norm(ob, None, RET_EPS).reshape(B, T, WB) * jax.nn.silu(gb.astype(F32))).astype(h.dtype)

    prev = jnp.zeros((B, 1, C_COLS), pc.dtype) if past is None else past[4].astype(pc.dtype)
    shifted = jnp.concatenate([prev, pc[:, :-1]], axis=1)
    pcm = (pc + W['mu_c'][l] * (shifted - pc)).astype(F32)
    rc, kc, vc, wl, al, gl = jnp.split(
        pcm, [WC, 2 * WC, 3 * WC, 3 * WC + LORA_W, 3 * WC + LORA_W + LORA_A], axis=-1)
    w_log = -jax.nn.softplus(-(W['w0'][l] + jnp.tanh(wl) @ W['w2'][l])) - 0.5
    decay = jnp.exp(-jnp.exp(w_log))
    a = jax.nn.sigmoid(W['a0'][l] + al @ W['a2'][l])
    g = jax.nn.sigmoid(gl) @ W['g2'][l]

    def hd(t):
        return t.reshape(B, T, HC, NC)

    kk = hd(kc * W['k_k'][l])
    kk = kk / jnp.maximum(jnp.sqrt(jnp.sum(kk * kk, axis=-1, keepdims=True)), 1e-12)
    kc = kc * (1.0 + (a - 1.0) * W['k_a'][l])
    S0 = jnp.zeros((B, HC, NC, NC), F32) if past is None else past[3].astype(F32)
    y, S = wkv_scan(hd(rc), hd(decay), hd(kc), hd(vc), -kk, kk * hd(a), S0)
    y = head_group_norm(y, W['lnx_g'][l], W['lnx_b'][l], GN_EPS_C)
    bonus = jnp.sum(hd(rc) * hd(kc) * W['r_k'][l], axis=-1, keepdims=True) * hd(vc)
    oc = ((y + bonus.reshape(B, T, WC)) * g).astype(h.dtype)

    gates = jax.nn.sigmoid(h @ W['w_gate'][l]).reshape(B, T, N_BRANCH, D_MODEL)
    wb = W['w_branch'][l]
    merged = gates[:, :, 0] * (oa @ wb[0]) + gates[:, :, 1] * (ob @ wb[1]) + gates[:, :, 2] * (oc @ wb[2])
    out = merged @ W['w_out'][l]
    return out, (ka, va, R, S, pc[:, -1:])


def run_trunk(x, pos, past, W):
    bufs = ([], [], [], [], [])
    for l in range(DEPTH):
        x = layer_norm(DEEPNORM_ALPHA * x + 0.5 * swiglu(x, W['ffn_in'][l, 0], W['ffn_out'][l, 0]),
                       W['ln_g'][l, 0], W['ln_b'][l, 0])
        past_l = None if past is None else tuple(c[l] for c in past)
        m, st = token_mix(x, pos, l, past_l, W)
        x = layer_norm(DEEPNORM_ALPHA * x + m, W['ln_g'][l, 1], W['ln_b'][l, 1])
        x = layer_norm(DEEPNORM_ALPHA * x + 0.5 * swiglu(x, W['ffn_in'][l, 1], W['ffn_out'][l, 1]),
                       W['ln_g'][l, 2], W['ln_b'][l, 2])
        for buf, s in zip(bufs, st):
            buf.append(s)
    return (x, *[jnp.stack(b) for b in bufs])


def setup_inputs(seed: int = 0) -> dict:
    key = jax.random.key(seed)
    ks = iter(jax.random.split(key, 32))

    def nrm(shape, s):
        return jax.random.normal(next(ks), shape, jnp.float32) * s

    beta = DEEPNORM_BETA
    return {
        'x_prompt': nrm((BATCH, SEQ, D_MODEL), 1.0),
        'x_sample': nrm((DEC_BATCH, DEC_SEQ, D_MODEL), 1.0),
        'cache_k': nrm((DEPTH, DEC_BATCH, PAST_LEN, HA, 2 * DA), 1.0),
        'cache_v': nrm((DEPTH, DEC_BATCH, PAST_LEN, HA, DVA), 1.0),
        'state_ret': nrm((DEPTH, DEC_BATCH, HB, DKB, DVB), 0.5),
        'state_wkv': nrm((DEPTH, DEC_BATCH, HC, NC, NC), 0.5),
        'state_shift': nrm((DEPTH, DEC_BATCH, 1, C_COLS), 1.0),
        'w_in': nrm((DEPTH, D_MODEL, IN_COLS), D_MODEL ** -0.5),
        'w_gate': nrm((DEPTH, D_MODEL, N_BRANCH * D_MODEL), D_MODEL ** -0.5),
        'w_branch': nrm((DEPTH, N_BRANCH, BRANCH_W, D_MODEL), beta * BRANCH_W ** -0.5),
        'w_out': nrm((DEPTH, D_MODEL, D_MODEL), beta * D_MODEL ** -0.5),
        'ffn_in': nrm((DEPTH, 2, D_MODEL, 2 * D_FF), D_MODEL ** -0.5),
        'ffn_out': nrm((DEPTH, 2, D_FF, D_MODEL), beta * D_FF ** -0.5),
        'ln_g': 1.0 + nrm((DEPTH, 3, D_MODEL), 0.02),
        'ln_b': nrm((DEPTH, 3, D_MODEL), 0.02),
        'diff_lambda': nrm((DEPTH, 4, DA), 0.1),
        'subln_g': 1.0 + nrm((DEPTH, DVA), 0.02),
        'mu_c': jax.random.uniform(next(ks), (DEPTH, C_COLS), jnp.float32),
        'w0': jax.random.uniform(next(ks), (DEPTH, WC), jnp.float32, minval=-6.0, maxval=0.0),
        'w2': nrm((DEPTH, LORA_W, WC), 0.1 * LORA_W ** -0.5),
        'a0': nrm((DEPTH, WC), 0.1),
        'a2': nrm((DEPTH, LORA_A, WC), LORA_A ** -0.5),
        'g2': nrm((DEPTH, LORA_G, WC), LORA_G ** -0.5),
        'k_k': 0.85 + nrm((DEPTH, WC), 0.02),
        'k_a': 1.0 + nrm((DEPTH, WC), 0.02),
        'r_k': nrm((DEPTH, HC, NC), 0.1),
        'lnx_g': 1.0 + nrm((DEPTH, WC), 0.02),
        'lnx_b': nrm((DEPTH, WC), 0.02),
    }


def reference(x_prompt, x_sample, cache_k, cache_v, state_ret, state_wkv, state_shift,
              w_in, w_gate, w_branch, w_out, ffn_in, ffn_out, ln_g, ln_b, diff_lambda, subln_g,
              mu_c, w0, w2, a0, a2, g2, k_k, k_a, r_k, lnx_g, lnx_b):
    W = {'w_in': w_in, 'w_gate': w_gate, 'w_branch': w_branch, 'w_out': w_out,
         'ffn_in': ffn_in, 'ffn_out': ffn_out, 'ln_g': ln_g, 'ln_b': ln_b,
         'diff_lambda': diff_lambda, 'subln_g': subln_g, 'mu_c': mu_c, 'w0': w0, 'w2': w2,
         'a0': a0, 'a2': a2, 'g2': g2, 'k_k': k_k, 'k_a': k_a, 'r_k': r_k,
         'lnx_g': lnx_g, 'lnx_b': lnx_b}
    pos_p = jnp.arange(x_prompt.shape[1], dtype=jnp.int32)
    pos_s = cache_k.shape[2] + jnp.arange(x_sample.shape[1], dtype=jnp.int32)
    y_prompt, k_p, v_p, ret_p, wkv_p, shift_p = run_trunk(x_prompt, pos_p, None, W)
    y_sample, k_s, v_s, ret_s, wkv_s, shift_s = run_trunk(
        x_sample, pos_s, (cache_k, cache_v, state_ret, state_wkv, state_shift), W)
    return (y_prompt, y_sample, k_p, v_p, ret_p, wkv_p, shift_p, k_s, v_s, ret_s, wkv_s, shift_s)
```

```python
import functools
import math

import jax
import jax.numpy as jnp
from jax import lax
from jax.experimental import pallas as pl
from jax.experimental.pallas import tpu as pltpu

F32 = jnp.float32
BF16 = jnp.bfloat16

D_MODEL = 2048
CHUNK = 64
HA, DA, DVA = 8, 64, 128
ROT_A = DA // 4
ROPE_THETA = 500000.0
HB, DKB, DVB = 8, 64, 128
RET_THETA = 10000.0
HC, NC = 16, 64
LORA_W, LORA_A, LORA_G = 64, 64, 160
WA, WB, WC = HA * DVA, HB * DVB, HC * NC
A_COLS = 2 * HA * 2 * DA + HA * DVA
B_COLS = 2 * HB * DKB + HB * DVB + WB
C_COLS = 3 * WC + LORA_W + LORA_A + LORA_G
D_FF = 5504
LN_EPS = 1e-5
SUBLN_EPS = 1e-5
RET_EPS = 1e-6
GN_EPS_C = 64e-5

LANE = 128
D_FF_PAD = 5632
FFN_TF = 512
LORA_PAD = 384
C_PAD = 3 * WC + LORA_PAD
VMEM_LIMIT = 56 * 1024 * 1024


def _cparams(sem):
    return pltpu.CompilerParams(dimension_semantics=sem, vmem_limit_bytes=VMEM_LIMIT)


def _pick(n, cands):
    for c in cands:
        if n % c == 0:
            return c
    raise ValueError(f"no tile for {n}")


def _sigmoid(x):
    return 1.0 / (1.0 + jnp.exp(-x))


def _ln_rows(y, g, b, eps):
    mu = jnp.mean(y, axis=-1, keepdims=True)
    d = y - mu
    var = jnp.mean(d * d, axis=-1, keepdims=True)
    return d * lax.rsqrt(var + eps) * g + b


def _dot(a, b):
    return jnp.dot(a, b, preferred_element_type=F32)


def _dot_nt(a, b):
    return lax.dot_general(a, b, (((1,), (1,)), ((), ())), preferred_element_type=F32)


def _dot_tn(a, b):
    return lax.dot_general(a, b, (((0,), (0,)), ((), ())), preferred_element_type=F32)


def _split2(x):
    hi = x.astype(BF16)
    lo = (x - hi.astype(F32)).astype(BF16)
    return hi, lo


def _split3(x):
    hi = x.astype(BF16)
    r1 = x - hi.astype(F32)
    mid = r1.astype(BF16)
    lo = (r1 - mid.astype(F32)).astype(BF16)
    return hi, mid, lo


def _mm3(a, b, dot=_dot):
    ah, al = a
    bh, bl = b
    return dot(ah, bh) + (dot(ah, bl) + dot(al, bh))


def _ffn_kernel(x_ref, wa_ref, wb_ref, wd_ref, g_ref, b_ref, o_ref, xb_ref, acc_ref, *, alpha):
    j = pl.program_id(1)

    @pl.when(j == 0)
    def _():
        xb_ref[...] = x_ref[...].astype(BF16)
        acc_ref[...] = jnp.zeros_like(acc_ref)

    xb = xb_ref[...]
    ha = _dot(xb, wa_ref[...])
    hb = _dot(xb, wb_ref[...])
    act = (ha * _sigmoid(ha) * hb).astype(BF16)
    acc_ref[...] += _dot(act, wd_ref[...])

    @pl.when(j == pl.num_programs(1) - 1)
    def _():
        y = alpha * x_ref[...] + 0.5 * acc_ref[...]
        o_ref[...] = _ln_rows(y, g_ref[...], b_ref[...], LN_EPS)


def _ffn(x, wa, wb, wd, g, b, alpha):
    n = x.shape[0]
    tm = _pick(n, (512, 256, 128, 64, 32, 16, 8))
    tf = FFN_TF
    grid = (n // tm, D_FF_PAD // tf)
    return pl.pallas_call(
        functools.partial(_ffn_kernel, alpha=alpha),
        grid=grid,
        in_specs=[
            pl.BlockSpec((tm, D_MODEL), lambda i, j: (i, 0)),
            pl.BlockSpec((D_MODEL, tf), lambda i, j: (0, j)),
            pl.BlockSpec((D_MODEL, tf), lambda i, j: (0, j)),
            pl.BlockSpec((tf, D_MODEL), lambda i, j: (j, 0)),
            pl.BlockSpec((1, D_MODEL), lambda i, j: (0, 0)),
            pl.BlockSpec((1, D_MODEL), lambda i, j: (0, 0)),
        ],
        out_specs=pl.BlockSpec((tm, D_MODEL), lambda i, j: (i, 0)),
        out_shape=jax.ShapeDtypeStruct((n, D_MODEL), F32),
        scratch_shapes=[pltpu.VMEM((tm, D_MODEL), BF16), pltpu.VMEM((tm, D_MODEL), F32)],
        compiler_params=_cparams(("parallel", "arbitrary")),
        name="ffn",
    )(x, wa, wb, wd, g, b)


def _proj_kernel(x_ref, w_ref, o_ref, *, act):
    acc = _dot(x_ref[...].astype(BF16), w_ref[...])
    if act == "sigmoid":
        acc = _sigmoid(acc)
    o_ref[...] = acc


def _proj(x, w, tn, act=None, name="proj"):
    n, m = x.shape[0], w.shape[1]
    tm = _pick(n, (512, 256, 128, 64, 32, 16, 8))
    return pl.pallas_call(
        functools.partial(_proj_kernel, act=act),
        grid=(n // tm, m // tn),
        in_specs=[
            pl.BlockSpec((tm, D_MODEL), lambda i, j: (i, 0)),
            pl.BlockSpec((D_MODEL, tn), lambda i, j: (0, j)),
        ],
        out_specs=pl.BlockSpec((tm, tn), lambda i, j: (i, j)),
        out_shape=jax.ShapeDtypeStruct((n, m), F32),
        compiler_params=_cparams(("parallel", "arbitrary")),
        name=name,
    )(x, w)


def _rope_tables(pos, rot, theta, width):
    half = rot // 2
    freq = jnp.power(theta, -jnp.arange(half, dtype=F32) / half)
    ang = pos.astype(F32)[:, None] * freq[None, :]
    cos, sin = jnp.cos(ang), jnp.sin(ang)
    n = pos.shape[0]
    pad = width - rot
    c = jnp.concatenate([cos, cos, jnp.ones((n, pad), F32)], axis=-1)
    s_lo = jnp.concatenate([-sin, jnp.zeros((n, half + pad), F32)], axis=-1)
    s_hi = jnp.concatenate([jnp.zeros((n, half), F32), sin, jnp.zeros((n, pad), F32)], axis=-1)
    rep = LANE // width
    return tuple(jnp.tile(t, (1, rep)) for t in (c, s_lo, s_hi))


def _rope_kernel(xa_ref, xb_ref, ca_ref, la_ref, ha_ref, cb_ref, lb_ref, hb_ref, oa_ref, ob_ref):
    def rot(x, c, lo, hi, half):
        return x * c + pltpu.roll(x, LANE - half, 1) * lo + pltpu.roll(x, half, 1) * hi

    ca, la, ha = ca_ref[...], la_ref[...], ha_ref[...]
    for t in range(2 * HA * 2 * DA // LANE):
        sl = slice(t * LANE, (t + 1) * LANE)
        oa_ref[:, sl] = rot(xa_ref[:, sl], ca, la, ha, ROT_A // 2)
    cb, lb, hb = cb_ref[...], lb_ref[...], hb_ref[...]
    nq = HB * DKB // LANE
    for t in range(2 * nq):
        sl = slice(t * LANE, (t + 1) * LANE)
        y = rot(xb_ref[:, sl], cb, lb, hb, DKB // 2)
        if t >= nq:
            y = y * (DKB ** -0.5)
        ob_ref[:, sl] = y


def _rope(pab, tabs_a, tabs_b):
    n = pab.shape[0]
    tm = _pick(n, (256, 128, 64, 32, 16, 8))
    wa = 2 * HA * 2 * DA
    wb = 2 * HB * DKB
    tab = pl.BlockSpec((tm, LANE), lambda i: (i, 0))
    return pl.pallas_call(
        _rope_kernel,
        grid=(n // tm,),
        in_specs=[
            pl.BlockSpec((tm, wa), lambda i: (i, 0)),
            pl.BlockSpec((tm, wb), lambda i: (i, A_COLS // wb)),
            tab, tab, tab, tab, tab, tab,
        ],
        out_specs=[pl.BlockSpec((tm, wa), lambda i: (i, 0)),
                   pl.BlockSpec((tm, wb), lambda i: (i, 0))],
        out_shape=[jax.ShapeDtypeStruct((n, wa), F32), jax.ShapeDtypeStruct((n, wb), F32)],
        compiler_params=_cparams(("parallel",)),
        name="rope",
    )(pab, pab, *tabs_a, *tabs_b)


def _diff_lambda(dl_ref, lam_init):
    lv = dl_ref[...]
    s1 = jnp.sum(lv[0:1] * lv[1:2], axis=-1, keepdims=True)
    s2 = jnp.sum(lv[2:3] * lv[3:4], axis=-1, keepdims=True)
    return jnp.exp(s1) - jnp.exp(s2) + lam_init


def _split_maps(q):
    lane = lax.broadcasted_iota(jnp.int32, q.shape, 1)
    qs = q * (DA ** -0.5)
    q1 = jnp.where(lane < DA, qs, 0.0).astype(BF16)
    q2 = jnp.where(lane >= DA, qs, 0.0).astype(BF16)
    return q1, q2


def _subln(o, g, scale):
    ms = jnp.mean(o * o, axis=-1, keepdims=True)
    return o * lax.rsqrt(ms + SUBLN_EPS) * g * scale


def _attn_prompt_kernel(q_ref, k_ref, v_ref, dl_ref, g_ref, o_ref, kb_ref, vb_ref, *, tq, lam_init):
    qi = pl.program_id(2)

    @pl.when(qi == 0)
    def _():
        kb_ref[...] = k_ref[...].astype(BF16)
        vb_ref[...] = v_ref[...].astype(BF16)

    lam = _diff_lambda(dl_ref, lam_init)
    q1, q2 = _split_maps(q_ref[...])

    def update(carry, qm, k, v, ok):
        m, l, acc = carry
        s = _dot_nt(qm, k)
        if ok is not None:
            s = jnp.where(ok, s, -1e30)
        m_new = jnp.maximum(m, jnp.max(s, axis=-1, keepdims=True))
        a = jnp.exp(m - m_new)
        p = jnp.exp(s - m_new)
        l_new = a * l + jnp.sum(p, axis=-1, keepdims=True)
        acc_new = a * acc + _dot(p.astype(BF16), v)
        return m_new, l_new, acc_new

    def block(kb, carry, ok):
        start = pl.multiple_of(kb * tq, tq)
        k = kb_ref[pl.ds(start, tq), :]
        v = vb_ref[pl.ds(start, tq), :]
        c1, c2 = carry
        return update(c1, q1, k, v, ok), update(c2, q2, k, v, ok)

    init1 = (jnp.full((tq, 1), -1e30, F32), jnp.zeros((tq, 1), F32), jnp.zeros((tq, DVA), F32))
    carry = lax.fori_loop(0, qi, lambda kb, c: block(kb, c, None), (init1, init1))
    row = lax.broadcasted_iota(jnp.int32, (tq, tq), 0) // CHUNK
    col = lax.broadcasted_iota(jnp.int32, (tq, tq), 1) // CHUNK
    (m1, l1, a1), (m2, l2, a2) = block(qi, carry, col <= row)
    o = a1 / l1 - lam * (a2 / l2)
    o_ref[...] = _subln(o, g_ref[...], 1.0 - lam_init)


def _attn_prompt(qk, pab, dl, g, nb, t, lam_init):
    tq = _pick(t, (256, 128, 64))
    nq = t // tq
    vcol0 = 2 * HA * 2 * DA // DVA
    return pl.pallas_call(
        functools.partial(_attn_prompt_kernel, tq=tq, lam_init=lam_init),
        grid=(nb, HA, nq),
        in_specs=[
            pl.BlockSpec((tq, 2 * DA), lambda b, h, i: (b * nq + i, h)),
            pl.BlockSpec((t, 2 * DA), lambda b, h, i: (b, HA + h)),
            pl.BlockSpec((t, DVA), lambda b, h, i: (b, vcol0 + h)),
            pl.BlockSpec((4, DA), lambda b, h, i: (0, 0)),
            pl.BlockSpec((1, DVA), lambda b, h, i: (0, 0)),
        ],
        out_specs=pl.BlockSpec((tq, DVA), lambda b, h, i: (b * nq + i, h)),
        out_shape=jax.ShapeDtypeStruct((nb * t, WA), F32),
        scratch_shapes=[pltpu.VMEM((t, 2 * DA), BF16), pltpu.VMEM((t, DVA), BF16)],
        compiler_params=_cparams(("parallel", "parallel", "arbitrary")),
        name="attn_prompt",
    )(qk, qk, pab, dl, g)


def _attn_sample_kernel(q_ref, kn_ref, vn_ref, ck_ref, cv_ref, dl_ref, g_ref, o_ref, *, s_len, lam_init):
    lam = _diff_lambda(dl_ref, lam_init)
    g = g_ref[...]
    for h in range(HA):
        sl = slice(h * 2 * DA, (h + 1) * 2 * DA)
        q1, q2 = _split_maps(q_ref[:, sl])
        qq = jnp.concatenate([q1, q2], axis=0)
        kc = ck_ref[:, sl].astype(BF16)
        kn = kn_ref[:, sl].astype(BF16)
        s_c = _dot_nt(qq, kc)
        s_n = _dot_nt(qq, kn)
        m = jnp.maximum(jnp.max(s_c, axis=-1, keepdims=True), jnp.max(s_n, axis=-1, keepdims=True))
        p_c = jnp.exp(s_c - m)
        p_n = jnp.exp(s_n - m)
        l = jnp.sum(p_c, axis=-1, keepdims=True) + jnp.sum(p_n, axis=-1, keepdims=True)
        o = (_dot(p_c.astype(BF16), cv_ref[:, sl].astype(BF16))
             + _dot(p_n.astype(BF16), vn_ref[:, sl].astype(BF16))) / l
        od = o[:s_len] - lam * o[s_len:]
        o_ref[:, sl] = _subln(od, g, 1.0 - lam_init)


def _attn_sample(qk, pab, ck, cv, dl, g, layer, nbs, s_len, row0, lam_init):
    p_len = ck.shape[2]
    rb0 = row0 // s_len
    return pl.pallas_call(
        functools.partial(_attn_sample_kernel, s_len=s_len, lam_init=lam_init),
        grid=(nbs,),
        in_specs=[
            pl.BlockSpec((s_len, WA), lambda b: (rb0 + b, 0)),
            pl.BlockSpec((s_len, WA), lambda b: (rb0 + b, 1)),
            pl.BlockSpec((s_len, WA), lambda b: (rb0 + b, 2)),
            pl.BlockSpec((None, None, p_len, WA), lambda b: (layer, b, 0, 0)),
            pl.BlockSpec((None, None, p_len, WA), lambda b: (layer, b, 0, 0)),
            pl.BlockSpec((4, DA), lambda b: (0, 0)),
            pl.BlockSpec((1, DVA), lambda b: (0, 0)),
        ],
        out_specs=pl.BlockSpec((s_len, WA), lambda b: (b, 0)),
        out_shape=jax.ShapeDtypeStruct((nbs * s_len, WA), F32),
        compiler_params=_cparams(("parallel",)),
        name="attn_sample",
    )(qk, qk, pab, ck, cv, dl, g)


def _ret_kernel(q_ref, k_ref, v_ref, gate_ref, r0_ref, o_ref, rout_ref, st_ref, *, blk):
    c = pl.program_id(1)

    @pl.when(c == 0)
    def _():
        st_ref[...] = jnp.zeros_like(st_ref)
        for h in range(HB):
            e = h % 2
            st_ref[h, e * DKB:(e + 1) * DKB, :] = r0_ref[h]

    n_i = lax.broadcasted_iota(jnp.int32, (blk, blk), 0)
    m_i = lax.broadcasted_iota(jnp.int32, (blk, blk), 1)
    diff = (n_i - m_i).astype(F32)
    causal = n_i >= m_i
    n_col = lax.broadcasted_iota(jnp.int32, (blk, LANE), 0).astype(F32)
    lane = lax.broadcasted_iota(jnp.int32, (blk, LANE), 1)
    for h in range(HB):
        log_g = math.log1p(-2.0 ** (-5.0 - h))
        p, e = h // 2, h % 2
        sl = slice(p * LANE, (p + 1) * LANE)
        mine = (lane >= DKB) if e else (lane < DKB)
        q = jnp.where(mine, q_ref[:, sl], 0.0)
        k = jnp.where(mine, k_ref[:, sl], 0.0)
        vsl = slice(h * DVB, (h + 1) * DVB)
        v = v_ref[:, vsl].astype(BF16)
        qb = q.astype(BF16)
        dec = jnp.where(causal, jnp.exp(jnp.maximum(diff, 0.0) * log_g), 0.0)
        s = _dot_nt(qb, k.astype(BF16)) * dec
        intra = _dot(s.astype(BF16), v)
        r_old = st_ref[h]
        cross = _dot(qb, r_old.astype(BF16)) * jnp.exp((n_col + 1.0) * log_g)
        kd = (k * jnp.exp((blk - 1.0 - n_col) * log_g)).astype(BF16)
        r_new = math.exp(blk * log_g) * r_old + _dot_tn(kd, v)
        st_ref[h] = r_new
        o = intra + cross
        ms = jnp.mean(o * o, axis=-1, keepdims=True)
        gt = gate_ref[:, vsl]
        o_ref[:, vsl] = o * lax.rsqrt(ms + RET_EPS) * (gt * _sigmoid(gt))

    @pl.when(c == pl.num_programs(1) - 1)
    def _():
        for h in range(HB):
            e = h % 2
            rout_ref[h] = st_ref[h, e * DKB:(e + 1) * DKB, :]


def _retention(qkb, pab, r0, nseg, seg_len, row0, blk):
    nc = seg_len // blk
    rb0 = row0 // blk
    vcol = (A_COLS + 2 * HB * DKB) // WB
    row = lambda b, c: rb0 + b * nc + c
    return pl.pallas_call(
        functools.partial(_ret_kernel, blk=blk),
        grid=(nseg, nc),
        in_specs=[
            pl.BlockSpec((blk, HB * DKB), lambda b, c: (row(b, c), 0)),
            pl.BlockSpec((blk, HB * DKB), lambda b, c: (row(b, c), 1)),
            pl.BlockSpec((blk, WB), lambda b, c: (row(b, c), vcol)),
            pl.BlockSpec((blk, WB), lambda b, c: (row(b, c), vcol + 1)),
            pl.BlockSpec((None, HB, DKB, DVB), lambda b, c: (b, 0, 0, 0)),
        ],
        out_specs=[
            pl.BlockSpec((blk, WB), lambda b, c: (b * nc + c, 0)),
            pl.BlockSpec((None, HB, DKB, DVB), lambda b, c: (b, 0, 0, 0)),
        ],
        out_shape=[jax.ShapeDtypeStruct((nseg * seg_len, WB), F32),
                   jax.ShapeDtypeStruct((nseg, HB, DKB, DVB), F32)],
        scratch_shapes=[pltpu.VMEM((HB, 2 * DKB, DVB), F32)],
        compiler_params=_cparams(("parallel", "arbitrary")),
        name="retention",
    )(qkb, qkb, pab, pab, r0)


def _head_sum(x, ones_bd):
    hi, lo = _split2(x)
    return _dot(hi, ones_bd) + _dot(lo, ones_bd)


def _wkv_pre_kernel(pc_ref, prev_ref, mu_ref, w0_ref, a0_ref, kk_ref, ka_ref, rk_ref,
                    w2_ref, a2_ref, g2_ref, ones_ref,
                    r_ref, lw_ref, k_ref, v_ref, an_ref, bn_ref, g_ref, bonus_ref, last_ref, carry_ref):
    c = pl.program_id(1)

    @pl.when(c == 0)
    def _():
        carry_ref[...] = prev_ref[...]

    pc = pc_ref[...]
    tb = pc.shape[0]
    row = lax.broadcasted_iota(jnp.int32, pc.shape, 0)
    shifted = jnp.where(row == 0, carry_ref[...], pltpu.roll(pc, 1, 0))
    carry_ref[...] = pc[tb - 1:tb, :]
    last_ref[...] = pc[tb - 1:tb, :]
    pcm = pc + mu_ref[...] * (shifted - pc)
    rc = pcm[:, 0:WC]
    kc = pcm[:, WC:2 * WC]
    vc = pcm[:, 2 * WC:3 * WC]
    lora = pcm[:, 3 * WC:C_PAD]
    z = w0_ref[...] + _dot(jnp.tanh(lora).astype(BF16), w2_ref[...])
    u = -z
    softplus = jnp.maximum(u, 0.0) + jnp.log1p(jnp.exp(-jnp.abs(u)))
    w_log = -softplus - 0.5
    a = _sigmoid(a0_ref[...] + _dot(lora.astype(BF16), a2_ref[...]))
    g = _dot(_sigmoid(lora).astype(BF16), g2_ref[...])
    kk = kc * kk_ref[...]
    kc2 = kc * (1.0 + (a - 1.0) * ka_ref[...])
    rkr = rc * kc2 * rk_ref[...]
    ones_bd = ones_ref[...]
    for t in range(WC // LANE):
        sl = slice(t * LANE, (t + 1) * LANE)
        kkt = kk[:, sl]
        nrm = jnp.maximum(jnp.sqrt(_head_sum(kkt * kkt, ones_bd)), 1e-12)
        kn = kkt / nrm
        an_ref[:, sl] = -kn
        bn_ref[:, sl] = kn * a[:, sl]
        bonus_ref[:, sl] = _head_sum(rkr[:, sl], ones_bd) * vc[:, sl]
    r_ref[...] = rc
    lw_ref[...] = -jnp.exp(w_log)
    k_ref[...] = kc2
    v_ref[...] = vc
    g_ref[...] = g


def _wkv_pre(pc, prev, vecs, mats, ones_bd, nseg, seg_len, row0, tb):
    nc = seg_len // tb
    rb0 = row0 // tb
    vec = lambda w: pl.BlockSpec((1, w), lambda b, c: (0, 0))
    mat = pl.BlockSpec((LORA_PAD, WC), lambda b, c: (0, 0))
    out = pl.BlockSpec((tb, WC), lambda b, c: (b * nc + c, 0))
    n = nseg * seg_len
    outs = pl.pallas_call(
        _wkv_pre_kernel,
        grid=(nseg, nc),
        in_specs=[
            pl.BlockSpec((tb, C_PAD), lambda b, c: (rb0 + b * nc + c, 0)),
            pl.BlockSpec((None, 1, C_PAD), lambda b, c: (b, 0, 0)),
            vec(C_PAD), vec(WC), vec(WC), vec(WC), vec(WC), vec(WC),
            mat, mat, mat,
            pl.BlockSpec((LANE, LANE), lambda b, c: (0, 0)),
        ],
        out_specs=[out] * 8 + [pl.BlockSpec((None, 1, C_PAD), lambda b, c: (b, 0, 0))],
        out_shape=[jax.ShapeDtypeStruct((n, WC), F32)] * 8
                  + [jax.ShapeDtypeStruct((nseg, 1, C_PAD), F32)],
        scratch_shapes=[pltpu.VMEM((1, C_PAD), F32)],
        compiler_params=_cparams(("parallel", "arbitrary")),
        name="wkv_pre",
    )(pc, prev, *vecs, *mats, ones_bd)
    return outs


def _wkv_kernel(r_ref, lw_ref, k_ref, v_ref, a_ref, b_ref, g_ref, bonus_ref, s0_ref,
                lng_ref, lnb_ref, tri_ref, ones_ref,
                o_ref, sout_ref, st_ref, *, ck):
    c = pl.program_id(1)

    @pl.when(c == 0)
    def _():
        st_ref[...] = s0_ref[...]

    lw = lw_ref[...]
    tri = tri_ref[...]
    l1, l2, l3 = _split3(lw)
    cum = _dot(tri, l1) + (_dot(tri, l2) + _dot(tri, l3))
    total = cum[ck - 1:ck, :]
    e_pos = jnp.exp(cum)
    e_neg = jnp.exp(-cum)
    e_rem = jnp.exp(total - cum)
    a_t = a_ref[...] * jnp.exp(cum - lw)
    r_t = r_ref[...] * e_pos
    b_t = b_ref[...] * e_neg
    k_t = k_ref[...] * e_neg
    b_p = b_ref[...] * e_rem
    k_p = k_ref[...] * e_rem
    p_tot = jnp.exp(total)

    ones_bd = ones_ref[...]
    lane = lax.broadcasted_iota(jnp.int32, (ck, LANE), 1)
    m0 = lane < NC

    def stack(x):
        return jnp.concatenate([jnp.where(m0, x, 0.0), jnp.where(m0, 0.0, x)], axis=0)

    n2 = 2 * ck
    ri = lax.broadcasted_iota(jnp.int32, (n2, n2), 0)
    ci = lax.broadcasted_iota(jnp.int32, (n2, n2), 1)
    same = (ri // ck) == (ci // ck)
    strict = jnp.logical_and(same, ri > ci)
    incl = jnp.logical_and(same, ri >= ci)

    for p in range(HC // 2):
        sl = slice(p * LANE, (p + 1) * LANE)
        a_s, r_s = _split2(stack(a_t[:, sl])), _split2(stack(r_t[:, sl]))
        b_s, k_s = _split2(stack(b_t[:, sl])), _split2(stack(k_t[:, sl]))
        v_s = _split2(stack(v_ref[:, sl]))
        bp_s, kp_s = _split2(stack(b_p[:, sl])), _split2(stack(k_p[:, sl]))
        s_old = st_ref[p]
        s_sp = _split2(s_old)
        l_ab = jnp.where(strict, _mm3(a_s, b_s, _dot_nt), 0.0)
        l_ak = jnp.where(strict, _mm3(a_s, k_s, _dot_nt), 0.0)
        l_rb = jnp.where(incl, _mm3(r_s, b_s, _dot_nt), 0.0)
        l_rk = jnp.where(incl, _mm3(r_s, k_s, _dot_nt), 0.0)
        x = _mm3(a_s, s_sp, _dot_nt) + _mm3(_split2(l_ak), v_s)
        lk = _split2(l_ab)
        span = 1
        while span < ck:
            x = x + _mm3(lk, _split2(x))
            span *= 2
            if span < ck:
                lk = _split2(_mm3(lk, lk))
        x_sp = _split2(x)
        y_st = _mm3(r_s, s_sp, _dot_nt) + _mm3(_split2(l_rb), x_sp) + _mm3(_split2(l_rk), v_s)
        y = y_st[0:ck] + y_st[ck:n2]
        st_ref[p] = s_old * p_tot[:, sl] + _mm3(x_sp, bp_s, _dot_tn) + _mm3(v_s, kp_s, _dot_tn)
        mean = _head_sum(y, ones_bd) * (1.0 / NC)
        d = y - mean
        var = _head_sum(d * d, ones_bd) * (1.0 / NC)
        yn = d * lax.rsqrt(var + GN_EPS_C) * lng_ref[:, sl] + lnb_ref[:, sl]
        o_ref[:, sl] = (yn + bonus_ref[:, sl]) * g_ref[:, sl]

    @pl.when(c == pl.num_programs(1) - 1)
    def _():
        sout_ref[...] = st_ref[...]


def _wkv(pre, s0, lng, lnb, tri, ones_bd, nseg, seg_len, ck):
    nc = seg_len // ck
    blk = pl.BlockSpec((ck, WC), lambda b, c: (b * nc + c, 0))
    st = pl.BlockSpec((None, HC // 2, LANE, LANE), lambda b, c: (b, 0, 0, 0))
    vec = pl.BlockSpec((1, WC), lambda b, c: (0, 0))
    n = nseg * seg_len
    return pl.pallas_call(
        functools.partial(_wkv_kernel, ck=ck),
        grid=(nseg, nc),
        in_specs=[blk] * 8 + [st, vec, vec,
                              pl.BlockSpec((ck, ck), lambda b, c: (0, 0)),
                              pl.BlockSpec((LANE, LANE), lambda b, c: (0, 0))],
        out_specs=[blk, st],
        out_shape=[jax.ShapeDtypeStruct((n, WC), F32),
                   jax.ShapeDtypeStruct((nseg, HC // 2, LANE, LANE), F32)],
        scratch_shapes=[pltpu.VMEM((HC // 2, LANE, LANE), F32)],
        compiler_params=_cparams(("parallel", "arbitrary")),
        name="wkv",
    )(*pre, s0, lng, lnb, tri, ones_bd)


def _state_to_blockdiag(s):
    b = s.shape[0]
    sp = s.reshape(b, HC // 2, 2, NC, NC)
    z = jnp.zeros_like(sp[:, :, 0])
    top = jnp.concatenate([sp[:, :, 0], z], axis=-1)
    bot = jnp.concatenate([z, sp[:, :, 1]], axis=-1)
    return jnp.concatenate([top, bot], axis=-2)


def _blockdiag_to_state(sb):
    b = sb.shape[0]
    h0 = sb[:, :, :NC, :NC]
    h1 = sb[:, :, NC:, NC:]
    return jnp.stack([h0, h1], axis=2).reshape(b, HC, NC, NC)


def _merge_kernel(oa_ref, ob_ref, oc_ref, ga_ref, gb_ref, gc_ref, wa_ref, wb_ref, wc_ref, o_ref):
    o_ref[...] = (ga_ref[...] * _dot(oa_ref[...].astype(BF16), wa_ref[...])
                  + gb_ref[...] * _dot(ob_ref[...].astype(BF16), wb_ref[...])
                  + gc_ref[...] * _dot(oc_ref[...].astype(BF16), wc_ref[...]))


def _merge(oa, ob, oc, gates, wbr):
    n = oa.shape[0]
    tm = _pick(n, (512, 256, 128, 64, 32, 16, 8))
    tn = 512
    nj = D_MODEL // tn
    bw = oa.shape[1]
    o_spec = pl.BlockSpec((tm, bw), lambda i, j: (i, 0))
    g_spec = lambda k: pl.BlockSpec((tm, tn), lambda i, j: (i, k * nj + j))
    w_spec = lambda k: pl.BlockSpec((None, bw, tn), lambda i, j: (k, 0, j))
    return pl.pallas_call(
        _merge_kernel,
        grid=(n // tm, nj),
        in_specs=[o_spec, o_spec, o_spec, g_spec(0), g_spec(1), g_spec(2),
                  w_spec(0), w_spec(1), w_spec(2)],
        out_specs=pl.BlockSpec((tm, tn), lambda i, j: (i, j)),
        out_shape=jax.ShapeDtypeStruct((n, D_MODEL), F32),
        compiler_params=_cparams(("parallel", "arbitrary")),
        name="merge",
    )(oa, ob, oc, gates, gates, gates, wbr, wbr, wbr)


def _outproj_kernel(m_ref, x_ref, w_ref, g_ref, b_ref, o_ref, *, alpha):
    y = alpha * x_ref[...] + _dot(m_ref[...].astype(BF16), w_ref[...])
    o_ref[...] = _ln_rows(y, g_ref[...], b_ref[...], LN_EPS)


def _outproj(merged, x, w, g, b, alpha):
    n = x.shape[0]
    tm = _pick(n, (256, 128, 64, 32, 16, 8))
    row = pl.BlockSpec((tm, D_MODEL), lambda i: (i, 0))
    vec = pl.BlockSpec((1, D_MODEL), lambda i: (0, 0))
    return pl.pallas_call(
        functools.partial(_outproj_kernel, alpha=alpha),
        grid=(n // tm,),
        in_specs=[row, row, pl.BlockSpec((D_MODEL, D_MODEL), lambda i: (0, 0)), vec, vec],
        out_specs=row,
        out_shape=jax.ShapeDtypeStruct((n, D_MODEL), F32),
        compiler_params=_cparams(("parallel",)),
        name="outproj",
    )(merged, x, w, g, b)


def kernel(x_prompt, x_sample, cache_k, cache_v, state_ret, state_wkv, state_shift, w_in, w_gate,
           w_branch, w_out, ffn_in, ffn_out, ln_g, ln_b, diff_lambda, subln_g, mu_c, w0, w2, a0, a2,
           g2, k_k, k_a, r_k, lnx_g, lnx_b):
    nb, t, _ = x_prompt.shape
    nbs, s_len, _ = x_sample.shape
    depth = w_in.shape[0]
    p_len = cache_k.shape[2]
    n_p = nb * t
    alpha = (2 * depth) ** 0.25

    x = jnp.concatenate([x_prompt.reshape(n_p, D_MODEL), x_sample.reshape(nbs * s_len, D_MODEL)], axis=0)

    pos = jnp.concatenate([jnp.tile(jnp.arange(t, dtype=jnp.int32), nb),
                           jnp.tile(p_len + jnp.arange(s_len, dtype=jnp.int32), nbs)])
    tabs_a = _rope_tables(pos, ROT_A, ROPE_THETA, DA)
    tabs_b = _rope_tables(pos, DKB, RET_THETA, DKB)

    ck_all = cache_k.reshape(depth, nbs, p_len, WA)
    cv_all = cache_v.reshape(depth, nbs, p_len, WA)

    eye = jnp.arange(LANE) // NC
    ones_bd = (eye[:, None] == eye[None, :]).astype(BF16)
    ck_p = _pick(t, (CHUNK,))
    ck_s = s_len
    tri_p = (jnp.arange(ck_p)[:, None] >= jnp.arange(ck_p)[None, :]).astype(BF16)
    tri_s = (jnp.arange(ck_s)[:, None] >= jnp.arange(ck_s)[None, :]).astype(BF16)
    ret_blk = _pick(t, (256, 128, 64))
    pre_tb = _pick(t, (256, 128, 64))

    outs = {k: [] for k in ("kp", "vp", "rp", "wp", "sp", "ks", "vs", "rs", "ws", "ss")}
    pad_f = D_FF_PAD - D_FF

    for l in range(depth):
        lam_init = 0.8 - 0.6 * math.exp(-0.3 * l)

        def ffn_weights(i):
            wa = jnp.pad(ffn_in[l, i, :, :D_FF].astype(BF16), ((0, 0), (0, pad_f)))
            wb = jnp.pad(ffn_in[l, i, :, D_FF:].astype(BF16), ((0, 0), (0, pad_f)))
            wd = jnp.pad(ffn_out[l, i].astype(BF16), ((0, pad_f), (0, 0)))
            return wa, wb, wd

        x = _ffn(x, *ffn_weights(0), ln_g[l, 0][None], ln_b[l, 0][None], alpha)

        w_ab = w_in[l, :, :A_COLS + B_COLS].astype(BF16)
        w_c = jnp.pad(w_in[l, :, A_COLS + B_COLS:].astype(BF16), ((0, 0), (0, C_PAD - C_COLS)))
        pab = _proj(x, w_ab, 512, name="proj_ab")
        pc = _proj(x, w_c, 384, name="proj_c")
        gates = _proj(x, w_gate[l].astype(BF16), 512, act="sigmoid", name="proj_gate")
        qka, qkb = _rope(pab, tabs_a, tabs_b)

        dl = diff_lambda[l]
        sg = subln_g[l][None]
        oa_p = _attn_prompt(qka, pab, dl, sg, nb, t, lam_init)
        oa_s = _attn_sample(qka, pab, ck_all, cv_all, dl, sg, l, nbs, s_len, n_p, lam_init)

        ob_p, ret_p = _retention(qkb, pab, jnp.zeros((nb, HB, DKB, DVB), F32), nb, t, 0, ret_blk)
        ob_s, ret_s = _retention(qkb, pab, state_ret[l], nbs, s_len, n_p, s_len)

        padc = C_PAD - C_COLS
        vecs = (jnp.pad(mu_c[l], (0, padc))[None], w0[l][None], a0[l][None], k_k[l][None],
                k_a[l][None], r_k[l].reshape(1, WC))
        zrow = lambda r0, r1, m: jnp.pad(m.astype(BF16), ((r0, LORA_PAD - r1), (0, 0)))
        mats = (zrow(0, LORA_W, w2[l]), zrow(LORA_W, LORA_W + LORA_A, a2[l]),
                zrow(LORA_W + LORA_A, LORA_W + LORA_A + LORA_G, g2[l]))
        prev_p = jnp.zeros((nb, 1, C_PAD), F32)
        prev_s = jnp.pad(state_shift[l], ((0, 0), (0, 0), (0, padc)))
        pre_p = _wkv_pre(pc, prev_p, vecs, mats, ones_bd, nb, t, 0, pre_tb)
        pre_s = _wkv_pre(pc, prev_s, vecs, mats, ones_bd, nbs, s_len, n_p, s_len)
        lng, lnb = lnx_g[l][None], lnx_b[l][None]
        oc_p, wkv_p = _wkv(pre_p[:8], jnp.zeros((nb, HC // 2, LANE, LANE), F32), lng, lnb, tri_p,
                           ones_bd, nb, t, ck_p)
        oc_s, wkv_s = _wkv(pre_s[:8], _state_to_blockdiag(state_wkv[l]), lng, lnb, tri_s,
                           ones_bd, nbs, s_len, ck_s)

        oa = jnp.concatenate([oa_p, oa_s], axis=0)
        ob = jnp.concatenate([ob_p, ob_s], axis=0)
        oc = jnp.concatenate([oc_p, oc_s], axis=0)
        merged = _merge(oa, ob, oc, gates, w_branch[l].astype(BF16))
        x = _outproj(merged, x, w_out[l].astype(BF16), ln_g[l, 1][None], ln_b[l, 1][None], alpha)

        x = _ffn(x, *ffn_weights(1), ln_g[l, 2][None], ln_b[l, 2][None], alpha)

        k_new = qka[:, HA * 2 * DA:]
        v_new = pab[:, 2 * HA * 2 * DA:A_COLS]
        outs["kp"].append(k_new[:n_p].reshape(nb, t, HA, 2 * DA))
        outs["vp"].append(v_new[:n_p].reshape(nb, t, HA, DVA))
        outs["ks"].append(k_new[n_p:].reshape(nbs, s_len, HA, 2 * DA))
        outs["vs"].append(v_new[n_p:].reshape(nbs, s_len, HA, DVA))
        outs["rp"].append(ret_p)
        outs["rs"].append(ret_s)
        outs["wp"].append(_blockdiag_to_state(wkv_p))
        outs["ws"].append(_blockdiag_to_state(wkv_s))
        outs["sp"].append(pre_p[8][:, :, :C_COLS])
        outs["ss"].append(pre_s[8][:, :, :C_COLS])

    st = {k: jnp.stack(v) for k, v in outs.items()}
    y_p = x[:n_p].reshape(nb, t, D_MODEL)
    y_s = x[n_p:].reshape(nbs, s_len, D_MODEL)
    return (y_p, y_s, st["kp"], st["vp"], st["rp"], st["wp"], st["sp"],
            st["ks"], st["vs"], st["rs"], st["ws"], st["ss"])
```

```python
import functools
import math

import jax
import jax.numpy as jnp
from jax import lax
from jax.experimental import pallas as pl
from jax.experimental.pallas import tpu as pltpu

F32 = jnp.float32
BF16 = jnp.bfloat16

D_MODEL = 2048
CHUNK = 64
HA, DA, DVA = 8, 64, 128
ROT_A = DA // 4
ROPE_THETA = 500000.0
HB, DKB, DVB = 8, 64, 128
RET_THETA = 10000.0
HC, NC = 16, 64
LORA_W, LORA_A, LORA_G = 64, 64, 160
WA, WB, WC = HA * DVA, HB * DVB, HC * NC
A_COLS = 2 * HA * 2 * DA + HA * DVA
B_COLS = 2 * HB * DKB + HB * DVB + WB
C_COLS = 3 * WC + LORA_W + LORA_A + LORA_G
D_FF = 5504
LN_EPS = 1e-5
SUBLN_EPS = 1e-5
RET_EPS = 1e-6
GN_EPS_C = 64e-5

LANE = 128
D_FF_PAD = 5632
FFN_TF = 512
LORA_PAD = 384
C_PAD = 3 * WC + LORA_PAD
VMEM_LIMIT = 56 * 1024 * 1024


def _cparams(sem):
    return pltpu.CompilerParams(dimension_semantics=sem, vmem_limit_bytes=VMEM_LIMIT)


def _pick(n, cands):
    for c in cands:
        if n % c == 0:
            return c
    raise ValueError(f"no tile for {n}")


def _sigmoid(x):
    return 1.0 / (1.0 + jnp.exp(-x))


def _ln_rows(y, g, b, eps):
    mu = jnp.mean(y, axis=-1, keepdims=True)
    d = y - mu
    var = jnp.mean(d * d, axis=-1, keepdims=True)
    return d * lax.rsqrt(var + eps) * g + b


def _dot(a, b):
    return jnp.dot(a, b, preferred_element_type=F32)


def _dot_nt(a, b):
    return lax.dot_general(a, b, (((1,), (1,)), ((), ())), preferred_element_type=F32)


def _dot_tn(a, b):
    return lax.dot_general(a, b, (((0,), (0,)), ((), ())), preferred_element_type=F32)


def _split2(x):
    hi = x.astype(BF16)
    lo = (x - hi.astype(F32)).astype(BF16)
    return hi, lo


def _split3(x):
    hi = x.astype(BF16)
    r1 = x - hi.astype(F32)
    mid = r1.astype(BF16)
    lo = (r1 - mid.astype(F32)).astype(BF16)
    return hi, mid, lo


def _mm3(a, b, dot=_dot):
    ah, al = a
    bh, bl = b
    return dot(ah, bh) + (dot(ah, bl) + dot(al, bh))


def _ffn_kernel(x_ref, xb_ref, wa_ref, wb_ref, wd_ref, g_ref, b_ref, o_ref, ob_ref, acc_ref, *, alpha):
    j = pl.program_id(1)

    @pl.when(j == 0)
    def _():
        acc_ref[...] = jnp.zeros_like(acc_ref)

    xb = xb_ref[...]
    ha = _dot(xb, wa_ref[...])
    hb = _dot(xb, wb_ref[...])
    act = (ha * _sigmoid(ha) * hb).astype(BF16)
    acc_ref[...] += _dot(act, wd_ref[...])

    @pl.when(j == pl.num_programs(1) - 1)
    def _():
        y = alpha * x_ref[...] + 0.5 * acc_ref[...]
        out = _ln_rows(y, g_ref[...], b_ref[...], LN_EPS)
        o_ref[...] = out
        ob_ref[...] = out.astype(BF16)


def _ffn(x, xb, wa, wb, wd, g, b, alpha):
    n = x.shape[0]
    tm = _pick(n, (512, 256, 128, 64, 32, 16, 8))
    tf = FFN_TF
    grid = (n // tm, D_FF_PAD // tf)
    row = pl.BlockSpec((tm, D_MODEL), lambda i, j: (i, 0))
    vec = pl.BlockSpec((1, D_MODEL), lambda i, j: (0, 0))
    return pl.pallas_call(
        functools.partial(_ffn_kernel, alpha=alpha),
        grid=grid,
        in_specs=[
            row, row,
            pl.BlockSpec((D_MODEL, tf), lambda i, j: (0, j)),
            pl.BlockSpec((D_MODEL, tf), lambda i, j: (0, j)),
            pl.BlockSpec((tf, D_MODEL), lambda i, j: (j, 0)),
            vec, vec,
        ],
        out_specs=[row, row],
        out_shape=[jax.ShapeDtypeStruct((n, D_MODEL), F32), jax.ShapeDtypeStruct((n, D_MODEL), BF16)],
        scratch_shapes=[pltpu.VMEM((tm, D_MODEL), F32)],
        compiler_params=_cparams(("parallel", "arbitrary")),
        name="ffn",
    )(x, xb, wa, wb, wd, g, b)


def _proj_kernel(x_ref, w_ref, o_ref, *, act):
    acc = _dot(x_ref[...], w_ref[...])
    if act == "sigmoid":
        acc = _sigmoid(acc)
    o_ref[...] = acc


def _proj(xb, w, tn, act=None, name="proj"):
    n, m = xb.shape[0], w.shape[1]
    tm = _pick(n, (1024, 512, 256, 128, 64, 32, 16, 8))
    return pl.pallas_call(
        functools.partial(_proj_kernel, act=act),
        grid=(n // tm, m // tn),
        in_specs=[
            pl.BlockSpec((tm, D_MODEL), lambda i, j: (i, 0)),
            pl.BlockSpec((D_MODEL, tn), lambda i, j: (0, j)),
        ],
        out_specs=pl.BlockSpec((tm, tn), lambda i, j: (i, j)),
        out_shape=jax.ShapeDtypeStruct((n, m), F32),
        compiler_params=_cparams(("parallel", "arbitrary")),
        name=name,
    )(xb, w)


def _rope_tables(pos, rot, theta, width):
    half = rot // 2
    freq = jnp.power(theta, -jnp.arange(half, dtype=F32) / half)
    ang = pos.astype(F32)[:, None] * freq[None, :]
    cos, sin = jnp.cos(ang), jnp.sin(ang)
    n = pos.shape[0]
    pad = width - rot
    c = jnp.concatenate([cos, cos, jnp.ones((n, pad), F32)], axis=-1)
    s_lo = jnp.concatenate([-sin, jnp.zeros((n, half + pad), F32)], axis=-1)
    s_hi = jnp.concatenate([jnp.zeros((n, half), F32), sin, jnp.zeros((n, pad), F32)], axis=-1)
    rep = LANE // width
    return tuple(jnp.tile(t, (1, rep)) for t in (c, s_lo, s_hi))


def _rope_kernel(xa_ref, xv_ref, xb_ref, ca_ref, la_ref, ha_ref, cb_ref, lb_ref, hb_ref,
                 qa_buf, kall_buf, vall_buf, qkb_buf, qa_ref, k_ref, v_ref, qkb_ref):
    del qa_buf, kall_buf, vall_buf, qkb_buf

    def rot(x, c, lo, hi, half):
        return x * c + pltpu.roll(x, LANE - half, 1) * lo + pltpu.roll(x, half, 1) * hi

    ca, la, ha = ca_ref[...], la_ref[...], ha_ref[...]
    nqa = HA * 2 * DA // LANE
    for t in range(2 * nqa):
        y = rot(xa_ref[:, t * LANE:(t + 1) * LANE], ca, la, ha, ROT_A // 2)
        if t < nqa:
            qa_ref[:, t * LANE:(t + 1) * LANE] = (y * (DA ** -0.5)).astype(BF16)
        else:
            k_ref[:, (t - nqa) * LANE:(t - nqa + 1) * LANE] = y
    v_ref[...] = xv_ref[...]
    cb, lb, hb = cb_ref[...], lb_ref[...], hb_ref[...]
    nqb = HB * DKB // LANE
    for t in range(2 * nqb):
        sl = slice(t * LANE, (t + 1) * LANE)
        y = rot(xb_ref[:, sl], cb, lb, hb, DKB // 2)
        if t >= nqb:
            y = y * (DKB ** -0.5)
        qkb_ref[:, sl] = y


def _rope(pab, tabs_a, tabs_b, qa_buf, kall_buf, vall_buf, qkb_buf, row0, nrows, kv_row0):
    tm = _pick(nrows, (256, 128, 64, 32, 16, 8))
    rb0, kb0 = row0 // tm, kv_row0 // tm
    wqk = 2 * HA * 2 * DA
    wb = 2 * HB * DKB
    tab = pl.BlockSpec((tm, LANE), lambda i: (rb0 + i, 0))
    anyspec = pl.BlockSpec(memory_space=pl.ANY)
    bufs = (qa_buf, kall_buf, vall_buf, qkb_buf)
    return pl.pallas_call(
        _rope_kernel,
        grid=(nrows // tm,),
        in_specs=[
            pl.BlockSpec((tm, wqk), lambda i: (rb0 + i, 0)),
            pl.BlockSpec((tm, WA), lambda i: (rb0 + i, wqk // WA)),
            pl.BlockSpec((tm, wb), lambda i: (rb0 + i, A_COLS // wb)),
            tab, tab, tab, tab, tab, tab,
            anyspec, anyspec, anyspec, anyspec,
        ],
        out_specs=[
            pl.BlockSpec((tm, WA), lambda i: (rb0 + i, 0)),
            pl.BlockSpec((tm, WA), lambda i: (kb0 + i, 0)),
            pl.BlockSpec((tm, WA), lambda i: (kb0 + i, 0)),
            pl.BlockSpec((tm, wb), lambda i: (rb0 + i, 0)),
        ],
        out_shape=[jax.ShapeDtypeStruct(b.shape, b.dtype) for b in bufs],
        input_output_aliases={9: 0, 10: 1, 11: 2, 12: 3},
        compiler_params=_cparams(("parallel",)),
        name="rope",
    )(pab, pab, pab, *tabs_a, *tabs_b, *bufs)


def _diff_lambda(dl_ref, lam_init):
    lv = dl_ref[...]
    s1 = jnp.sum(lv[0:1] * lv[1:2], axis=-1, keepdims=True)
    s2 = jnp.sum(lv[2:3] * lv[3:4], axis=-1, keepdims=True)
    return jnp.exp(s1) - jnp.exp(s2) + lam_init


def _split_maps(q):
    lane = lax.broadcasted_iota(jnp.int32, q.shape, 1)
    zero = jnp.zeros_like(q)
    return jnp.where(lane < DA, q, zero), jnp.where(lane >= DA, q, zero)


def _subln(o, g, scale):
    ms = jnp.mean(o * o, axis=-1, keepdims=True)
    return o * lax.rsqrt(ms + SUBLN_EPS) * g * scale


def _attn_prompt_kernel(q_ref, k_ref, v_ref, dl_ref, g_ref, o_buf, o_ref, kb_ref, vb_ref, *,
                        tq, tk, lam_init):
    del o_buf
    qi = pl.program_id(2)

    @pl.when(qi == 0)
    def _():
        kb_ref[...] = k_ref[...].astype(BF16)
        vb_ref[...] = v_ref[...].astype(BF16)

    lam = _diff_lambda(dl_ref, lam_init)
    qs = _split_maps(q_ref[...])

    def step(start, carries, ok):
        k = kb_ref[pl.ds(start, tk), :]
        v = vb_ref[pl.ds(start, tk), :]
        ss = [_dot_nt(qm, k) for qm in qs]
        if ok is not None:
            ss = [jnp.where(ok, s, -1e30) for s in ss]
        stats = []
        for (m, l, acc), s in zip(carries, ss):
            m_new = jnp.maximum(m, jnp.max(s, axis=-1, keepdims=True))
            a = jnp.exp(m - m_new)
            p = jnp.exp(s - m_new)
            stats.append((m_new, a * l + jnp.sum(p, axis=-1, keepdims=True), a, p.astype(BF16)))
        pvs = [_dot(st[3], v) for st in stats]
        return tuple((st[0], st[1], st[2] * c[2] + pv) for st, c, pv in zip(stats, carries, pvs))

    q0 = qi * tq
    n_full = q0 // tk
    init1 = (jnp.full((tq, 1), -1e30, F32), jnp.zeros((tq, 1), F32), jnp.zeros((tq, DVA), F32))
    carry = lax.fori_loop(0, n_full, lambda kb, c: step(pl.multiple_of(kb * tk, tk), c, None),
                          (init1, init1))
    tail = pl.multiple_of(n_full * tk, tk)
    row = lax.broadcasted_iota(jnp.int32, (tq, tk), 0) // CHUNK
    col = lax.broadcasted_iota(jnp.int32, (tq, tk), 1) // CHUNK
    ok = (col - row) <= (q0 - tail) // CHUNK
    (m1, l1, a1), (m2, l2, a2) = step(tail, carry, ok)
    o = a1 / l1 - lam * (a2 / l2)
    o_ref[...] = _subln(o, g_ref[...], 1.0 - lam_init).astype(BF16)


def _attn_prompt(qa, k_all, v_all, dl, g, o_buf, layer, nb, t, lam_init):
    tq = _pick(t, (256, 128, 64))
    tk = _pick(t, (1024, 512, 256, 128, 64))
    nq = t // tq
    kv = pl.BlockSpec((t, 2 * DA), lambda b, h, i: (layer * nb + b, h))
    return pl.pallas_call(
        functools.partial(_attn_prompt_kernel, tq=tq, tk=tk, lam_init=lam_init),
        grid=(nb, HA, nq),
        in_specs=[
            pl.BlockSpec((tq, 2 * DA), lambda b, h, i: (b * nq + i, h)),
            kv, kv,
            pl.BlockSpec((4, DA), lambda b, h, i: (0, 0)),
            pl.BlockSpec((1, DVA), lambda b, h, i: (0, 0)),
            pl.BlockSpec(memory_space=pl.ANY),
        ],
        out_specs=pl.BlockSpec((tq, DVA), lambda b, h, i: (b * nq + i, h)),
        out_shape=jax.ShapeDtypeStruct(o_buf.shape, o_buf.dtype),
        input_output_aliases={5: 0},
        scratch_shapes=[pltpu.VMEM((t, 2 * DA), BF16), pltpu.VMEM((t, DVA), BF16)],
        compiler_params=_cparams(("parallel", "parallel", "arbitrary")),
        name="attn_prompt",
    )(qa, k_all, v_all, dl, g, o_buf)


def _attn_sample_kernel(q_ref, kn_ref, vn_ref, ck_ref, cv_ref, dl_ref, g_ref, o_buf, o_ref, *,
                        s_len, lam_init):
    del o_buf
    lam = _diff_lambda(dl_ref, lam_init)
    g = g_ref[...]
    for h in range(HA):
        sl = slice(h * 2 * DA, (h + 1) * 2 * DA)
        q1, q2 = _split_maps(q_ref[:, sl])
        qq = jnp.concatenate([q1, q2], axis=0)
        kc = ck_ref[:, sl].astype(BF16)
        kn = kn_ref[:, sl].astype(BF16)
        s_c = _dot_nt(qq, kc)
        s_n = _dot_nt(qq, kn)
        m = jnp.maximum(jnp.max(s_c, axis=-1, keepdims=True), jnp.max(s_n, axis=-1, keepdims=True))
        p_c = jnp.exp(s_c - m)
        p_n = jnp.exp(s_n - m)
        l = jnp.sum(p_c, axis=-1, keepdims=True) + jnp.sum(p_n, axis=-1, keepdims=True)
        o = (_dot(p_c.astype(BF16), cv_ref[:, sl].astype(BF16))
             + _dot(p_n.astype(BF16), vn_ref[:, sl].astype(BF16))) / l
        od = o[:s_len] - lam * o[s_len:]
        o_ref[:, sl] = _subln(od, g, 1.0 - lam_init).astype(BF16)


def _attn_sample(qa, k_all, v_all, ck, cv, dl, g, o_buf, layer, nbs, s_len, row0, lam_init):
    p_len = ck.shape[2]
    rb0 = row0 // s_len
    new = pl.BlockSpec((s_len, WA), lambda b: (layer * nbs + b, 0))
    cache = pl.BlockSpec((None, None, p_len, WA), lambda b: (layer, b, 0, 0))
    return pl.pallas_call(
        functools.partial(_attn_sample_kernel, s_len=s_len, lam_init=lam_init),
        grid=(nbs,),
        in_specs=[
            pl.BlockSpec((s_len, WA), lambda b: (rb0 + b, 0)),
            new, new, cache, cache,
            pl.BlockSpec((4, DA), lambda b: (0, 0)),
            pl.BlockSpec((1, DVA), lambda b: (0, 0)),
            pl.BlockSpec(memory_space=pl.ANY),
        ],
        out_specs=pl.BlockSpec((s_len, WA), lambda b: (rb0 + b, 0)),
        out_shape=jax.ShapeDtypeStruct(o_buf.shape, o_buf.dtype),
        input_output_aliases={7: 0},
        compiler_params=_cparams(("parallel",)),
        name="attn_sample",
    )(qa, k_all, v_all, ck, cv, dl, g, o_buf)


def _ret_kernel(q_ref, k_ref, v_ref, gate_ref, r0_ref, o_buf, o_ref, rout_ref, st_ref, *, blk):
    del o_buf
    c = pl.program_id(1)

    @pl.when(c == 0)
    def _():
        st_ref[...] = jnp.zeros_like(st_ref)
        for h in range(HB):
            e = h % 2
            st_ref[h, e * DKB:(e + 1) * DKB, :] = r0_ref[h]

    n_i = lax.broadcasted_iota(jnp.int32, (blk, blk), 0)
    m_i = lax.broadcasted_iota(jnp.int32, (blk, blk), 1)
    diff = (n_i - m_i).astype(F32)
    causal = n_i >= m_i
    n_col = lax.broadcasted_iota(jnp.int32, (blk, LANE), 0).astype(F32)
    lane = lax.broadcasted_iota(jnp.int32, (blk, LANE), 1)
    for h in range(HB):
        log_g = math.log1p(-2.0 ** (-5.0 - h))
        p, e = h // 2, h % 2
        sl = slice(p * LANE, (p + 1) * LANE)
        mine = (lane >= DKB) if e else (lane < DKB)
        q = jnp.where(mine, q_ref[:, sl], 0.0)
        k = jnp.where(mine, k_ref[:, sl], 0.0)
        vsl = slice(h * DVB, (h + 1) * DVB)
        v = v_ref[:, vsl].astype(BF16)
        qb = q.astype(BF16)
        dec = jnp.where(causal, jnp.exp(jnp.maximum(diff, 0.0) * log_g), 0.0)
        s = _dot_nt(qb, k.astype(BF16)) * dec
        intra = _dot(s.astype(BF16), v)
        r_old = st_ref[h]
        cross = _dot(qb, r_old.astype(BF16)) * jnp.exp((n_col + 1.0) * log_g)
        kd = (k * jnp.exp((blk - 1.0 - n_col) * log_g)).astype(BF16)
        r_new = math.exp(blk * log_g) * r_old + _dot_tn(kd, v)
        st_ref[h] = r_new
        o = intra + cross
        ms = jnp.mean(o * o, axis=-1, keepdims=True)
        gt = gate_ref[:, vsl]
        o_ref[:, vsl] = (o * lax.rsqrt(ms + RET_EPS) * (gt * _sigmoid(gt))).astype(BF16)

    @pl.when(c == pl.num_programs(1) - 1)
    def _():
        for h in range(HB):
            e = h % 2
            rout_ref[h] = st_ref[h, e * DKB:(e + 1) * DKB, :]


def _retention(qkb, pab, r0, o_buf, nseg, seg_len, row0, blk):
    nc = seg_len // blk
    rb0 = row0 // blk
    vcol = (A_COLS + 2 * HB * DKB) // WB
    row = lambda b, c: rb0 + b * nc + c
    return pl.pallas_call(
        functools.partial(_ret_kernel, blk=blk),
        grid=(nseg, nc),
        in_specs=[
            pl.BlockSpec((blk, HB * DKB), lambda b, c: (row(b, c), 0)),
            pl.BlockSpec((blk, HB * DKB), lambda b, c: (row(b, c), 1)),
            pl.BlockSpec((blk, WB), lambda b, c: (row(b, c), vcol)),
            pl.BlockSpec((blk, WB), lambda b, c: (row(b, c), vcol + 1)),
            pl.BlockSpec((None, HB, DKB, DVB), lambda b, c: (b, 0, 0, 0)),
            pl.BlockSpec(memory_space=pl.ANY),
        ],
        out_specs=[
            pl.BlockSpec((blk, WB), lambda b, c: (row(b, c), 0)),
            pl.BlockSpec((None, HB, DKB, DVB), lambda b, c: (b, 0, 0, 0)),
        ],
        out_shape=[jax.ShapeDtypeStruct(o_buf.shape, o_buf.dtype),
                   jax.ShapeDtypeStruct((nseg, HB, DKB, DVB), F32)],
        input_output_aliases={5: 0},
        scratch_shapes=[pltpu.VMEM((HB, 2 * DKB, DVB), F32)],
        compiler_params=_cparams(("parallel", "arbitrary")),
        name="retention",
    )(qkb, qkb, pab, pab, r0, o_buf)


def _head_sum(x, ones_bd):
    hi, lo = _split2(x)
    return _dot(hi, ones_bd) + _dot(lo, ones_bd)


def _wkv_pre_kernel(pc_ref, prev_ref, mu_ref, w0_ref, a0_ref, kk_ref, ka_ref, rk_ref,
                    w2_ref, a2_ref, g2_ref, ones_ref,
                    r_ref, lw_ref, k_ref, v_ref, an_ref, bn_ref, g_ref, bonus_ref, last_ref, carry_ref):
    c = pl.program_id(1)

    @pl.when(c == 0)
    def _():
        carry_ref[...] = prev_ref[...]

    pc = pc_ref[...]
    tb = pc.shape[0]
    row = lax.broadcasted_iota(jnp.int32, pc.shape, 0)
    shifted = jnp.where(row == 0, carry_ref[...], pltpu.roll(pc, 1, 0))
    carry_ref[...] = pc[tb - 1:tb, :]
    last_ref[...] = pc[tb - 1:tb, :]
    pcm = pc + mu_ref[...] * (shifted - pc)
    rc = pcm[:, 0:WC]
    kc = pcm[:, WC:2 * WC]
    vc = pcm[:, 2 * WC:3 * WC]
    lora = pcm[:, 3 * WC:C_PAD]
    z = w0_ref[...] + _dot(jnp.tanh(lora).astype(BF16), w2_ref[...])
    u = -z
    softplus = jnp.maximum(u, 0.0) + jnp.log1p(jnp.exp(-jnp.abs(u)))
    w_log = -softplus - 0.5
    a = _sigmoid(a0_ref[...] + _dot(lora.astype(BF16), a2_ref[...]))
    g = _dot(_sigmoid(lora).astype(BF16), g2_ref[...])
    kk = kc * kk_ref[...]
    kc2 = kc * (1.0 + (a - 1.0) * ka_ref[...])
    rkr = rc * kc2 * rk_ref[...]
    ones_bd = ones_ref[...]
    for t in range(WC // LANE):
        sl = slice(t * LANE, (t + 1) * LANE)
        kkt = kk[:, sl]
        nrm = jnp.maximum(jnp.sqrt(_head_sum(kkt * kkt, ones_bd)), 1e-12)
        kn = kkt / nrm
        an_ref[:, sl] = -kn
        bn_ref[:, sl] = kn * a[:, sl]
        bonus_ref[:, sl] = _head_sum(rkr[:, sl], ones_bd) * vc[:, sl]
    r_ref[...] = rc
    lw_ref[...] = -jnp.exp(w_log)
    k_ref[...] = kc2
    v_ref[...] = vc
    g_ref[...] = g


def _wkv_pre(pc, prev, vecs, mats, ones_bd, nseg, seg_len, row0, tb):
    nc = seg_len // tb
    rb0 = row0 // tb
    vec = lambda w: pl.BlockSpec((1, w), lambda b, c: (0, 0))
    mat = pl.BlockSpec((LORA_PAD, WC), lambda b, c: (0, 0))
    out = pl.BlockSpec((tb, WC), lambda b, c: (b * nc + c, 0))
    n = nseg * seg_len
    outs = pl.pallas_call(
        _wkv_pre_kernel,
        grid=(nseg, nc),
        in_specs=[
            pl.BlockSpec((tb, C_PAD), lambda b, c: (rb0 + b * nc + c, 0)),
            pl.BlockSpec((None, 1, C_PAD), lambda b, c: (b, 0, 0)),
            vec(C_PAD), vec(WC), vec(WC), vec(WC), vec(WC), vec(WC),
            mat, mat, mat,
            pl.BlockSpec((LANE, LANE), lambda b, c: (0, 0)),
        ],
        out_specs=[out] * 8 + [pl.BlockSpec((None, 1, C_PAD), lambda b, c: (b, 0, 0))],
        out_shape=[jax.ShapeDtypeStruct((n, WC), F32)] * 8
                  + [jax.ShapeDtypeStruct((nseg, 1, C_PAD), F32)],
        scratch_shapes=[pltpu.VMEM((1, C_PAD), F32)],
        compiler_params=_cparams(("parallel", "arbitrary")),
        name="wkv_pre",
    )(pc, prev, *vecs, *mats, ones_bd)
    return outs


def _wkv_kernel(r_ref, lw_ref, k_ref, v_ref, a_ref, b_ref, g_ref, bonus_ref, s0_ref,
                lng_ref, lnb_ref, tri_ref, ones_ref, o_buf,
                o_ref, sout_ref, st_ref, *, ck):
    del o_buf
    c = pl.program_id(1)

    @pl.when(c == 0)
    def _():
        st_ref[...] = s0_ref[...]

    lw = lw_ref[...]
    tri = tri_ref[...]
    l1, l2, l3 = _split3(lw)
    cum = _dot(tri, l1) + (_dot(tri, l2) + _dot(tri, l3))
    total = cum[ck - 1:ck, :]
    e_pos = jnp.exp(cum)
    e_neg = jnp.exp(-cum)
    e_rem = jnp.exp(total - cum)
    a_t = a_ref[...] * jnp.exp(cum - lw)
    r_t = r_ref[...] * e_pos
    b_t = b_ref[...] * e_neg
    k_t = k_ref[...] * e_neg
    b_p = b_ref[...] * e_rem
    k_p = k_ref[...] * e_rem
    p_tot = jnp.exp(total)

    ones_bd = ones_ref[...]
    lane = lax.broadcasted_iota(jnp.int32, (ck, LANE), 1)
    m0 = lane < NC

    def stack(x):
        return jnp.concatenate([jnp.where(m0, x, 0.0), jnp.where(m0, 0.0, x)], axis=0)

    n2 = 2 * ck
    ri = lax.broadcasted_iota(jnp.int32, (n2, n2), 0)
    ci = lax.broadcasted_iota(jnp.int32, (n2, n2), 1)
    same = (ri // ck) == (ci // ck)
    strict = jnp.logical_and(same, ri > ci)
    incl = jnp.logical_and(same, ri >= ci)

    pairs = range(HC // 2)
    sls = [slice(p * LANE, (p + 1) * LANE) for p in pairs]
    bf = lambda t: t.astype(BF16)
    a_s = [bf(stack(a_t[:, sl])) for sl in sls]
    r_s = [bf(stack(r_t[:, sl])) for sl in sls]
    b_s = [bf(stack(b_t[:, sl])) for sl in sls]
    k_s = [bf(stack(k_t[:, sl])) for sl in sls]
    v_s = [_split2(stack(v_ref[:, sl])) for sl in sls]
    s_old = [st_ref[p] for p in pairs]
    s_b = [bf(s) for s in s_old]
    lk = [bf(jnp.where(strict, _dot_nt(a_s[p], b_s[p]), 0.0)) for p in pairs]
    l_ak = [bf(jnp.where(strict, _dot_nt(a_s[p], k_s[p]), 0.0)) for p in pairs]
    l_rb = [bf(jnp.where(incl, _dot_nt(r_s[p], b_s[p]), 0.0)) for p in pairs]
    l_rk = [bf(jnp.where(incl, _dot_nt(r_s[p], k_s[p]), 0.0)) for p in pairs]
    x = [_dot_nt(a_s[p], s_b[p]) + _dot(l_ak[p], v_s[p][0]) for p in pairs]
    y_part = [_dot_nt(r_s[p], s_b[p]) + _dot(l_rk[p], v_s[p][0]) for p in pairs]
    span = 1
    while span < ck:
        x = [x[p] + _dot(lk[p], bf(x[p])) for p in pairs]
        span *= 2
        if span < ck:
            lk = [bf(_dot(lk[p], lk[p])) for p in pairs]
    x_sp = [_split2(x[p]) for p in pairs]
    y_st = [y_part[p] + _dot(l_rb[p], x_sp[p][0]) for p in pairs]
    for p in pairs:
        sl = sls[p]
        bp_s, kp_s = _split2(stack(b_p[:, sl])), _split2(stack(k_p[:, sl]))
        st_ref[p] = (s_old[p] * p_tot[:, sl] + _mm3(x_sp[p], bp_s, _dot_tn)
                     + _mm3(v_s[p], kp_s, _dot_tn))
    for p in pairs:
        sl = sls[p]
        y = y_st[p][0:ck] + y_st[p][ck:n2]
        mean = _head_sum(y, ones_bd) * (1.0 / NC)
        d = y - mean
        var = _head_sum(d * d, ones_bd) * (1.0 / NC)
        yn = d * lax.rsqrt(var + GN_EPS_C) * lng_ref[:, sl] + lnb_ref[:, sl]
        o_ref[:, sl] = ((yn + bonus_ref[:, sl]) * g_ref[:, sl]).astype(BF16)

    @pl.when(c == pl.num_programs(1) - 1)
    def _():
        sout_ref[...] = st_ref[...]


def _wkv(pre, s0, lng, lnb, tri, ones_bd, o_buf, nseg, seg_len, row0, ck):
    nc = seg_len // ck
    rb0 = row0 // ck
    blk = pl.BlockSpec((ck, WC), lambda b, c: (b * nc + c, 0))
    st = pl.BlockSpec((None, HC // 2, LANE, LANE), lambda b, c: (b, 0, 0, 0))
    vec = pl.BlockSpec((1, WC), lambda b, c: (0, 0))
    return pl.pallas_call(
        functools.partial(_wkv_kernel, ck=ck),
        grid=(nseg, nc),
        in_specs=[blk] * 8 + [st, vec, vec,
                              pl.BlockSpec((ck, ck), lambda b, c: (0, 0)),
                              pl.BlockSpec((LANE, LANE), lambda b, c: (0, 0)),
                              pl.BlockSpec(memory_space=pl.ANY)],
        out_specs=[pl.BlockSpec((ck, WC), lambda b, c: (rb0 + b * nc + c, 0)), st],
        out_shape=[jax.ShapeDtypeStruct(o_buf.shape, o_buf.dtype),
                   jax.ShapeDtypeStruct((nseg, HC // 2, LANE, LANE), F32)],
        input_output_aliases={13: 0},
        scratch_shapes=[pltpu.VMEM((HC // 2, LANE, LANE), F32)],
        compiler_params=_cparams(("parallel", "arbitrary")),
        name="wkv",
    )(*pre, s0, lng, lnb, tri, ones_bd, o_buf)


def _state_to_blockdiag(s):
    b = s.shape[0]
    sp = s.reshape(b, HC // 2, 2, NC, NC)
    z = jnp.zeros_like(sp[:, :, 0])
    top = jnp.concatenate([sp[:, :, 0], z], axis=-1)
    bot = jnp.concatenate([z, sp[:, :, 1]], axis=-1)
    return jnp.concatenate([top, bot], axis=-2)


def _blockdiag_to_state(sb):
    b = sb.shape[0]
    h0 = sb[:, :, :NC, :NC]
    h1 = sb[:, :, NC:, NC:]
    return jnp.stack([h0, h1], axis=2).reshape(b, HC, NC, NC)


def _merge_kernel(oa_ref, ob_ref, oc_ref, ga_ref, gb_ref, gc_ref, wa_ref, wb_ref, wc_ref, o_ref):
    o_ref[...] = (ga_ref[...] * _dot(oa_ref[...], wa_ref[...])
                  + gb_ref[...] * _dot(ob_ref[...], wb_ref[...])
                  + gc_ref[...] * _dot(oc_ref[...], wc_ref[...]))


def _merge(oa, ob, oc, gates, wbr):
    n = oa.shape[0]
    tm = _pick(n, (1024, 512, 256, 128, 64, 32, 16, 8))
    tn = 512
    nj = D_MODEL // tn
    bw = oa.shape[1]
    o_spec = pl.BlockSpec((tm, bw), lambda i, j: (i, 0))
    g_spec = lambda k: pl.BlockSpec((tm, tn), lambda i, j: (i, k * nj + j))
    w_spec = lambda k: pl.BlockSpec((None, bw, tn), lambda i, j: (k, 0, j))
    return pl.pallas_call(
        _merge_kernel,
        grid=(n // tm, nj),
        in_specs=[o_spec, o_spec, o_spec, g_spec(0), g_spec(1), g_spec(2),
                  w_spec(0), w_spec(1), w_spec(2)],
        out_specs=pl.BlockSpec((tm, tn), lambda i, j: (i, j)),
        out_shape=jax.ShapeDtypeStruct((n, D_MODEL), F32),
        compiler_params=_cparams(("parallel", "arbitrary")),
        name="merge",
    )(oa, ob, oc, gates, gates, gates, wbr, wbr, wbr)


def _outproj_kernel(m_ref, x_ref, w_ref, g_ref, b_ref, o_ref, ob_ref, *, alpha):
    y = alpha * x_ref[...] + _dot(m_ref[...].astype(BF16), w_ref[...])
    out = _ln_rows(y, g_ref[...], b_ref[...], LN_EPS)
    o_ref[...] = out
    ob_ref[...] = out.astype(BF16)


def _outproj(merged, x, w, g, b, alpha):
    n = x.shape[0]
    tm = _pick(n, (256, 128, 64, 32, 16, 8))
    row = pl.BlockSpec((tm, D_MODEL), lambda i: (i, 0))
    vec = pl.BlockSpec((1, D_MODEL), lambda i: (0, 0))
    return pl.pallas_call(
        functools.partial(_outproj_kernel, alpha=alpha),
        grid=(n // tm,),
        in_specs=[row, row, pl.BlockSpec((D_MODEL, D_MODEL), lambda i: (0, 0)), vec, vec],
        out_specs=[row, row],
        out_shape=[jax.ShapeDtypeStruct((n, D_MODEL), F32), jax.ShapeDtypeStruct((n, D_MODEL), BF16)],
        compiler_params=_cparams(("parallel",)),
        name="outproj",
    )(merged, x, w, g, b)


def kernel(x_prompt, x_sample, cache_k, cache_v, state_ret, state_wkv, state_shift, w_in, w_gate,
           w_branch, w_out, ffn_in, ffn_out, ln_g, ln_b, diff_lambda, subln_g, mu_c, w0, w2, a0, a2,
           g2, k_k, k_a, r_k, lnx_g, lnx_b):
    nb, t, _ = x_prompt.shape
    nbs, s_len, _ = x_sample.shape
    depth = w_in.shape[0]
    p_len = cache_k.shape[2]
    n_p = nb * t
    n_s = nbs * s_len
    n = n_p + n_s
    alpha = (2 * depth) ** 0.25

    x = jnp.concatenate([x_prompt.reshape(n_p, D_MODEL), x_sample.reshape(n_s, D_MODEL)], axis=0)
    xb = x.astype(BF16)

    pos = jnp.concatenate([jnp.tile(jnp.arange(t, dtype=jnp.int32), nb),
                           jnp.tile(p_len + jnp.arange(s_len, dtype=jnp.int32), nbs)])
    tabs_a = _rope_tables(pos, ROT_A, ROPE_THETA, DA)
    tabs_b = _rope_tables(pos, DKB, RET_THETA, DKB)

    ck_all = cache_k.reshape(depth, nbs, p_len, WA)
    cv_all = cache_v.reshape(depth, nbs, p_len, WA)

    eye = jnp.arange(LANE) // NC
    ones_bd = (eye[:, None] == eye[None, :]).astype(BF16)
    ck_p = _pick(t, (CHUNK,))
    ck_s = s_len
    tri_p = (jnp.arange(ck_p)[:, None] >= jnp.arange(ck_p)[None, :]).astype(BF16)
    tri_s = (jnp.arange(ck_s)[:, None] >= jnp.arange(ck_s)[None, :]).astype(BF16)
    ret_blk = _pick(t, (256, 128, 64))
    pre_tb = _pick(t, (256, 128, 64))

    outs = {k: [] for k in ("rp", "wp", "sp", "rs", "ws", "ss")}
    pad_f = D_FF_PAD - D_FF

    qa = jnp.zeros((n, WA), BF16)
    qkb = jnp.zeros((n, 2 * HB * DKB), F32)
    k_p, v_p = jnp.zeros((depth * n_p, WA), F32), jnp.zeros((depth * n_p, WA), F32)
    k_s, v_s = jnp.zeros((depth * n_s, WA), F32), jnp.zeros((depth * n_s, WA), F32)
    oa, ob, oc = (jnp.zeros((n, w), BF16) for w in (WA, WB, WC))

    for l in range(depth):
        lam_init = 0.8 - 0.6 * math.exp(-0.3 * l)

        def ffn_weights(i):
            wa = jnp.pad(ffn_in[l, i, :, :D_FF].astype(BF16), ((0, 0), (0, pad_f)))
            wb = jnp.pad(ffn_in[l, i, :, D_FF:].astype(BF16), ((0, 0), (0, pad_f)))
            wd = jnp.pad(ffn_out[l, i].astype(BF16), ((0, pad_f), (0, 0)))
            return wa, wb, wd

        x, xb = _ffn(x, xb, *ffn_weights(0), ln_g[l, 0][None], ln_b[l, 0][None], alpha)

        w_ab = w_in[l, :, :A_COLS + B_COLS].astype(BF16)
        w_c = jnp.pad(w_in[l, :, A_COLS + B_COLS:].astype(BF16), ((0, 0), (0, C_PAD - C_COLS)))
        pab = _proj(xb, w_ab, 1024, name="proj_ab")
        pc = _proj(xb, w_c, C_PAD // 3, name="proj_c")
        gates = _proj(xb, w_gate[l].astype(BF16), 1024, act="sigmoid", name="proj_gate")
        qa, k_p, v_p, qkb = _rope(pab, tabs_a, tabs_b, qa, k_p, v_p, qkb, 0, n_p, l * n_p)
        qa, k_s, v_s, qkb = _rope(pab, tabs_a, tabs_b, qa, k_s, v_s, qkb, n_p, n_s, l * n_s)

        dl = diff_lambda[l]
        sg = subln_g[l][None]
        oa = _attn_prompt(qa, k_p, v_p, dl, sg, oa, l, nb, t, lam_init)
        oa = _attn_sample(qa, k_s, v_s, ck_all, cv_all, dl, sg, oa, l, nbs, s_len, n_p, lam_init)

        ob, ret_p = _retention(qkb, pab, jnp.zeros((nb, HB, DKB, DVB), F32), ob, nb, t, 0, ret_blk)
        ob, ret_s = _retention(qkb, pab, state_ret[l], ob, nbs, s_len, n_p, s_len)

        padc = C_PAD - C_COLS
        vecs = (jnp.pad(mu_c[l], (0, padc))[None], w0[l][None], a0[l][None], k_k[l][None],
                k_a[l][None], r_k[l].reshape(1, WC))
        zrow = lambda r0, r1, m: jnp.pad(m.astype(BF16), ((r0, LORA_PAD - r1), (0, 0)))
        mats = (zrow(0, LORA_W, w2[l]), zrow(LORA_W, LORA_W + LORA_A, a2[l]),
                zrow(LORA_W + LORA_A, LORA_W + LORA_A + LORA_G, g2[l]))
        prev_p = jnp.zeros((nb, 1, C_PAD), F32)
        prev_s = jnp.pad(state_shift[l], ((0, 0), (0, 0), (0, padc)))
        pre_p = _wkv_pre(pc, prev_p, vecs, mats, ones_bd, nb, t, 0, pre_tb)
        pre_s = _wkv_pre(pc, prev_s, vecs, mats, ones_bd, nbs, s_len, n_p, s_len)
        lng, lnb = lnx_g[l][None], lnx_b[l][None]
        oc, wkv_p = _wkv(pre_p[:8], jnp.zeros((nb, HC // 2, LANE, LANE), F32), lng, lnb, tri_p,
                         ones_bd, oc, nb, t, 0, ck_p)
        oc, wkv_s = _wkv(pre_s[:8], _state_to_blockdiag(state_wkv[l]), lng, lnb, tri_s,
                         ones_bd, oc, nbs, s_len, n_p, ck_s)

        merged = _merge(oa, ob, oc, gates, w_branch[l].astype(BF16))
        x, xb = _outproj(merged, x, w_out[l].astype(BF16), ln_g[l, 1][None], ln_b[l, 1][None], alpha)

        x, xb = _ffn(x, xb, *ffn_weights(1), ln_g[l, 2][None], ln_b[l, 2][None], alpha)

        outs["rp"].append(ret_p)
        outs["rs"].append(ret_s)
        outs["wp"].append(_blockdiag_to_state(wkv_p))
        outs["ws"].append(_blockdiag_to_state(wkv_s))
        outs["sp"].append(pre_p[8][:, :, :C_COLS])
        outs["ss"].append(pre_s[8][:, :, :C_COLS])

    st = {k: jnp.stack(v) for k, v in outs.items()}
    y_p = x[:n_p].reshape(nb, t, D_MODEL)
    y_s = x[n_p:].reshape(nbs, s_len, D_MODEL)
    kv_p = lambda a: a.reshape(depth, nb, t, HA, DVA)
    kv_s = lambda a: a.reshape(depth, nbs, s_len, HA, DVA)
    return (y_p, y_s, kv_p(k_p), kv_p(v_p), st["rp"], st["wp"], st["sp"],
            kv_s(k_s), kv_s(v_s), st["rs"], st["ws"], st["ss"])
```

```python
import functools
import math

import jax
import jax.numpy as jnp
from jax import lax
from jax.experimental import pallas as pl
from jax.experimental.pallas import tpu as pltpu

F32 = jnp.float32
BF16 = jnp.bfloat16

D_MODEL = 2048
CHUNK = 64
HA, DA, DVA = 8, 64, 128
ROT_A = DA // 4
ROPE_THETA = 500000.0
HB, DKB, DVB = 8, 64, 128
RET_THETA = 10000.0
HC, NC = 16, 64
LORA_W, LORA_A, LORA_G = 64, 64, 160
WA, WB, WC = HA * DVA, HB * DVB, HC * NC
A_COLS = 2 * HA * 2 * DA + HA * DVA
B_COLS = 2 * HB * DKB + HB * DVB + WB
C_COLS = 3 * WC + LORA_W + LORA_A + LORA_G
D_FF = 5504
LN_EPS = 1e-5
SUBLN_EPS = 1e-5
RET_EPS = 1e-6
GN_EPS_C = 64e-5

LANE = 128
D_FF_PAD = 5632
FFN_TF = 512
LORA_PAD = 384
C_PAD = 3 * WC + LORA_PAD
VMEM_LIMIT = 56 * 1024 * 1024


def _cparams(sem):
    return pltpu.CompilerParams(dimension_semantics=sem, vmem_limit_bytes=VMEM_LIMIT)


def _pick(n, cands):
    for c in cands:
        if n % c == 0:
            return c
    raise ValueError(f"no tile for {n}")


def _sigmoid(x):
    return 1.0 / (1.0 + jnp.exp(-x))


def _ln_rows(y, g, b, eps):
    mu = jnp.mean(y, axis=-1, keepdims=True)
    d = y - mu
    var = jnp.mean(d * d, axis=-1, keepdims=True)
    return d * lax.rsqrt(var + eps) * g + b


def _dot(a, b):
    return jnp.dot(a, b, preferred_element_type=F32)


def _dot_nt(a, b):
    return lax.dot_general(a, b, (((1,), (1,)), ((), ())), preferred_element_type=F32)


def _dot_tn(a, b):
    return lax.dot_general(a, b, (((0,), (0,)), ((), ())), preferred_element_type=F32)


def _split2(x):
    hi = x.astype(BF16)
    lo = (x - hi.astype(F32)).astype(BF16)
    return hi, lo


def _split3(x):
    hi = x.astype(BF16)
    r1 = x - hi.astype(F32)
    mid = r1.astype(BF16)
    lo = (r1 - mid.astype(F32)).astype(BF16)
    return hi, mid, lo


def _mm3(a, b, dot=_dot):
    ah, al = a
    bh, bl = b
    return dot(ah, bh) + (dot(ah, bl) + dot(al, bh))


def _ffn_kernel(x_ref, xb_ref, wa_ref, wb_ref, wd_ref, g_ref, b_ref, o_ref, ob_ref, acc_ref, *, alpha):
    j = pl.program_id(1)

    @pl.when(j == 0)
    def _():
        acc_ref[...] = jnp.zeros_like(acc_ref)

    xb = xb_ref[...]
    ha = _dot(xb, wa_ref[...])
    hb = _dot(xb, wb_ref[...])
    act = (ha * _sigmoid(ha) * hb).astype(BF16)
    acc_ref[...] += _dot(act, wd_ref[...])

    @pl.when(j == pl.num_programs(1) - 1)
    def _():
        y = alpha * x_ref[...] + 0.5 * acc_ref[...]
        out = _ln_rows(y, g_ref[...], b_ref[...], LN_EPS)
        o_ref[...] = out
        ob_ref[...] = out.astype(BF16)


def _ffn(x, xb, wa, wb, wd, g, b, alpha):
    n = x.shape[0]
    tm = _pick(n, (512, 256, 128, 64, 32, 16, 8))
    tf = FFN_TF
    grid = (n // tm, D_FF_PAD // tf)
    row = pl.BlockSpec((tm, D_MODEL), lambda i, j: (i, 0))
    vec = pl.BlockSpec((1, D_MODEL), lambda i, j: (0, 0))
    return pl.pallas_call(
        functools.partial(_ffn_kernel, alpha=alpha),
        grid=grid,
        in_specs=[
            row, row,
            pl.BlockSpec((D_MODEL, tf), lambda i, j: (0, j)),
            pl.BlockSpec((D_MODEL, tf), lambda i, j: (0, j)),
            pl.BlockSpec((tf, D_MODEL), lambda i, j: (j, 0)),
            vec, vec,
        ],
        out_specs=[row, row],
        out_shape=[jax.ShapeDtypeStruct((n, D_MODEL), F32), jax.ShapeDtypeStruct((n, D_MODEL), BF16)],
        scratch_shapes=[pltpu.VMEM((tm, D_MODEL), F32)],
        compiler_params=_cparams(("parallel", "arbitrary")),
        name="ffn",
    )(x, xb, wa, wb, wd, g, b)


def _proj_kernel(x_ref, w_ref, o_ref, *, act):
    acc = _dot(x_ref[...], w_ref[...])
    if act == "sigmoid":
        acc = _sigmoid(acc)
    o_ref[...] = acc


def _proj(xb, w, tn, act=None, name="proj"):
    n, m = xb.shape[0], w.shape[1]
    tm = _pick(n, (1024, 512, 256, 128, 64, 32, 16, 8))
    return pl.pallas_call(
        functools.partial(_proj_kernel, act=act),
        grid=(n // tm, m // tn),
        in_specs=[
            pl.BlockSpec((tm, D_MODEL), lambda i, j: (i, 0)),
            pl.BlockSpec((D_MODEL, tn), lambda i, j: (0, j)),
        ],
        out_specs=pl.BlockSpec((tm, tn), lambda i, j: (i, j)),
        out_shape=jax.ShapeDtypeStruct((n, m), F32),
        compiler_params=_cparams(("parallel", "arbitrary")),
        name=name,
    )(xb, w)


def _rope_tables(pos, rot, theta, width):
    half = rot // 2
    freq = jnp.power(theta, -jnp.arange(half, dtype=F32) / half)
    ang = pos.astype(F32)[:, None] * freq[None, :]
    cos, sin = jnp.cos(ang), jnp.sin(ang)
    n = pos.shape[0]
    pad = width - rot
    c = jnp.concatenate([cos, cos, jnp.ones((n, pad), F32)], axis=-1)
    s_lo = jnp.concatenate([-sin, jnp.zeros((n, half + pad), F32)], axis=-1)
    s_hi = jnp.concatenate([jnp.zeros((n, half), F32), sin, jnp.zeros((n, pad), F32)], axis=-1)
    rep = LANE // width
    return tuple(jnp.tile(t, (1, rep)) for t in (c, s_lo, s_hi))


def _rope_kernel(xa_ref, xv_ref, xb_ref, ca_ref, la_ref, ha_ref, cb_ref, lb_ref, hb_ref,
                 qa_buf, kall_buf, vall_buf, qkb_buf, qa_ref, kall_ref, vall_ref, qkb_ref, kd_ref, vd_ref):
    del qa_buf, kall_buf, vall_buf, qkb_buf

    def rot(x, c, lo, hi, half):
        return x * c + pltpu.roll(x, LANE - half, 1) * lo + pltpu.roll(x, half, 1) * hi

    ca, la, ha = ca_ref[...], la_ref[...], ha_ref[...]
    nqa = HA * 2 * DA // LANE
    tm = xa_ref.shape[0]
    ks = []
    for t in range(2 * nqa):
        y = rot(xa_ref[:, t * LANE:(t + 1) * LANE], ca, la, ha, ROT_A // 2)
        if t < nqa:
            qa_ref[:, t * LANE:(t + 1) * LANE] = (y * (DA ** -0.5)).astype(BF16)
        else:
            ks.append(y)
            kd_ref[:, (t - nqa) * LANE:(t - nqa + 1) * LANE] = y.astype(BF16)
    kall_ref[...] = jnp.concatenate(ks, axis=1).reshape(tm, HA, 2 * DA)
    v = xv_ref[...]
    vall_ref[...] = v.reshape(tm, HA, DVA)
    vd_ref[...] = v.astype(BF16)
    cb, lb, hb = cb_ref[...], lb_ref[...], hb_ref[...]
    nqb = HB * DKB // LANE
    for t in range(2 * nqb):
        sl = slice(t * LANE, (t + 1) * LANE)
        y = rot(xb_ref[:, sl], cb, lb, hb, DKB // 2)
        if t >= nqb:
            y = y * (DKB ** -0.5)
        qkb_ref[:, sl] = y


def _rope(pab, tabs_a, tabs_b, qa_buf, kall_buf, vall_buf, qkb_buf, row0, nrows, kv_row0):
    tm = _pick(nrows, (256, 128, 64, 32, 16, 8))
    rb0, kb0 = row0 // tm, kv_row0 // tm
    wqk = 2 * HA * 2 * DA
    wb = 2 * HB * DKB
    tab = pl.BlockSpec((tm, LANE), lambda i: (rb0 + i, 0))
    anyspec = pl.BlockSpec(memory_space=pl.ANY)
    bufs = (qa_buf, kall_buf, vall_buf, qkb_buf)
    cache = pl.BlockSpec((tm, HA, DVA), lambda i: (kb0 + i, 0, 0))
    dense = pl.BlockSpec((tm, WA), lambda i: (i, 0))
    return pl.pallas_call(
        _rope_kernel,
        grid=(nrows // tm,),
        in_specs=[
            pl.BlockSpec((tm, wqk), lambda i: (rb0 + i, 0)),
            pl.BlockSpec((tm, WA), lambda i: (rb0 + i, wqk // WA)),
            pl.BlockSpec((tm, wb), lambda i: (rb0 + i, A_COLS // wb)),
            tab, tab, tab, tab, tab, tab,
            anyspec, anyspec, anyspec, anyspec,
        ],
        out_specs=[
            pl.BlockSpec((tm, WA), lambda i: (rb0 + i, 0)),
            cache, cache,
            pl.BlockSpec((tm, wb), lambda i: (rb0 + i, 0)),
            dense, dense,
        ],
        out_shape=[jax.ShapeDtypeStruct(b.shape, b.dtype) for b in bufs]
                  + [jax.ShapeDtypeStruct((nrows, WA), BF16)] * 2,
        input_output_aliases={9: 0, 10: 1, 11: 2, 12: 3},
        compiler_params=_cparams(("parallel",)),
        name="rope",
    )(pab, pab, pab, *tabs_a, *tabs_b, *bufs)


def _diff_lambda(dl_ref, lam_init):
    lv = dl_ref[...]
    s1 = jnp.sum(lv[0:1] * lv[1:2], axis=-1, keepdims=True)
    s2 = jnp.sum(lv[2:3] * lv[3:4], axis=-1, keepdims=True)
    return jnp.exp(s1) - jnp.exp(s2) + lam_init


def _split_maps(q):
    lane = lax.broadcasted_iota(jnp.int32, q.shape, 1)
    zero = jnp.zeros_like(q)
    return jnp.where(lane < DA, q, zero), jnp.where(lane >= DA, q, zero)


def _subln(o, g, scale):
    ms = jnp.mean(o * o, axis=-1, keepdims=True)
    return o * lax.rsqrt(ms + SUBLN_EPS) * g * scale


def _attn_prompt_kernel(q_ref, kb_ref, v_ref, dl_ref, g_ref, o_buf, o_ref, vb_ref, s_ref,
                        m_ref, acc_ref, *, tq, tk, lam_init):
    del o_buf
    qi = pl.program_id(2)

    @pl.when(qi == 0)
    def _():
        vb_ref[:, 0:DVA] = v_ref[...]
        vb_ref[:, DVA:2 * DVA] = jnp.ones(v_ref.shape, BF16)

    lam = _diff_lambda(dl_ref, lam_init)
    qs = _split_maps(q_ref[...])

    def scores(kb, slot):
        k = kb_ref[pl.ds(pl.multiple_of(kb * tk, tk), tk), :]
        for i, qm in enumerate(qs):
            s_ref[slot, i] = _dot_nt(qm, k)

    def consume(ss, start, width):
        v1 = vb_ref[pl.ds(start, width), :]
        stats = []
        for i, s in enumerate(ss):
            m = m_ref[i]
            m_new = jnp.maximum(m, jnp.max(s, axis=-1, keepdims=True))
            m_ref[i] = m_new
            stats.append((jnp.exp(m - m_new), jnp.exp(s - m_new).astype(BF16)))
        pvs = [_dot(p, v1) for _, p in stats]
        for i, ((a, _), pv) in enumerate(zip(stats, pvs)):
            acc_ref[i] = a * acc_ref[i] + pv

    def full_step(kb, slot):
        scores(kb + 1, 1 - slot)
        consume((s_ref[slot, 0], s_ref[slot, 1]), pl.multiple_of(kb * tk, tk), tk)

    q0 = qi * tq
    n_full = q0 // tk
    m_ref[...] = jnp.full(m_ref.shape, -1e30, F32)
    acc_ref[...] = jnp.zeros_like(acc_ref)
    scores(0, 0)

    def two_steps(j, _):
        full_step(2 * j, 0)
        full_step(2 * j + 1, 1)
        return 0

    lax.fori_loop(0, n_full // 2, two_steps, 0)

    @pl.when(n_full % 2 == 1)
    def _():
        full_step(n_full - 1, 0)

    tail = pl.multiple_of(n_full * tk, tk)
    g = g_ref[...]

    for r in range(tk // tq):
        @pl.when(q0 - tail == r * tq)
        def _():
            width = (r + 1) * tq
            row = lax.broadcasted_iota(jnp.int32, (tq, width), 0) // CHUNK
            col = lax.broadcasted_iota(jnp.int32, (tq, width), 1) // CHUNK
            ok = col - row <= r * (tq // CHUNK)
            consume(tuple(jnp.where(ok, s_ref[n_full % 2, i, :, 0:width], -1e30) for i in range(2)),
                    tail, width)
            a1, a2 = acc_ref[0], acc_ref[1]
            o = a1[:, 0:DVA] / a1[:, DVA:2 * DVA] - lam * (a2[:, 0:DVA] / a2[:, DVA:2 * DVA])
            o_ref[...] = _subln(o, g, 1.0 - lam_init).astype(BF16)


def _attn_prompt(qa, kd, vd, dl, g, o_buf, nb, t, lam_init):
    tq = _pick(t, (256, 128, 64))
    tk = _pick(t, (1024, 512, 256, 128, 64))
    nq = t // tq
    kv = pl.BlockSpec((t, 2 * DA), lambda b, h, i: (b, h))
    return pl.pallas_call(
        functools.partial(_attn_prompt_kernel, tq=tq, tk=tk, lam_init=lam_init),
        grid=(nb, HA, nq),
        in_specs=[
            pl.BlockSpec((tq, 2 * DA), lambda b, h, i: (b * nq + i, h)),
            kv, kv,
            pl.BlockSpec((4, DA), lambda b, h, i: (0, 0)),
            pl.BlockSpec((1, DVA), lambda b, h, i: (0, 0)),
            pl.BlockSpec(memory_space=pl.ANY),
        ],
        out_specs=pl.BlockSpec((tq, DVA), lambda b, h, i: (b * nq + i, h)),
        out_shape=jax.ShapeDtypeStruct(o_buf.shape, o_buf.dtype),
        input_output_aliases={5: 0},
        scratch_shapes=[pltpu.VMEM((t, 2 * DVA), BF16),
                        pltpu.VMEM((2, 2, tq, tk), F32),
                        pltpu.VMEM((2, tq, 1), F32), pltpu.VMEM((2, tq, 2 * DVA), F32)],
        compiler_params=_cparams(("parallel", "parallel", "arbitrary")),
        name="attn_prompt",
    )(qa, kd, vd, dl, g, o_buf)


def _attn_sample_kernel(q_ref, kn_ref, vn_ref, ck_ref, cv_ref, dl_ref, g_ref, o_buf, o_ref, *,
                        s_len, lam_init):
    del o_buf
    lam = _diff_lambda(dl_ref, lam_init)
    g = g_ref[...]
    for h in range(HA):
        sl = slice(h * 2 * DA, (h + 1) * 2 * DA)
        q1, q2 = _split_maps(q_ref[:, sl])
        qq = jnp.concatenate([q1, q2], axis=0)
        kc = ck_ref[:, sl].astype(BF16)
        kn = kn_ref[:, sl]
        s_c = _dot_nt(qq, kc)
        s_n = _dot_nt(qq, kn)
        m = jnp.maximum(jnp.max(s_c, axis=-1, keepdims=True), jnp.max(s_n, axis=-1, keepdims=True))
        p_c = jnp.exp(s_c - m)
        p_n = jnp.exp(s_n - m)
        l = jnp.sum(p_c, axis=-1, keepdims=True) + jnp.sum(p_n, axis=-1, keepdims=True)
        o = (_dot(p_c.astype(BF16), cv_ref[:, sl].astype(BF16))
             + _dot(p_n.astype(BF16), vn_ref[:, sl])) / l
        od = o[:s_len] - lam * o[s_len:]
        o_ref[:, sl] = _subln(od, g, 1.0 - lam_init).astype(BF16)


def _attn_sample(qa, kd, vd, ck, cv, dl, g, o_buf, layer, nbs, s_len, row0, lam_init):
    p_len = ck.shape[2]
    rb0 = row0 // s_len
    new = pl.BlockSpec((s_len, WA), lambda b: (b, 0))
    cache = pl.BlockSpec((None, None, p_len, WA), lambda b: (layer, b, 0, 0))
    return pl.pallas_call(
        functools.partial(_attn_sample_kernel, s_len=s_len, lam_init=lam_init),
        grid=(nbs,),
        in_specs=[
            pl.BlockSpec((s_len, WA), lambda b: (rb0 + b, 0)),
            new, new, cache, cache,
            pl.BlockSpec((4, DA), lambda b: (0, 0)),
            pl.BlockSpec((1, DVA), lambda b: (0, 0)),
            pl.BlockSpec(memory_space=pl.ANY),
        ],
        out_specs=pl.BlockSpec((s_len, WA), lambda b: (rb0 + b, 0)),
        out_shape=jax.ShapeDtypeStruct(o_buf.shape, o_buf.dtype),
        input_output_aliases={7: 0},
        compiler_params=_cparams(("parallel",)),
        name="attn_sample",
    )(qa, kd, vd, ck, cv, dl, g, o_buf)


def _ret_kernel(q_ref, k_ref, v_ref, gate_ref, r0_ref, o_buf, o_ref, rout_ref, st_ref, *, blk):
    del o_buf
    c = pl.program_id(1)

    @pl.when(c == 0)
    def _():
        st_ref[...] = jnp.zeros_like(st_ref)
        for h in range(HB):
            e = h % 2
            st_ref[h, e * DKB:(e + 1) * DKB, :] = r0_ref[h]

    n_i = lax.broadcasted_iota(jnp.int32, (blk, blk), 0)
    m_i = lax.broadcasted_iota(jnp.int32, (blk, blk), 1)
    diff = (n_i - m_i).astype(F32)
    causal = n_i >= m_i
    n_col = lax.broadcasted_iota(jnp.int32, (blk, LANE), 0).astype(F32)
    lane = lax.broadcasted_iota(jnp.int32, (blk, LANE), 1)
    for h in range(HB):
        log_g = math.log1p(-2.0 ** (-5.0 - h))
        p, e = h // 2, h % 2
        sl = slice(p * LANE, (p + 1) * LANE)
        mine = (lane >= DKB) if e else (lane < DKB)
        q = jnp.where(mine, q_ref[:, sl], 0.0)
        k = jnp.where(mine, k_ref[:, sl], 0.0)
        vsl = slice(h * DVB, (h + 1) * DVB)
        v = v_ref[:, vsl].astype(BF16)
        qb = q.astype(BF16)
        dec = jnp.where(causal, jnp.exp(jnp.maximum(diff, 0.0) * log_g), 0.0)
        s = _dot_nt(qb, k.astype(BF16)) * dec
        intra = _dot(s.astype(BF16), v)
        r_old = st_ref[h]
        cross = _dot(qb, r_old.astype(BF16)) * jnp.exp((n_col + 1.0) * log_g)
        kd = (k * jnp.exp((blk - 1.0 - n_col) * log_g)).astype(BF16)
        r_new = math.exp(blk * log_g) * r_old + _dot_tn(kd, v)
        st_ref[h] = r_new
        o = intra + cross
        ms = jnp.mean(o * o, axis=-1, keepdims=True)
        gt = gate_ref[:, vsl]
        o_ref[:, vsl] = (o * lax.rsqrt(ms + RET_EPS) * (gt * _sigmoid(gt))).astype(BF16)

    @pl.when(c == pl.num_programs(1) - 1)
    def _():
        for h in range(HB):
            e = h % 2
            rout_ref[h] = st_ref[h, e * DKB:(e + 1) * DKB, :]


def _retention(qkb, pab, r0, o_buf, nseg, seg_len, row0, blk):
    nc = seg_len // blk
    rb0 = row0 // blk
    vcol = (A_COLS + 2 * HB * DKB) // WB
    row = lambda b, c: rb0 + b * nc + c
    return pl.pallas_call(
        functools.partial(_ret_kernel, blk=blk),
        grid=(nseg, nc),
        in_specs=[
            pl.BlockSpec((blk, HB * DKB), lambda b, c: (row(b, c), 0)),
            pl.BlockSpec((blk, HB * DKB), lambda b, c: (row(b, c), 1)),
            pl.BlockSpec((blk, WB), lambda b, c: (row(b, c), vcol)),
            pl.BlockSpec((blk, WB), lambda b, c: (row(b, c), vcol + 1)),
            pl.BlockSpec((None, HB, DKB, DVB), lambda b, c: (b, 0, 0, 0)),
            pl.BlockSpec(memory_space=pl.ANY),
        ],
        out_specs=[
            pl.BlockSpec((blk, WB), lambda b, c: (row(b, c), 0)),
            pl.BlockSpec((None, HB, DKB, DVB), lambda b, c: (b, 0, 0, 0)),
        ],
        out_shape=[jax.ShapeDtypeStruct(o_buf.shape, o_buf.dtype),
                   jax.ShapeDtypeStruct((nseg, HB, DKB, DVB), F32)],
        input_output_aliases={5: 0},
        scratch_shapes=[pltpu.VMEM((HB, 2 * DKB, DVB), F32)],
        compiler_params=_cparams(("parallel", "arbitrary")),
        name="retention",
    )(qkb, qkb, pab, pab, r0, o_buf)


def _head_sum(x, ones_bd):
    hi, lo = _split2(x)
    return _dot(hi, ones_bd) + _dot(lo, ones_bd)


def _wkv_kernel(*refs, ck, nsb):
    pc_refs = refs[:nsb]
    (prev_ref, s0_ref, mu_ref, w0_ref, a0_ref, kk_ref, ka_ref, rk_ref, w2_ref, a2_ref, g2_ref,
     lng_ref, lnb_ref, tri_ref, ones_ref, o_ref, sout_ref, last_ref, st_ref, carry_ref) = refs[nsb:]
    c = pl.program_id(1)

    @pl.when(c == 0)
    def _():
        st_ref[...] = s0_ref[...]
        carry_ref[...] = prev_ref[...]

    row = lax.broadcasted_iota(jnp.int32, (ck, C_PAD), 0)
    pcs, shifted = [], []
    for g, pc_ref in enumerate(pc_refs):
        pc = pc_ref[...]
        pcs.append(pc)
        shifted.append(jnp.where(row == 0, carry_ref[g], pltpu.roll(pc, 1, 0)))
        carry_ref[g] = pc[ck - 1:ck, :]
        last_ref[g] = pc[ck - 1:ck, :]
    pc_all = jnp.concatenate(pcs, axis=0)
    pcm = pc_all + mu_ref[...] * (jnp.concatenate(shifted, axis=0) - pc_all)
    rc = pcm[:, 0:WC]
    kc = pcm[:, WC:2 * WC]
    vc = pcm[:, 2 * WC:3 * WC]
    lora = pcm[:, 3 * WC:C_PAD]
    z = w0_ref[...] + _dot(jnp.tanh(lora).astype(BF16), w2_ref[...])
    u = -z
    softplus = jnp.maximum(u, 0.0) + jnp.log1p(jnp.exp(-jnp.abs(u)))
    lw = -jnp.exp(-softplus - 0.5)
    a_sig = _sigmoid(a0_ref[...] + _dot(lora.astype(BF16), a2_ref[...]))
    gate = _dot(_sigmoid(lora).astype(BF16), g2_ref[...])
    kk = kc * kk_ref[...]
    kc2 = kc * (1.0 + (a_sig - 1.0) * ka_ref[...])
    rkr = rc * kc2 * rk_ref[...]
    ones_bd = ones_ref[...]

    tri = tri_ref[...]
    l1, l2, l3 = _split3(lw)
    cum = _dot(tri, l1) + (_dot(tri, l2) + _dot(tri, l3))
    total = jnp.concatenate(
        [jnp.broadcast_to(cum[(g + 1) * ck - 1:(g + 1) * ck, :], (ck, WC)) for g in range(nsb)], axis=0)
    e_neg = jnp.exp(-cum)
    e_rem = jnp.exp(total - cum)
    e_exc = jnp.exp(cum - lw)
    r_t = rc * jnp.exp(cum)
    k_t = kc2 * e_neg
    k_p = kc2 * e_rem
    p_tot = jnp.exp(total)

    lane = lax.broadcasted_iota(jnp.int32, (ck, LANE), 1)
    m0 = lane < NC

    def stack(x):
        return jnp.concatenate([jnp.where(m0, x, 0.0), jnp.where(m0, 0.0, x)], axis=0)

    n2 = 2 * ck
    ri = lax.broadcasted_iota(jnp.int32, (n2, n2), 0)
    ci = lax.broadcasted_iota(jnp.int32, (n2, n2), 1)
    same = (ri // ck) == (ci // ck)
    strict = jnp.logical_and(same, ri > ci)
    incl = jnp.logical_and(same, ri >= ci)

    bf = lambda t: t.astype(BF16)
    pairs = range(HC // 2)
    chains = [(g, p) for g in range(nsb) for p in pairs]
    a_s, r_s, b_s, k_s, v_s, bp_s, kp_s, bonus = {}, {}, {}, {}, {}, {}, {}, {}
    for p in pairs:
        sl = slice(p * LANE, (p + 1) * LANE)
        kkt = kk[:, sl]
        kn = kkt / jnp.maximum(jnp.sqrt(_head_sum(kkt * kkt, ones_bd)), 1e-12)
        bn = kn * a_sig[:, sl]
        a_tp = -kn * e_exc[:, sl]
        b_tp = bn * e_neg[:, sl]
        b_pp = bn * e_rem[:, sl]
        bonus_p = _head_sum(rkr[:, sl], ones_bd) * vc[:, sl]
        for g in range(nsb):
            rows = slice(g * ck, (g + 1) * ck)
            a_s[g, p] = bf(stack(a_tp[rows]))
            r_s[g, p] = bf(stack(r_t[rows, sl]))
            b_s[g, p] = bf(stack(b_tp[rows]))
            k_s[g, p] = bf(stack(k_t[rows, sl]))
            v_s[g, p] = _split2(stack(vc[rows, sl]))
            bp_s[g, p] = _split2(stack(b_pp[rows]))
            kp_s[g, p] = _split2(stack(k_p[rows, sl]))
            bonus[g, p] = bonus_p[rows]

    s_old = {gp: st_ref[gp[0], gp[1]] for gp in chains}
    s_b = {gp: bf(s_old[gp]) for gp in chains}
    if n2 % LANE == 0:
        big = {gp: _dot_nt(jnp.concatenate([a_s[gp], r_s[gp]], axis=0),
                           jnp.concatenate([b_s[gp], k_s[gp]], axis=0)) for gp in chains}
        ab = {gp: big[gp][0:n2, 0:n2] for gp in chains}
        ak = {gp: big[gp][0:n2, n2:2 * n2] for gp in chains}
        rb = {gp: big[gp][n2:2 * n2, 0:n2] for gp in chains}
        rk = {gp: big[gp][n2:2 * n2, n2:2 * n2] for gp in chains}
    else:
        ab = {gp: _dot_nt(a_s[gp], b_s[gp]) for gp in chains}
        ak = {gp: _dot_nt(a_s[gp], k_s[gp]) for gp in chains}
        rb = {gp: _dot_nt(r_s[gp], b_s[gp]) for gp in chains}
        rk = {gp: _dot_nt(r_s[gp], k_s[gp]) for gp in chains}
    lk = {gp: bf(jnp.where(strict, ab[gp], 0.0)) for gp in chains}
    l_ak = {gp: bf(jnp.where(strict, ak[gp], 0.0)) for gp in chains}
    l_rb = {gp: bf(jnp.where(incl, rb[gp], 0.0)) for gp in chains}
    l_rk = {gp: bf(jnp.where(incl, rk[gp], 0.0)) for gp in chains}
    x = {gp: _dot_nt(a_s[gp], s_b[gp]) + _dot(l_ak[gp], v_s[gp][0]) for gp in chains}
    y_part = {gp: _dot_nt(r_s[gp], s_b[gp]) + _dot(l_rk[gp], v_s[gp][0]) for gp in chains}
    span = 1
    while span < ck:
        x = {gp: x[gp] + _dot(lk[gp], bf(x[gp])) for gp in chains}
        span *= 2
        if span < ck:
            lk = {gp: bf(_dot(lk[gp], lk[gp])) for gp in chains}
    x_sp = {gp: _split2(x[gp]) for gp in chains}
    y_st = {gp: y_part[gp] + _dot(l_rb[gp], x_sp[gp][0]) for gp in chains}
    for g, p in chains:
        sl = slice(p * LANE, (p + 1) * LANE)
        decay = p_tot[(g + 1) * ck - 1:(g + 1) * ck, sl]
        uv = tuple(jnp.concatenate([xs, vs], axis=0) for xs, vs in zip(x_sp[g, p], v_s[g, p]))
        bk = tuple(jnp.concatenate([bs, ks], axis=0) for bs, ks in zip(bp_s[g, p], kp_s[g, p]))
        st_ref[g, p] = s_old[g, p] * decay + _mm3(uv, bk, _dot_tn)
    for g, p in chains:
        sl = slice(p * LANE, (p + 1) * LANE)
        rows = slice(g * ck, (g + 1) * ck)
        y = y_st[g, p][0:ck] + y_st[g, p][ck:n2]
        mean = _head_sum(y, ones_bd) * (1.0 / NC)
        d = y - mean
        var = _head_sum(d * d, ones_bd) * (1.0 / NC)
        yn = d * lax.rsqrt(var + GN_EPS_C) * lng_ref[:, sl] + lnb_ref[:, sl]
        o_ref[g, :, sl] = ((yn + bonus[g, p]) * gate[rows, sl]).astype(BF16)

    @pl.when(c == pl.num_programs(1) - 1)
    def _():
        sout_ref[...] = st_ref[...]


def _wkv(pc, prev, s0, vecs, mats, lng, lnb, ones_bd, nseg, seg_len, row0, ck, nsb):
    nc = seg_len // ck
    rb0 = row0 // ck
    pos = jnp.arange(nsb * ck)
    tri = ((pos[:, None] // ck == pos[None, :] // ck) & (pos[:, None] >= pos[None, :])).astype(BF16)
    pc_spec = lambda g: pl.BlockSpec((ck, C_PAD), lambda b, c: (rb0 + (b * nsb + g) * nc + c, 0))
    vec = lambda w: pl.BlockSpec((1, w), lambda b, c: (0, 0))
    mat = pl.BlockSpec((LORA_PAD, WC), lambda b, c: (0, 0))
    shift = pl.BlockSpec((nsb, 1, C_PAD), lambda b, c: (b, 0, 0))
    st = pl.BlockSpec((nsb, HC // 2, LANE, LANE), lambda b, c: (b, 0, 0, 0))
    return pl.pallas_call(
        functools.partial(_wkv_kernel, ck=ck, nsb=nsb),
        grid=(nseg // nsb, nc),
        in_specs=[pc_spec(g) for g in range(nsb)] + [
            shift, st,
            vec(C_PAD), vec(WC), vec(WC), vec(WC), vec(WC), vec(WC),
            mat, mat, mat, vec(WC), vec(WC),
            pl.BlockSpec((nsb * ck, nsb * ck), lambda b, c: (0, 0)),
            pl.BlockSpec((LANE, LANE), lambda b, c: (0, 0)),
        ],
        out_specs=[pl.BlockSpec((nsb, ck, WC), lambda b, c: (b, c, 0)), st, shift],
        out_shape=[jax.ShapeDtypeStruct((nseg, seg_len, WC), BF16),
                   jax.ShapeDtypeStruct((nseg, HC // 2, LANE, LANE), F32),
                   jax.ShapeDtypeStruct((nseg, 1, C_PAD), F32)],
        scratch_shapes=[pltpu.VMEM((nsb, HC // 2, LANE, LANE), F32), pltpu.VMEM((nsb, 1, C_PAD), F32)],
        compiler_params=_cparams(("parallel", "arbitrary")),
        name="wkv",
    )(*([pc] * nsb), prev, s0, *vecs, *mats, lng, lnb, tri, ones_bd)


def _state_to_blockdiag(s):
    b = s.shape[0]
    sp = s.reshape(b, HC // 2, 2, NC, NC)
    z = jnp.zeros_like(sp[:, :, 0])
    top = jnp.concatenate([sp[:, :, 0], z], axis=-1)
    bot = jnp.concatenate([z, sp[:, :, 1]], axis=-1)
    return jnp.concatenate([top, bot], axis=-2)


def _blockdiag_to_state(sb):
    b = sb.shape[0]
    h0 = sb[:, :, :NC, :NC]
    h1 = sb[:, :, NC:, NC:]
    return jnp.stack([h0, h1], axis=2).reshape(b, HC, NC, NC)


def _merge_kernel(oa_ref, ob_ref, oc_ref, ga_ref, gb_ref, gc_ref, wa_ref, wb_ref, wc_ref, o_ref):
    o_ref[...] = (ga_ref[...] * _dot(oa_ref[...], wa_ref[...])
                  + gb_ref[...] * _dot(ob_ref[...], wb_ref[...])
                  + gc_ref[...] * _dot(oc_ref[...], wc_ref[...]))


def _merge(oa, ob, oc, gates, wbr):
    n = oa.shape[0]
    tm = _pick(n, (1024, 512, 256, 128, 64, 32, 16, 8))
    tn = 512
    nj = D_MODEL // tn
    bw = oa.shape[1]
    o_spec = pl.BlockSpec((tm, bw), lambda i, j: (i, 0))
    g_spec = lambda k: pl.BlockSpec((tm, tn), lambda i, j: (i, k * nj + j))
    w_spec = lambda k: pl.BlockSpec((None, bw, tn), lambda i, j: (k, 0, j))
    return pl.pallas_call(
        _merge_kernel,
        grid=(n // tm, nj),
        in_specs=[o_spec, o_spec, o_spec, g_spec(0), g_spec(1), g_spec(2),
                  w_spec(0), w_spec(1), w_spec(2)],
        out_specs=pl.BlockSpec((tm, tn), lambda i, j: (i, j)),
        out_shape=jax.ShapeDtypeStruct((n, D_MODEL), F32),
        compiler_params=_cparams(("parallel", "arbitrary")),
        name="merge",
    )(oa, ob, oc, gates, gates, gates, wbr, wbr, wbr)


def _outproj_kernel(m_ref, x_ref, w_ref, g_ref, b_ref, o_ref, ob_ref, *, alpha):
    y = alpha * x_ref[...] + _dot(m_ref[...].astype(BF16), w_ref[...])
    out = _ln_rows(y, g_ref[...], b_ref[...], LN_EPS)
    o_ref[...] = out
    ob_ref[...] = out.astype(BF16)


def _outproj(merged, x, w, g, b, alpha):
    n = x.shape[0]
    tm = _pick(n, (256, 128, 64, 32, 16, 8))
    row = pl.BlockSpec((tm, D_MODEL), lambda i: (i, 0))
    vec = pl.BlockSpec((1, D_MODEL), lambda i: (0, 0))
    return pl.pallas_call(
        functools.partial(_outproj_kernel, alpha=alpha),
        grid=(n // tm,),
        in_specs=[row, row, pl.BlockSpec((D_MODEL, D_MODEL), lambda i: (0, 0)), vec, vec],
        out_specs=[row, row],
        out_shape=[jax.ShapeDtypeStruct((n, D_MODEL), F32), jax.ShapeDtypeStruct((n, D_MODEL), BF16)],
        compiler_params=_cparams(("parallel",)),
        name="outproj",
    )(merged, x, w, g, b)


def kernel(x_prompt, x_sample, cache_k, cache_v, state_ret, state_wkv, state_shift, w_in, w_gate,
           w_branch, w_out, ffn_in, ffn_out, ln_g, ln_b, diff_lambda, subln_g, mu_c, w0, w2, a0, a2,
           g2, k_k, k_a, r_k, lnx_g, lnx_b):
    nb, t, _ = x_prompt.shape
    nbs, s_len, _ = x_sample.shape
    depth = w_in.shape[0]
    p_len = cache_k.shape[2]
    n_p = nb * t
    n_s = nbs * s_len
    n = n_p + n_s
    alpha = (2 * depth) ** 0.25

    x = jnp.concatenate([x_prompt.reshape(n_p, D_MODEL), x_sample.reshape(n_s, D_MODEL)], axis=0)
    xb = x.astype(BF16)

    pos = jnp.concatenate([jnp.tile(jnp.arange(t, dtype=jnp.int32), nb),
                           jnp.tile(p_len + jnp.arange(s_len, dtype=jnp.int32), nbs)])
    tabs_a = _rope_tables(pos, ROT_A, ROPE_THETA, DA)
    tabs_b = _rope_tables(pos, DKB, RET_THETA, DKB)

    ck_all = cache_k.reshape(depth, nbs, p_len, WA)
    cv_all = cache_v.reshape(depth, nbs, p_len, WA)

    eye = jnp.arange(LANE) // NC
    ones_bd = (eye[:, None] == eye[None, :]).astype(BF16)
    ck_p = _pick(t, (CHUNK,))
    ck_s = s_len
    nsb_p = _pick(nb, (2, 1))
    nsb_s = _pick(nbs, (4, 2, 1))
    ret_blk = _pick(t, (256, 128, 64))

    outs = {k: [] for k in ("rp", "wp", "sp", "rs", "ws", "ss")}
    pad_f = D_FF_PAD - D_FF

    qa = jnp.zeros((n, WA), BF16)
    qkb = jnp.zeros((n, 2 * HB * DKB), F32)
    k_p, v_p = (jnp.zeros((depth * n_p, HA, DVA), F32) for _ in range(2))
    k_s, v_s = (jnp.zeros((depth * n_s, HA, DVA), F32) for _ in range(2))
    oa, ob = jnp.zeros((n, WA), BF16), jnp.zeros((n, WB), BF16)

    for l in range(depth):
        lam_init = 0.8 - 0.6 * math.exp(-0.3 * l)

        def ffn_weights(i):
            wa = jnp.pad(ffn_in[l, i, :, :D_FF].astype(BF16), ((0, 0), (0, pad_f)))
            wb = jnp.pad(ffn_in[l, i, :, D_FF:].astype(BF16), ((0, 0), (0, pad_f)))
            wd = jnp.pad(ffn_out[l, i].astype(BF16), ((0, pad_f), (0, 0)))
            return wa, wb, wd

        x, xb = _ffn(x, xb, *ffn_weights(0), ln_g[l, 0][None], ln_b[l, 0][None], alpha)

        w_ab = w_in[l, :, :A_COLS + B_COLS].astype(BF16)
        w_c = jnp.pad(w_in[l, :, A_COLS + B_COLS:].astype(BF16), ((0, 0), (0, C_PAD - C_COLS)))
        pab = _proj(xb, w_ab, 1024, name="proj_ab")
        pc = _proj(xb, w_c, C_PAD // 3, name="proj_c")
        gates = _proj(xb, w_gate[l].astype(BF16), 1024, act="sigmoid", name="proj_gate")
        qa, k_p, v_p, qkb, kd_p, vd_p = _rope(pab, tabs_a, tabs_b, qa, k_p, v_p, qkb, 0, n_p, l * n_p)
        qa, k_s, v_s, qkb, kd_s, vd_s = _rope(pab, tabs_a, tabs_b, qa, k_s, v_s, qkb, n_p, n_s, l * n_s)

        dl = diff_lambda[l]
        sg = subln_g[l][None]
        oa = _attn_prompt(qa, kd_p, vd_p, dl, sg, oa, nb, t, lam_init)
        oa = _attn_sample(qa, kd_s, vd_s, ck_all, cv_all, dl, sg, oa, l, nbs, s_len, n_p, lam_init)

        ob, ret_p = _retention(qkb, pab, jnp.zeros((nb, HB, DKB, DVB), F32), ob, nb, t, 0, ret_blk)
        ob, ret_s = _retention(qkb, pab, state_ret[l], ob, nbs, s_len, n_p, s_len)

        padc = C_PAD - C_COLS
        vecs = (jnp.pad(mu_c[l], (0, padc))[None], w0[l][None], a0[l][None], k_k[l][None],
                k_a[l][None], r_k[l].reshape(1, WC))
        zrow = lambda r0, r1, m: jnp.pad(m.astype(BF16), ((r0, LORA_PAD - r1), (0, 0)))
        mats = (zrow(0, LORA_W, w2[l]), zrow(LORA_W, LORA_W + LORA_A, a2[l]),
                zrow(LORA_W + LORA_A, LORA_W + LORA_A + LORA_G, g2[l]))
        prev_p = jnp.zeros((nb, 1, C_PAD), F32)
        prev_s = jnp.pad(state_shift[l], ((0, 0), (0, 0), (0, padc)))
        lng, lnb = lnx_g[l][None], lnx_b[l][None]
        oc_p, wkv_p, shift_p = _wkv(pc, prev_p, jnp.zeros((nb, HC // 2, LANE, LANE), F32), vecs, mats,
                                          lng, lnb, ones_bd, nb, t, 0, ck_p, nsb_p)
        oc_s, wkv_s, shift_s = _wkv(pc, prev_s, _state_to_blockdiag(state_wkv[l]), vecs, mats,
                                          lng, lnb, ones_bd, nbs, s_len, n_p, ck_s, nsb_s)
        oc = jnp.concatenate([oc_p.reshape(n_p, WC), oc_s.reshape(n_s, WC)], axis=0)

        merged = _merge(oa, ob, oc, gates, w_branch[l].astype(BF16))
        x, xb = _outproj(merged, x, w_out[l].astype(BF16), ln_g[l, 1][None], ln_b[l, 1][None], alpha)

        x, xb = _ffn(x, xb, *ffn_weights(1), ln_g[l, 2][None], ln_b[l, 2][None], alpha)

        outs["rp"].append(ret_p)
        outs["rs"].append(ret_s)
        outs["wp"].append(_blockdiag_to_state(wkv_p))
        outs["ws"].append(_blockdiag_to_state(wkv_s))
        outs["sp"].append(shift_p[:, :, :C_COLS])
        outs["ss"].append(shift_s[:, :, :C_COLS])

    st = {k: jnp.stack(v) for k, v in outs.items()}
    y_p = x[:n_p].reshape(nb, t, D_MODEL)
    y_s = x[n_p:].reshape(nbs, s_len, D_MODEL)
    kv_p = lambda a: a.reshape(depth, nb, t, HA, DVA)
    kv_s = lambda a: a.reshape(depth, nbs, s_len, HA, DVA)
    return (y_p, y_s, kv_p(k_p), kv_p(v_p), st["rp"], st["wp"], st["sp"],
            kv_s(k_s), kv_s(v_s), st["rs"], st["ws"], st["ss"])
```

```python
import functools
import math

import jax
import jax.numpy as jnp
from jax import lax
from jax.experimental import pallas as pl
from jax.experimental.pallas import tpu as pltpu

F32 = jnp.float32
BF16 = jnp.bfloat16

D_MODEL = 2048
CHUNK = 64
HA, DA, DVA = 8, 64, 128
ROT_A = DA // 4
ROPE_THETA = 500000.0
HB, DKB, DVB = 8, 64, 128
RET_THETA = 10000.0
HC, NC = 16, 64
LORA_W, LORA_A, LORA_G = 64, 64, 160
WA, WB, WC = HA * DVA, HB * DVB, HC * NC
N_BRANCH = 3
A_COLS = 2 * HA * 2 * DA + HA * DVA
B_COLS = 2 * HB * DKB + HB * DVB + WB
C_COLS = 3 * WC + LORA_W + LORA_A + LORA_G
D_FF = 5504
LN_EPS = 1e-5
SUBLN_EPS = 1e-5
RET_EPS = 1e-6
GN_EPS_C = 64e-5

LANE = 128
D_FF_PAD = 5632
FFN_TF = 512
LORA_PAD = 384
C_PAD = 3 * WC + LORA_PAD
VMEM_LIMIT = 56 * 1024 * 1024


def _cparams(sem):
    return pltpu.CompilerParams(dimension_semantics=sem, vmem_limit_bytes=VMEM_LIMIT)


def _pick(n, cands):
    for c in cands:
        if n % c == 0:
            return c
    raise ValueError(f"no tile for {n}")


def _sigmoid(x):
    return 1.0 / (1.0 + jnp.exp(-x))


def _ln_rows(y, g, b, eps):
    mu = jnp.mean(y, axis=-1, keepdims=True)
    d = y - mu
    var = jnp.mean(d * d, axis=-1, keepdims=True)
    return d * lax.rsqrt(var + eps) * g + b


def _dot(a, b):
    return jnp.dot(a, b, preferred_element_type=F32)


def _dot_nt(a, b):
    return lax.dot_general(a, b, (((1,), (1,)), ((), ())), preferred_element_type=F32)


def _dot_tn(a, b):
    return lax.dot_general(a, b, (((0,), (0,)), ((), ())), preferred_element_type=F32)


def _split2(x):
    hi = x.astype(BF16)
    lo = (x - hi.astype(F32)).astype(BF16)
    return hi, lo


def _split3(x):
    hi = x.astype(BF16)
    r1 = x - hi.astype(F32)
    mid = r1.astype(BF16)
    lo = (r1 - mid.astype(F32)).astype(BF16)
    return hi, mid, lo


def _mm3(a, b, dot=_dot):
    ah, al = a
    bh, bl = b
    return dot(ah, bh) + (dot(ah, bl) + dot(al, bh))


def _cast_kernel(w_ref, o_ref):
    o_ref[...] = w_ref[...].astype(BF16)


def _cast(src, lead, rows, width, col_block=0):
    tr = _pick(rows, (256, 128, 64, 32, 16, 8))
    return pl.pallas_call(
        _cast_kernel,
        grid=(rows // tr,),
        in_specs=[pl.BlockSpec((None,) * len(lead) + (tr, width), lambda i: lead + (i, col_block))],
        out_specs=pl.BlockSpec((tr, width), lambda i: (i, 0)),
        out_shape=jax.ShapeDtypeStruct((rows, width), BF16),
        compiler_params=_cparams(("parallel",)),
        name="cast",
    )(src)


def _cast_ffn_in_kernel(a_ref, b_ref, oa_ref, ob_ref):
    for src, dst in ((a_ref, oa_ref), (b_ref, ob_ref)):
        dst[:, 0:D_FF] = src[...].astype(BF16)
        dst[:, D_FF:D_FF_PAD] = jnp.zeros((dst.shape[0], D_FF_PAD - D_FF), BF16)


def _cast_ffn_in(ffn_in, l, i):
    tr = 128
    half = lambda k: pl.BlockSpec((None, None, tr, D_FF), lambda r: (l, i, r, k))
    out = pl.BlockSpec((tr, D_FF_PAD), lambda r: (r, 0))
    return pl.pallas_call(
        _cast_ffn_in_kernel,
        grid=(D_MODEL // tr,),
        in_specs=[half(0), half(1)],
        out_specs=[out, out],
        out_shape=[jax.ShapeDtypeStruct((D_MODEL, D_FF_PAD), BF16)] * 2,
        compiler_params=_cparams(("parallel",)),
        name="cast_ffn_in",
    )(ffn_in, ffn_in)


def _cast_ffn_out_kernel(w_ref, o_ref, *, n_valid):
    x = w_ref[...].astype(BF16)
    o_ref[...] = jnp.where(pl.program_id(0) < n_valid, x, jnp.zeros_like(x))


def _cast_ffn_out(ffn_out, l, i):
    tr = LANE
    n_valid = D_FF // tr
    return pl.pallas_call(
        functools.partial(_cast_ffn_out_kernel, n_valid=n_valid),
        grid=(D_FF_PAD // tr,),
        in_specs=[pl.BlockSpec((None, None, tr, D_MODEL),
                               lambda r: (l, i, jnp.minimum(r, n_valid - 1), 0))],
        out_specs=pl.BlockSpec((tr, D_MODEL), lambda r: (r, 0)),
        out_shape=jax.ShapeDtypeStruct((D_FF_PAD, D_MODEL), BF16),
        compiler_params=_cparams(("parallel",)),
        name="cast_ffn_out",
    )(ffn_out)


def _cast_wc_kernel(w_ref, o_ref):
    lane = lax.broadcasted_iota(jnp.int32, w_ref.shape, 1)
    valid = lane < C_COLS - pl.program_id(1) * LORA_PAD
    o_ref[...] = jnp.where(valid, w_ref[...], 0.0).astype(BF16)


def _cast_wc(w_in, l):
    tr = 512
    cb0 = (A_COLS + B_COLS) // LORA_PAD
    return pl.pallas_call(
        _cast_wc_kernel,
        grid=(D_MODEL // tr, C_PAD // LORA_PAD),
        in_specs=[pl.BlockSpec((None, tr, LORA_PAD), lambda r, j: (l, r, cb0 + j))],
        out_specs=pl.BlockSpec((tr, LORA_PAD), lambda r, j: (r, j)),
        out_shape=jax.ShapeDtypeStruct((D_MODEL, C_PAD), BF16),
        compiler_params=_cparams(("parallel", "parallel")),
        name="cast_wc",
    )(w_in)


def _ffn_kernel(x_ref, xb_ref, wa_ref, wb_ref, wd_ref, g_ref, b_ref, o_ref, ob_ref, acc_ref, *, alpha):
    j = pl.program_id(1)

    @pl.when(j == 0)
    def _():
        acc_ref[...] = jnp.zeros_like(acc_ref)

    xb = xb_ref[...]
    ha = _dot(xb, wa_ref[...])
    hb = _dot(xb, wb_ref[...])
    act = (ha * _sigmoid(ha) * hb).astype(BF16)
    acc_ref[...] += _dot(act, wd_ref[...])

    @pl.when(j == pl.num_programs(1) - 1)
    def _():
        y = alpha * x_ref[...] + 0.5 * acc_ref[...]
        out = _ln_rows(y, g_ref[...], b_ref[...], LN_EPS)
        o_ref[...] = out
        ob_ref[...] = out.astype(BF16)


def _ffn(x, xb, wa, wb, wd, g, b, alpha):
    n = x.shape[0]
    tm = _pick(n, (512, 256, 128, 64, 32, 16, 8))
    tf = FFN_TF
    grid = (n // tm, D_FF_PAD // tf)
    row = pl.BlockSpec((tm, D_MODEL), lambda i, j: (i, 0))
    vec = pl.BlockSpec((1, D_MODEL), lambda i, j: (0, 0))
    return pl.pallas_call(
        functools.partial(_ffn_kernel, alpha=alpha),
        grid=grid,
        in_specs=[
            row, row,
            pl.BlockSpec((D_MODEL, tf), lambda i, j: (0, j)),
            pl.BlockSpec((D_MODEL, tf), lambda i, j: (0, j)),
            pl.BlockSpec((tf, D_MODEL), lambda i, j: (j, 0)),
            vec, vec,
        ],
        out_specs=[row, row],
        out_shape=[jax.ShapeDtypeStruct((n, D_MODEL), F32), jax.ShapeDtypeStruct((n, D_MODEL), BF16)],
        scratch_shapes=[pltpu.VMEM((tm, D_MODEL), F32)],
        compiler_params=_cparams(("parallel", "arbitrary")),
        name="ffn",
    )(x, xb, wa, wb, wd, g, b)


def _proj_kernel(x_ref, w_ref, o_ref, *, act):
    acc = _dot(x_ref[...], w_ref[...])
    if act == "sigmoid":
        acc = _sigmoid(acc)
    o_ref[...] = acc


def _proj(xb, w, tn, act=None, name="proj"):
    n, m = xb.shape[0], w.shape[1]
    tm = _pick(n, (1024, 512, 256, 128, 64, 32, 16, 8))
    return pl.pallas_call(
        functools.partial(_proj_kernel, act=act),
        grid=(n // tm, m // tn),
        in_specs=[
            pl.BlockSpec((tm, D_MODEL), lambda i, j: (i, 0)),
            pl.BlockSpec((D_MODEL, tn), lambda i, j: (0, j)),
        ],
        out_specs=pl.BlockSpec((tm, tn), lambda i, j: (i, j)),
        out_shape=jax.ShapeDtypeStruct((n, m), F32),
        compiler_params=_cparams(("parallel", "arbitrary")),
        name=name,
    )(xb, w)


def _rope_tables(pos, rot, theta, width):
    half = rot // 2
    freq = jnp.power(theta, -jnp.arange(half, dtype=F32) / half)
    ang = pos.astype(F32)[:, None] * freq[None, :]
    cos, sin = jnp.cos(ang), jnp.sin(ang)
    n = pos.shape[0]
    pad = width - rot
    c = jnp.concatenate([cos, cos, jnp.ones((n, pad), F32)], axis=-1)
    s_lo = jnp.concatenate([-sin, jnp.zeros((n, half + pad), F32)], axis=-1)
    s_hi = jnp.concatenate([jnp.zeros((n, half), F32), sin, jnp.zeros((n, pad), F32)], axis=-1)
    rep = LANE // width
    return tuple(jnp.tile(t, (1, rep)) for t in (c, s_lo, s_hi))


def _rope_kernel(xa_ref, xv_ref, xb_ref, ca_ref, la_ref, ha_ref, cb_ref, lb_ref, hb_ref,
                 qa_buf, kall_buf, vall_buf, qkb_buf, qa_ref, kall_ref, vall_ref, qkb_ref, kd_ref, vd_ref):
    del qa_buf, kall_buf, vall_buf, qkb_buf

    def rot(x, c, lo, hi, half):
        return x * c + pltpu.roll(x, LANE - half, 1) * lo + pltpu.roll(x, half, 1) * hi

    ca, la, ha = ca_ref[...], la_ref[...], ha_ref[...]
    nqa = HA * 2 * DA // LANE
    tm = xa_ref.shape[0]
    ks = []
    for t in range(2 * nqa):
        y = rot(xa_ref[:, t * LANE:(t + 1) * LANE], ca, la, ha, ROT_A // 2)
        if t < nqa:
            qa_ref[:, t * LANE:(t + 1) * LANE] = (y * (DA ** -0.5)).astype(BF16)
        else:
            ks.append(y)
            kd_ref[:, (t - nqa) * LANE:(t - nqa + 1) * LANE] = y.astype(BF16)
    kall_ref[...] = jnp.concatenate(ks, axis=1).reshape(tm, HA, 2 * DA)
    v = xv_ref[...]
    vall_ref[...] = v.reshape(tm, HA, DVA)
    vd_ref[...] = v.astype(BF16)
    cb, lb, hb = cb_ref[...], lb_ref[...], hb_ref[...]
    nqb = HB * DKB // LANE
    for t in range(2 * nqb):
        sl = slice(t * LANE, (t + 1) * LANE)
        y = rot(xb_ref[:, sl], cb, lb, hb, DKB // 2)
        if t >= nqb:
            y = y * (DKB ** -0.5)
        qkb_ref[:, sl] = y


def _rope(pab, tabs_a, tabs_b, qa_buf, kall_buf, vall_buf, qkb_buf, row0, nrows, kv_row0):
    tm = _pick(nrows, (256, 128, 64, 32, 16, 8))
    rb0, kb0 = row0 // tm, kv_row0 // tm
    wqk = 2 * HA * 2 * DA
    wb = 2 * HB * DKB
    tab = pl.BlockSpec((tm, LANE), lambda i: (rb0 + i, 0))
    anyspec = pl.BlockSpec(memory_space=pl.ANY)
    bufs = (qa_buf, kall_buf, vall_buf, qkb_buf)
    cache = pl.BlockSpec((tm, HA, DVA), lambda i: (kb0 + i, 0, 0))
    dense = pl.BlockSpec((tm, WA), lambda i: (i, 0))
    return pl.pallas_call(
        _rope_kernel,
        grid=(nrows // tm,),
        in_specs=[
            pl.BlockSpec((tm, wqk), lambda i: (rb0 + i, 0)),
            pl.BlockSpec((tm, WA), lambda i: (rb0 + i, wqk // WA)),
            pl.BlockSpec((tm, wb), lambda i: (rb0 + i, A_COLS // wb)),
            tab, tab, tab, tab, tab, tab,
            anyspec, anyspec, anyspec, anyspec,
        ],
        out_specs=[
            pl.BlockSpec((tm, WA), lambda i: (rb0 + i, 0)),
            cache, cache,
            pl.BlockSpec((tm, wb), lambda i: (rb0 + i, 0)),
            dense, dense,
        ],
        out_shape=[jax.ShapeDtypeStruct(b.shape, b.dtype) for b in bufs]
                  + [jax.ShapeDtypeStruct((nrows, WA), BF16)] * 2,
        input_output_aliases={9: 0, 10: 1, 11: 2, 12: 3},
        compiler_params=_cparams(("parallel",)),
        name="rope",
    )(pab, pab, pab, *tabs_a, *tabs_b, *bufs)


def _diff_lambda(dl_ref, lam_init):
    lv = dl_ref[...]
    s1 = jnp.sum(lv[0:1] * lv[1:2], axis=-1, keepdims=True)
    s2 = jnp.sum(lv[2:3] * lv[3:4], axis=-1, keepdims=True)
    return jnp.exp(s1) - jnp.exp(s2) + lam_init


def _split_maps(q):
    lane = lax.broadcasted_iota(jnp.int32, q.shape, 1)
    zero = jnp.zeros_like(q)
    return jnp.where(lane < DA, q, zero), jnp.where(lane >= DA, q, zero)


def _subln(o, g, scale):
    ms = jnp.mean(o * o, axis=-1, keepdims=True)
    return o * lax.rsqrt(ms + SUBLN_EPS) * g * scale


def _attn_prompt_kernel(q_ref, kb_ref, v_ref, dl_ref, g_ref, o_buf, o_ref, vb_ref, s_ref,
                        m_ref, acc_ref, *, tq, tk, lam_init):
    del o_buf
    qi = pl.program_id(2)

    @pl.when(qi == 0)
    def _():
        vb_ref[:, 0:DVA] = v_ref[...]
        vb_ref[:, DVA:2 * DVA] = jnp.ones(v_ref.shape, BF16)

    lam = _diff_lambda(dl_ref, lam_init)
    qs = _split_maps(q_ref[...])

    def scores(kb, slot):
        k = kb_ref[pl.ds(pl.multiple_of(kb * tk, tk), tk), :]
        for i, qm in enumerate(qs):
            s_ref[slot, i] = _dot_nt(qm, k)

    def consume(ss, start, width):
        v1 = vb_ref[pl.ds(start, width), :]
        stats = []
        for i, s in enumerate(ss):
            m = m_ref[i]
            m_new = jnp.maximum(m, jnp.max(s, axis=-1, keepdims=True))
            m_ref[i] = m_new
            stats.append((jnp.exp(m - m_new), jnp.exp(s - m_new).astype(BF16)))
        pvs = [_dot(p, v1) for _, p in stats]
        for i, ((a, _), pv) in enumerate(zip(stats, pvs)):
            acc_ref[i] = a * acc_ref[i] + pv

    def full_step(kb, slot):
        scores(kb + 1, 1 - slot)
        consume((s_ref[slot, 0], s_ref[slot, 1]), pl.multiple_of(kb * tk, tk), tk)

    q0 = qi * tq
    n_full = q0 // tk
    m_ref[...] = jnp.full(m_ref.shape, -1e30, F32)
    acc_ref[...] = jnp.zeros_like(acc_ref)
    scores(0, 0)

    def two_steps(j, _):
        full_step(2 * j, 0)
        full_step(2 * j + 1, 1)
        return 0

    lax.fori_loop(0, n_full // 2, two_steps, 0)

    @pl.when(n_full % 2 == 1)
    def _():
        full_step(n_full - 1, 0)

    tail = pl.multiple_of(n_full * tk, tk)
    g = g_ref[...]

    for r in range(tk // tq):
        @pl.when(q0 - tail == r * tq)
        def _():
            width = (r + 1) * tq
            row = lax.broadcasted_iota(jnp.int32, (tq, width), 0) // CHUNK
            col = lax.broadcasted_iota(jnp.int32, (tq, width), 1) // CHUNK
            ok = col - row <= r * (tq // CHUNK)
            consume(tuple(jnp.where(ok, s_ref[n_full % 2, i, :, 0:width], -1e30) for i in range(2)),
                    tail, width)
            a1, a2 = acc_ref[0], acc_ref[1]
            o = a1[:, 0:DVA] / a1[:, DVA:2 * DVA] - lam * (a2[:, 0:DVA] / a2[:, DVA:2 * DVA])
            o_ref[...] = _subln(o, g, 1.0 - lam_init).astype(BF16)


def _attn_prompt(qa, kd, vd, dl, g, o_buf, nb, t, lam_init):
    tq = _pick(t, (512, 256, 128, 64))
    tk = _pick(t, (1024, 512, 256, 128, 64))
    nq = t // tq
    kv = pl.BlockSpec((t, 2 * DA), lambda b, h, i: (b, h))
    return pl.pallas_call(
        functools.partial(_attn_prompt_kernel, tq=tq, tk=tk, lam_init=lam_init),
        grid=(nb, HA, nq),
        in_specs=[
            pl.BlockSpec((tq, 2 * DA), lambda b, h, i: (b * nq + i, h)),
            kv, kv,
            pl.BlockSpec((4, DA), lambda b, h, i: (0, 0)),
            pl.BlockSpec((1, DVA), lambda b, h, i: (0, 0)),
            pl.BlockSpec(memory_space=pl.ANY),
        ],
        out_specs=pl.BlockSpec((tq, DVA), lambda b, h, i: (b * nq + i, h)),
        out_shape=jax.ShapeDtypeStruct(o_buf.shape, o_buf.dtype),
        input_output_aliases={5: 0},
        scratch_shapes=[pltpu.VMEM((t, 2 * DVA), BF16),
                        pltpu.VMEM((2, 2, tq, tk), F32),
                        pltpu.VMEM((2, tq, 1), F32), pltpu.VMEM((2, tq, 2 * DVA), F32)],
        compiler_params=_cparams(("parallel", "parallel", "arbitrary")),
        name="attn_prompt",
    )(qa, kd, vd, dl, g, o_buf)


def _attn_sample_kernel(q_ref, kn_ref, vn_ref, ck_ref, cv_ref, dl_ref, g_ref, o_buf, o_ref, *,
                        s_len, lam_init):
    del o_buf
    lam = _diff_lambda(dl_ref, lam_init)
    g = g_ref[...]
    p_len = ck_ref.shape[0]
    kc_all = ck_ref[...].reshape(p_len, WA).astype(BF16)
    vc_all = cv_ref[...].reshape(p_len, WA).astype(BF16)
    for h in range(HA):
        sl = slice(h * 2 * DA, (h + 1) * 2 * DA)
        q1, q2 = _split_maps(q_ref[:, sl])
        qq = jnp.concatenate([q1, q2], axis=0)
        kc = kc_all[:, sl]
        kn = kn_ref[:, sl]
        s_c = _dot_nt(qq, kc)
        s_n = _dot_nt(qq, kn)
        m = jnp.maximum(jnp.max(s_c, axis=-1, keepdims=True), jnp.max(s_n, axis=-1, keepdims=True))
        p_c = jnp.exp(s_c - m)
        p_n = jnp.exp(s_n - m)
        l = jnp.sum(p_c, axis=-1, keepdims=True) + jnp.sum(p_n, axis=-1, keepdims=True)
        o = (_dot(p_c.astype(BF16), vc_all[:, sl])
             + _dot(p_n.astype(BF16), vn_ref[:, sl])) / l
        od = o[:s_len] - lam * o[s_len:]
        o_ref[:, sl] = _subln(od, g, 1.0 - lam_init).astype(BF16)


def _attn_sample(qa, kd, vd, ck, cv, dl, g, o_buf, layer, nbs, s_len, row0, lam_init):
    p_len = ck.shape[2]
    rb0 = row0 // s_len
    new = pl.BlockSpec((s_len, WA), lambda b: (b, 0))
    cache = pl.BlockSpec((None, None, p_len, HA, DVA), lambda b: (layer, b, 0, 0, 0))
    return pl.pallas_call(
        functools.partial(_attn_sample_kernel, s_len=s_len, lam_init=lam_init),
        grid=(nbs,),
        in_specs=[
            pl.BlockSpec((s_len, WA), lambda b: (rb0 + b, 0)),
            new, new, cache, cache,
            pl.BlockSpec((4, DA), lambda b: (0, 0)),
            pl.BlockSpec((1, DVA), lambda b: (0, 0)),
            pl.BlockSpec(memory_space=pl.ANY),
        ],
        out_specs=pl.BlockSpec((s_len, WA), lambda b: (rb0 + b, 0)),
        out_shape=jax.ShapeDtypeStruct(o_buf.shape, o_buf.dtype),
        input_output_aliases={7: 0},
        compiler_params=_cparams(("parallel",)),
        name="attn_sample",
    )(qa, kd, vd, ck, cv, dl, g, o_buf)


def _ret_kernel(q_ref, k_ref, v_ref, gate_ref, r0_ref, o_buf, o_ref, rout_ref, st_ref, *, blk):
    del o_buf
    c = pl.program_id(1)

    @pl.when(c == 0)
    def _():
        st_ref[...] = jnp.zeros_like(st_ref)
        for h in range(HB):
            e = h % 2
            st_ref[h, e * DKB:(e + 1) * DKB, :] = r0_ref[h]

    n_i = lax.broadcasted_iota(jnp.int32, (blk, blk), 0)
    m_i = lax.broadcasted_iota(jnp.int32, (blk, blk), 1)
    diff = (n_i - m_i).astype(F32)
    causal = n_i >= m_i
    n_col = lax.broadcasted_iota(jnp.int32, (blk, LANE), 0).astype(F32)
    lane = lax.broadcasted_iota(jnp.int32, (blk, LANE), 1)
    for h in range(HB):
        log_g = math.log1p(-2.0 ** (-5.0 - h))
        p, e = h // 2, h % 2
        sl = slice(p * LANE, (p + 1) * LANE)
        mine = (lane >= DKB) if e else (lane < DKB)
        q = jnp.where(mine, q_ref[:, sl], 0.0)
        k = jnp.where(mine, k_ref[:, sl], 0.0)
        vsl = slice(h * DVB, (h + 1) * DVB)
        v = v_ref[:, vsl].astype(BF16)
        qb = q.astype(BF16)
        dec = jnp.where(causal, jnp.exp(jnp.maximum(diff, 0.0) * log_g), 0.0)
        s = _dot_nt(qb, k.astype(BF16)) * dec
        intra = _dot(s.astype(BF16), v)
        r_old = st_ref[h]
        cross = _dot(qb, r_old.astype(BF16)) * jnp.exp((n_col + 1.0) * log_g)
        kd = (k * jnp.exp((blk - 1.0 - n_col) * log_g)).astype(BF16)
        r_new = math.exp(blk * log_g) * r_old + _dot_tn(kd, v)
        st_ref[h] = r_new
        o = intra + cross
        ms = jnp.mean(o * o, axis=-1, keepdims=True)
        gt = gate_ref[:, vsl]
        o_ref[:, vsl] = (o * lax.rsqrt(ms + RET_EPS) * (gt * _sigmoid(gt))).astype(BF16)

    @pl.when(c == pl.num_programs(1) - 1)
    def _():
        for h in range(HB):
            e = h % 2
            rout_ref[h] = st_ref[h, e * DKB:(e + 1) * DKB, :]


def _retention(qkb, pab, r0, o_buf, nseg, seg_len, row0, blk):
    nc = seg_len // blk
    rb0 = row0 // blk
    vcol = (A_COLS + 2 * HB * DKB) // WB
    row = lambda b, c: rb0 + b * nc + c
    return pl.pallas_call(
        functools.partial(_ret_kernel, blk=blk),
        grid=(nseg, nc),
        in_specs=[
            pl.BlockSpec((blk, HB * DKB), lambda b, c: (row(b, c), 0)),
            pl.BlockSpec((blk, HB * DKB), lambda b, c: (row(b, c), 1)),
            pl.BlockSpec((blk, WB), lambda b, c: (row(b, c), vcol)),
            pl.BlockSpec((blk, WB), lambda b, c: (row(b, c), vcol + 1)),
            pl.BlockSpec((None, HB, DKB, DVB), lambda b, c: (b, 0, 0, 0)),
            pl.BlockSpec(memory_space=pl.ANY),
        ],
        out_specs=[
            pl.BlockSpec((blk, WB), lambda b, c: (row(b, c), 0)),
            pl.BlockSpec((None, HB, DKB, DVB), lambda b, c: (b, 0, 0, 0)),
        ],
        out_shape=[jax.ShapeDtypeStruct(o_buf.shape, o_buf.dtype),
                   jax.ShapeDtypeStruct((nseg, HB, DKB, DVB), F32)],
        input_output_aliases={5: 0},
        scratch_shapes=[pltpu.VMEM((HB, 2 * DKB, DVB), F32)],
        compiler_params=_cparams(("parallel", "arbitrary")),
        name="retention",
    )(qkb, qkb, pab, pab, r0, o_buf)


def _head_sum(x, ones_bd):
    hi, lo = _split2(x)
    return _dot(hi, ones_bd) + _dot(lo, ones_bd)


def _wkv_kernel(*refs, ck, nsb):
    pc_refs = refs[:nsb]
    (prev_ref, s0_ref, mu_ref, w0_ref, a0_ref, kk_ref, ka_ref, rk_ref, w2_ref, a2_ref, g2_ref,
     lng_ref, lnb_ref, tri_ref, ones_ref, o_ref, sout_ref, last_ref, st_ref, carry_ref) = refs[nsb:]
    c = pl.program_id(1)

    @pl.when(c == 0)
    def _():
        st_ref[...] = s0_ref[...]
        carry_ref[...] = prev_ref[...]

    row = lax.broadcasted_iota(jnp.int32, (ck, C_PAD), 0)
    pcs, shifted = [], []
    for g, pc_ref in enumerate(pc_refs):
        pc = pc_ref[...]
        pcs.append(pc)
        shifted.append(jnp.where(row == 0, carry_ref[g], pltpu.roll(pc, 1, 0)))
        carry_ref[g] = pc[ck - 1:ck, :]
        last_ref[g] = pc[ck - 1:ck, :]
    pc_all = jnp.concatenate(pcs, axis=0)
    pcm = pc_all + mu_ref[...] * (jnp.concatenate(shifted, axis=0) - pc_all)
    rc = pcm[:, 0:WC]
    kc = pcm[:, WC:2 * WC]
    vc = pcm[:, 2 * WC:3 * WC]
    lora = pcm[:, 3 * WC:C_PAD]
    z = w0_ref[...] + _dot(jnp.tanh(lora).astype(BF16), w2_ref[...])
    u = -z
    softplus = jnp.maximum(u, 0.0) + jnp.log1p(jnp.exp(-jnp.abs(u)))
    lw = -jnp.exp(-softplus - 0.5)
    a_sig = _sigmoid(a0_ref[...] + _dot(lora.astype(BF16), a2_ref[...]))
    gate = _dot(_sigmoid(lora).astype(BF16), g2_ref[...])
    kk = kc * kk_ref[...]
    kc2 = kc * (1.0 + (a_sig - 1.0) * ka_ref[...])
    rkr = rc * kc2 * rk_ref[...]
    ones_bd = ones_ref[...]

    tri = tri_ref[...]
    l1, l2, l3 = _split3(lw)
    cum = _dot(tri, l1) + (_dot(tri, l2) + _dot(tri, l3))
    total = jnp.concatenate(
        [jnp.broadcast_to(cum[(g + 1) * ck - 1:(g + 1) * ck, :], (ck, WC)) for g in range(nsb)], axis=0)
    e_neg = jnp.exp(-cum)
    e_rem = jnp.exp(total - cum)
    e_exc = jnp.exp(cum - lw)
    r_t = rc * jnp.exp(cum)
    k_t = kc2 * e_neg
    k_p = kc2 * e_rem
    p_tot = jnp.exp(total)

    lane = lax.broadcasted_iota(jnp.int32, (ck, LANE), 1)
    m0 = lane < NC

    def stack(x):
        return jnp.concatenate([jnp.where(m0, x, 0.0), jnp.where(m0, 0.0, x)], axis=0)

    n2 = 2 * ck
    ri = lax.broadcasted_iota(jnp.int32, (n2, n2), 0)
    ci = lax.broadcasted_iota(jnp.int32, (n2, n2), 1)
    same = (ri // ck) == (ci // ck)
    strict = jnp.logical_and(same, ri > ci)
    incl = jnp.logical_and(same, ri >= ci)

    bf = lambda t: t.astype(BF16)
    pairs = range(HC // 2)
    chains = [(g, p) for g in range(nsb) for p in pairs]
    a_s, r_s, b_s, k_s, v_s, bp_s, kp_s, bonus = {}, {}, {}, {}, {}, {}, {}, {}
    for p in pairs:
        sl = slice(p * LANE, (p + 1) * LANE)
        kkt = kk[:, sl]
        kn = kkt / jnp.maximum(jnp.sqrt(_head_sum(kkt * kkt, ones_bd)), 1e-12)
        bn = kn * a_sig[:, sl]
        a_tp = -kn * e_exc[:, sl]
        b_tp = bn * e_neg[:, sl]
        b_pp = bn * e_rem[:, sl]
        bonus_p = _head_sum(rkr[:, sl], ones_bd) * vc[:, sl]
        for g in range(nsb):
            rows = slice(g * ck, (g + 1) * ck)
            a_s[g, p] = bf(stack(a_tp[rows]))
            r_s[g, p] = bf(stack(r_t[rows, sl]))
            b_s[g, p] = bf(stack(b_tp[rows]))
            k_s[g, p] = bf(stack(k_t[rows, sl]))
            v_s[g, p] = _split2(stack(vc[rows, sl]))
            bp_s[g, p] = _split2(stack(b_pp[rows]))
            kp_s[g, p] = _split2(stack(k_p[rows, sl]))
            bonus[g, p] = bonus_p[rows]

    s_old = {gp: st_ref[gp[0], gp[1]] for gp in chains}
    s_b = {gp: bf(s_old[gp]) for gp in chains}
    if n2 % LANE == 0:
        big = {gp: _dot_nt(jnp.concatenate([a_s[gp], r_s[gp]], axis=0),
                           jnp.concatenate([b_s[gp], k_s[gp]], axis=0)) for gp in chains}
        ab = {gp: big[gp][0:n2, 0:n2] for gp in chains}
        ak = {gp: big[gp][0:n2, n2:2 * n2] for gp in chains}
        rb = {gp: big[gp][n2:2 * n2, 0:n2] for gp in chains}
        rk = {gp: big[gp][n2:2 * n2, n2:2 * n2] for gp in chains}
    else:
        ab = {gp: _dot_nt(a_s[gp], b_s[gp]) for gp in chains}
        ak = {gp: _dot_nt(a_s[gp], k_s[gp]) for gp in chains}
        rb = {gp: _dot_nt(r_s[gp], b_s[gp]) for gp in chains}
        rk = {gp: _dot_nt(r_s[gp], k_s[gp]) for gp in chains}
    lk = {gp: bf(jnp.where(strict, ab[gp], 0.0)) for gp in chains}
    l_ak = {gp: bf(jnp.where(strict, ak[gp], 0.0)) for gp in chains}
    l_rb = {gp: bf(jnp.where(incl, rb[gp], 0.0)) for gp in chains}
    l_rk = {gp: bf(jnp.where(incl, rk[gp], 0.0)) for gp in chains}
    x = {gp: _dot_nt(a_s[gp], s_b[gp]) + _dot(l_ak[gp], v_s[gp][0]) for gp in chains}
    y_part = {gp: _dot_nt(r_s[gp], s_b[gp]) + _dot(l_rk[gp], v_s[gp][0]) for gp in chains}
    span = 1
    while span < ck:
        x = {gp: x[gp] + _dot(lk[gp], bf(x[gp])) for gp in chains}
        span *= 2
        if span < ck:
            lk = {gp: bf(_dot(lk[gp], lk[gp])) for gp in chains}
    x_sp = {gp: _split2(x[gp]) for gp in chains}
    y_st = {gp: y_part[gp] + _dot(l_rb[gp], x_sp[gp][0]) for gp in chains}
    for g, p in chains:
        sl = slice(p * LANE, (p + 1) * LANE)
        decay = p_tot[(g + 1) * ck - 1:(g + 1) * ck, sl]
        uv = tuple(jnp.concatenate([xs, vs], axis=0) for xs, vs in zip(x_sp[g, p], v_s[g, p]))
        bk = tuple(jnp.concatenate([bs, ks], axis=0) for bs, ks in zip(bp_s[g, p], kp_s[g, p]))
        st_ref[g, p] = s_old[g, p] * decay + _mm3(uv, bk, _dot_tn)
    for g, p in chains:
        sl = slice(p * LANE, (p + 1) * LANE)
        rows = slice(g * ck, (g + 1) * ck)
        y = y_st[g, p][0:ck] + y_st[g, p][ck:n2]
        mean = _head_sum(y, ones_bd) * (1.0 / NC)
        d = y - mean
        var = _head_sum(d * d, ones_bd) * (1.0 / NC)
        yn = d * lax.rsqrt(var + GN_EPS_C) * lng_ref[:, sl] + lnb_ref[:, sl]
        o_ref[g, :, sl] = ((yn + bonus[g, p]) * gate[rows, sl]).astype(BF16)

    @pl.when(c == pl.num_programs(1) - 1)
    def _():
        sout_ref[...] = st_ref[...]


def _wkv(pc, prev, s0, vecs, mats, lng, lnb, ones_bd, nseg, seg_len, row0, ck, nsb):
    nc = seg_len // ck
    rb0 = row0 // ck
    pos = jnp.arange(nsb * ck)
    tri = ((pos[:, None] // ck == pos[None, :] // ck) & (pos[:, None] >= pos[None, :])).astype(BF16)
    pc_spec = lambda g: pl.BlockSpec((ck, C_PAD), lambda b, c: (rb0 + (b * nsb + g) * nc + c, 0))
    vec = lambda w: pl.BlockSpec((1, w), lambda b, c: (0, 0))
    mat = pl.BlockSpec((LORA_PAD, WC), lambda b, c: (0, 0))
    shift = pl.BlockSpec((nsb, 1, C_PAD), lambda b, c: (b, 0, 0))
    st = pl.BlockSpec((nsb, HC // 2, LANE, LANE), lambda b, c: (b, 0, 0, 0))
    return pl.pallas_call(
        functools.partial(_wkv_kernel, ck=ck, nsb=nsb),
        grid=(nseg // nsb, nc),
        in_specs=[pc_spec(g) for g in range(nsb)] + [
            shift, st,
            vec(C_PAD), vec(WC), vec(WC), vec(WC), vec(WC), vec(WC),
            mat, mat, mat, vec(WC), vec(WC),
            pl.BlockSpec((nsb * ck, nsb * ck), lambda b, c: (0, 0)),
            pl.BlockSpec((LANE, LANE), lambda b, c: (0, 0)),
        ],
        out_specs=[pl.BlockSpec((nsb, ck, WC), lambda b, c: (b, c, 0)), st, shift],
        out_shape=[jax.ShapeDtypeStruct((nseg, seg_len, WC), BF16),
                   jax.ShapeDtypeStruct((nseg, HC // 2, LANE, LANE), F32),
                   jax.ShapeDtypeStruct((nseg, 1, C_PAD), F32)],
        scratch_shapes=[pltpu.VMEM((nsb, HC // 2, LANE, LANE), F32), pltpu.VMEM((nsb, 1, C_PAD), F32)],
        compiler_params=_cparams(("parallel", "arbitrary")),
        name="wkv",
    )(*([pc] * nsb), prev, s0, *vecs, *mats, lng, lnb, tri, ones_bd)


def _state_to_blockdiag(s):
    b = s.shape[0]
    sp = s.reshape(b, HC // 2, 2, NC, NC)
    z = jnp.zeros_like(sp[:, :, 0])
    top = jnp.concatenate([sp[:, :, 0], z], axis=-1)
    bot = jnp.concatenate([z, sp[:, :, 1]], axis=-1)
    return jnp.concatenate([top, bot], axis=-2)


def _blockdiag_to_state(sb):
    b = sb.shape[0]
    h0 = sb[:, :, :NC, :NC]
    h1 = sb[:, :, NC:, NC:]
    return jnp.stack([h0, h1], axis=2).reshape(b, HC, NC, NC)


def _merge_kernel(oa_ref, ob_ref, oc_ref, ga_ref, gb_ref, gc_ref, wa_ref, wb_ref, wc_ref, o_ref):
    o_ref[...] = (ga_ref[...] * _dot(oa_ref[...], wa_ref[...])
                  + gb_ref[...] * _dot(ob_ref[...], wb_ref[...])
                  + gc_ref[...] * _dot(oc_ref[...], wc_ref[...]))


def _merge(oa, ob, oc, gates, wbr):
    n = oa.shape[0]
    tm = _pick(n, (1024, 512, 256, 128, 64, 32, 16, 8))
    tn = 512
    nj = D_MODEL // tn
    bw = oa.shape[1]
    o_spec = pl.BlockSpec((tm, bw), lambda i, j: (i, 0))
    g_spec = lambda k: pl.BlockSpec((tm, tn), lambda i, j: (i, k * nj + j))
    w_spec = pl.BlockSpec((bw, tn), lambda i, j: (0, j))
    return pl.pallas_call(
        _merge_kernel,
        grid=(n // tm, nj),
        in_specs=[o_spec, o_spec, o_spec, g_spec(0), g_spec(1), g_spec(2),
                  w_spec, w_spec, w_spec],
        out_specs=pl.BlockSpec((tm, tn), lambda i, j: (i, j)),
        out_shape=jax.ShapeDtypeStruct((n, D_MODEL), F32),
        compiler_params=_cparams(("parallel", "arbitrary")),
        name="merge",
    )(oa, ob, oc, gates, gates, gates, *wbr)


def _outproj_kernel(m_ref, x_ref, w_ref, g_ref, b_ref, o_ref, ob_ref, *, alpha):
    y = alpha * x_ref[...] + _dot(m_ref[...].astype(BF16), w_ref[...])
    out = _ln_rows(y, g_ref[...], b_ref[...], LN_EPS)
    o_ref[...] = out
    ob_ref[...] = out.astype(BF16)


def _outproj(merged, x, w, g, b, alpha):
    n = x.shape[0]
    tm = _pick(n, (256, 128, 64, 32, 16, 8))
    row = pl.BlockSpec((tm, D_MODEL), lambda i: (i, 0))
    vec = pl.BlockSpec((1, D_MODEL), lambda i: (0, 0))
    return pl.pallas_call(
        functools.partial(_outproj_kernel, alpha=alpha),
        grid=(n // tm,),
        in_specs=[row, row, pl.BlockSpec((D_MODEL, D_MODEL), lambda i: (0, 0)), vec, vec],
        out_specs=[row, row],
        out_shape=[jax.ShapeDtypeStruct((n, D_MODEL), F32), jax.ShapeDtypeStruct((n, D_MODEL), BF16)],
        compiler_params=_cparams(("parallel",)),
        name="outproj",
    )(merged, x, w, g, b)


def kernel(x_prompt, x_sample, cache_k, cache_v, state_ret, state_wkv, state_shift, w_in, w_gate,
           w_branch, w_out, ffn_in, ffn_out, ln_g, ln_b, diff_lambda, subln_g, mu_c, w0, w2, a0, a2,
           g2, k_k, k_a, r_k, lnx_g, lnx_b):
    nb, t, _ = x_prompt.shape
    nbs, s_len, _ = x_sample.shape
    depth = w_in.shape[0]
    p_len = cache_k.shape[2]
    n_p = nb * t
    n_s = nbs * s_len
    n = n_p + n_s
    alpha = (2 * depth) ** 0.25

    x = jnp.concatenate([x_prompt.reshape(n_p, D_MODEL), x_sample.reshape(n_s, D_MODEL)], axis=0)
    xb = x.astype(BF16)

    pos = jnp.concatenate([jnp.tile(jnp.arange(t, dtype=jnp.int32), nb),
                           jnp.tile(p_len + jnp.arange(s_len, dtype=jnp.int32), nbs)])
    tabs_a = _rope_tables(pos, ROT_A, ROPE_THETA, DA)
    tabs_b = _rope_tables(pos, DKB, RET_THETA, DKB)

    ck_all, cv_all = cache_k, cache_v

    eye = jnp.arange(LANE) // NC
    ones_bd = (eye[:, None] == eye[None, :]).astype(BF16)
    ck_p = _pick(t, (CHUNK,))
    ck_s = s_len
    nsb_p = _pick(nb, (2, 1))
    nsb_s = _pick(nbs, (4, 2, 1))
    ret_blk = _pick(t, (256, 128, 64))

    outs = {k: [] for k in ("rp", "wp", "sp", "rs", "ws", "ss")}

    qa = jnp.zeros((n, WA), BF16)
    qkb = jnp.zeros((n, 2 * HB * DKB), F32)
    k_p, v_p = (jnp.zeros((depth * n_p, HA, DVA), F32) for _ in range(2))
    k_s, v_s = (jnp.zeros((depth * n_s, HA, DVA), F32) for _ in range(2))
    oa, ob = jnp.zeros((n, WA), BF16), jnp.zeros((n, WB), BF16)

    for l in range(depth):
        lam_init = 0.8 - 0.6 * math.exp(-0.3 * l)

        def ffn_weights(i):
            return (*_cast_ffn_in(ffn_in, l, i), _cast_ffn_out(ffn_out, l, i))

        x, xb = _ffn(x, xb, *ffn_weights(0), ln_g[l, 0][None], ln_b[l, 0][None], alpha)

        w_ab = _cast(w_in, (l,), D_MODEL, A_COLS + B_COLS)
        w_c = _cast_wc(w_in, l)
        pab = _proj(xb, w_ab, 1024, name="proj_ab")
        pc = _proj(xb, w_c, C_PAD // 3, name="proj_c")
        w_g = _cast(w_gate, (l,), D_MODEL, N_BRANCH * D_MODEL)
        gates = _proj(xb, w_g, 1024, act="sigmoid", name="proj_gate")
        qa, k_p, v_p, qkb, kd_p, vd_p = _rope(pab, tabs_a, tabs_b, qa, k_p, v_p, qkb, 0, n_p, l * n_p)
        qa, k_s, v_s, qkb, kd_s, vd_s = _rope(pab, tabs_a, tabs_b, qa, k_s, v_s, qkb, n_p, n_s, l * n_s)

        dl = diff_lambda[l]
        sg = subln_g[l][None]
        oa = _attn_prompt(qa, kd_p, vd_p, dl, sg, oa, nb, t, lam_init)
        oa = _attn_sample(qa, kd_s, vd_s, ck_all, cv_all, dl, sg, oa, l, nbs, s_len, n_p, lam_init)

        ob, ret_p = _retention(qkb, pab, jnp.zeros((nb, HB, DKB, DVB), F32), ob, nb, t, 0, ret_blk)
        ob, ret_s = _retention(qkb, pab, state_ret[l], ob, nbs, s_len, n_p, s_len)

        padc = C_PAD - C_COLS
        vecs = (jnp.pad(mu_c[l], (0, padc))[None], w0[l][None], a0[l][None], k_k[l][None],
                k_a[l][None], r_k[l].reshape(1, WC))
        zrow = lambda r0, r1, m: jnp.pad(m.astype(BF16), ((r0, LORA_PAD - r1), (0, 0)))
        mats = (zrow(0, LORA_W, w2[l]), zrow(LORA_W, LORA_W + LORA_A, a2[l]),
                zrow(LORA_W + LORA_A, LORA_W + LORA_A + LORA_G, g2[l]))
        prev_p = jnp.zeros((nb, 1, C_PAD), F32)
        prev_s = jnp.pad(state_shift[l], ((0, 0), (0, 0), (0, padc)))
        lng, lnb = lnx_g[l][None], lnx_b[l][None]
        oc_p, wkv_p, shift_p = _wkv(pc, prev_p, jnp.zeros((nb, HC // 2, LANE, LANE), F32), vecs, mats,
                                          lng, lnb, ones_bd, nb, t, 0, ck_p, nsb_p)
        oc_s, wkv_s, shift_s = _wkv(pc, prev_s, _state_to_blockdiag(state_wkv[l]), vecs, mats,
                                          lng, lnb, ones_bd, nbs, s_len, n_p, ck_s, nsb_s)
        oc = jnp.concatenate([oc_p.reshape(n_p, WC), oc_s.reshape(n_s, WC)], axis=0)

        w_br = [_cast(w_branch, (l, k), WA, D_MODEL) for k in range(N_BRANCH)]
        merged = _merge(oa, ob, oc, gates, w_br)
        x, xb = _outproj(merged, x, _cast(w_out, (l,), D_MODEL, D_MODEL), ln_g[l, 1][None],
                         ln_b[l, 1][None], alpha)

        x, xb = _ffn(x, xb, *ffn_weights(1), ln_g[l, 2][None], ln_b[l, 2][None], alpha)

        outs["rp"].append(ret_p)
        outs["rs"].append(ret_s)
        outs["wp"].append(_blockdiag_to_state(wkv_p))
        outs["ws"].append(_blockdiag_to_state(wkv_s))
        outs["sp"].append(shift_p[:, :, :C_COLS])
        outs["ss"].append(shift_s[:, :, :C_COLS])

    st = {k: jnp.stack(v) for k, v in outs.items()}
    y_p = x[:n_p].reshape(nb, t, D_MODEL)
    y_s = x[n_p:].reshape(nbs, s_len, D_MODEL)
    kv_p = lambda a: a.reshape(depth, nb, t, HA, DVA)
    kv_s = lambda a: a.reshape(depth, nbs, s_len, HA, DVA)
    return (y_p, y_s, kv_p(k_p), kv_p(v_p), st["rp"], st["wp"], st["sp"],
            kv_s(k_s), kv_s(v_s), st["rs"], st["ws"], st["ss"])
```

```python
import functools
import math

import jax
import jax.numpy as jnp
from jax import lax
from jax.experimental import pallas as pl
from jax.experimental.pallas import tpu as pltpu

F32 = jnp.float32
BF16 = jnp.bfloat16

D_MODEL = 2048
CHUNK = 64
HA, DA, DVA = 8, 64, 128
ROT_A = DA // 4
ROPE_THETA = 500000.0
HB, DKB, DVB = 8, 64, 128
RET_THETA = 10000.0
HC, NC = 16, 64
LORA_W, LORA_A, LORA_G = 64, 64, 160
WA, WB, WC = HA * DVA, HB * DVB, HC * NC
N_BRANCH = 3
A_COLS = 2 * HA * 2 * DA + HA * DVA
B_COLS = 2 * HB * DKB + HB * DVB + WB
C_COLS = 3 * WC + LORA_W + LORA_A + LORA_G
D_FF = 5504
LN_EPS = 1e-5
SUBLN_EPS = 1e-5
RET_EPS = 1e-6
GN_EPS_C = 64e-5

LANE = 128
D_FF_PAD = 5632
FFN_TF = 512
LORA_PAD = 384
C_PAD = 3 * WC + LORA_PAD
VMEM_LIMIT = 56 * 1024 * 1024


def _cparams(sem):
    return pltpu.CompilerParams(dimension_semantics=sem, vmem_limit_bytes=VMEM_LIMIT)


def _pick(n, cands):
    for c in cands:
        if n % c == 0:
            return c
    raise ValueError(f"no tile for {n}")


def _sigmoid(x):
    return 1.0 / (1.0 + jnp.exp(-x))


def _ln_rows(y, g, b, eps):
    mu = jnp.mean(y, axis=-1, keepdims=True)
    d = y - mu
    var = jnp.mean(d * d, axis=-1, keepdims=True)
    return d * lax.rsqrt(var + eps) * g + b


def _dot(a, b):
    return jnp.dot(a, b, preferred_element_type=F32)


def _dot_nt(a, b):
    return lax.dot_general(a, b, (((1,), (1,)), ((), ())), preferred_element_type=F32)


def _dot_tn(a, b):
    return lax.dot_general(a, b, (((0,), (0,)), ((), ())), preferred_element_type=F32)


def _split2(x):
    hi = x.astype(BF16)
    lo = (x - hi.astype(F32)).astype(BF16)
    return hi, lo


def _split3(x):
    hi = x.astype(BF16)
    r1 = x - hi.astype(F32)
    mid = r1.astype(BF16)
    lo = (r1 - mid.astype(F32)).astype(BF16)
    return hi, mid, lo


def _mm3(a, b, dot=_dot):
    ah, al = a
    bh, bl = b
    return dot(ah, bh) + (dot(ah, bl) + dot(al, bh))


def _cast_kernel(w_ref, o_ref):
    o_ref[...] = w_ref[...].astype(BF16)


def _cast(src, lead, rows, width, col_block=0):
    tr = _pick(rows, (256, 128, 64, 32, 16, 8))
    return pl.pallas_call(
        _cast_kernel,
        grid=(rows // tr,),
        in_specs=[pl.BlockSpec((None,) * len(lead) + (tr, width), lambda i: lead + (i, col_block))],
        out_specs=pl.BlockSpec((tr, width), lambda i: (i, 0)),
        out_shape=jax.ShapeDtypeStruct((rows, width), BF16),
        compiler_params=_cparams(("parallel",)),
        name="cast",
    )(src)


def _cast_ffn_in_kernel(a_ref, b_ref, oa_ref, ob_ref):
    for src, dst in ((a_ref, oa_ref), (b_ref, ob_ref)):
        dst[:, 0:D_FF] = src[...].astype(BF16)
        dst[:, D_FF:D_FF_PAD] = jnp.zeros((dst.shape[0], D_FF_PAD - D_FF), BF16)


def _cast_ffn_in(ffn_in, l, i):
    tr = 128
    half = lambda k: pl.BlockSpec((None, None, tr, D_FF), lambda r: (l, i, r, k))
    out = pl.BlockSpec((tr, D_FF_PAD), lambda r: (r, 0))
    return pl.pallas_call(
        _cast_ffn_in_kernel,
        grid=(D_MODEL // tr,),
        in_specs=[half(0), half(1)],
        out_specs=[out, out],
        out_shape=[jax.ShapeDtypeStruct((D_MODEL, D_FF_PAD), BF16)] * 2,
        compiler_params=_cparams(("parallel",)),
        name="cast_ffn_in",
    )(ffn_in, ffn_in)


def _cast_ffn_out_kernel(w_ref, o_ref):
    tr = w_ref.shape[0]
    row = lax.broadcasted_iota(jnp.int32, w_ref.shape, 0) + pl.program_id(0) * tr
    o_ref[...] = jnp.where(row < D_FF, w_ref[...], 0.0).astype(BF16)


def _cast_ffn_out(ffn_out, l, i):
    tr = FFN_TF
    return pl.pallas_call(
        _cast_ffn_out_kernel,
        grid=(D_FF_PAD // tr,),
        in_specs=[pl.BlockSpec((None, None, tr, D_MODEL), lambda r: (l, i, r, 0))],
        out_specs=pl.BlockSpec((tr, D_MODEL), lambda r: (r, 0)),
        out_shape=jax.ShapeDtypeStruct((D_FF_PAD, D_MODEL), BF16),
        compiler_params=_cparams(("parallel",)),
        name="cast_ffn_out",
    )(ffn_out)


def _cast_wc_kernel(w_ref, o_ref):
    lane = lax.broadcasted_iota(jnp.int32, w_ref.shape, 1)
    valid = lane < C_COLS - pl.program_id(1) * LORA_PAD
    o_ref[...] = jnp.where(valid, w_ref[...], 0.0).astype(BF16)


def _cast_wc(w_in, l):
    tr = 512
    cb0 = (A_COLS + B_COLS) // LORA_PAD
    return pl.pallas_call(
        _cast_wc_kernel,
        grid=(D_MODEL // tr, C_PAD // LORA_PAD),
        in_specs=[pl.BlockSpec((None, tr, LORA_PAD), lambda r, j: (l, r, cb0 + j))],
        out_specs=pl.BlockSpec((tr, LORA_PAD), lambda r, j: (r, j)),
        out_shape=jax.ShapeDtypeStruct((D_MODEL, C_PAD), BF16),
        compiler_params=_cparams(("parallel", "parallel")),
        name="cast_wc",
    )(w_in)


def _ffn_kernel(x_ref, xb_ref, wa_ref, wb_ref, wd_ref, g_ref, b_ref, o_ref, ob_ref, acc_ref, *, alpha):
    j = pl.program_id(1)

    @pl.when(j == 0)
    def _():
        acc_ref[...] = jnp.zeros_like(acc_ref)

    xb = xb_ref[...]
    ha = _dot(xb, wa_ref[...])
    hb = _dot(xb, wb_ref[...])
    act = (ha * _sigmoid(ha) * hb).astype(BF16)
    acc_ref[...] += _dot(act, wd_ref[...])

    @pl.when(j == pl.num_programs(1) - 1)
    def _():
        y = alpha * x_ref[...] + 0.5 * acc_ref[...]
        out = _ln_rows(y, g_ref[...], b_ref[...], LN_EPS)
        o_ref[...] = out
        ob_ref[...] = out.astype(BF16)


def _ffn(x, xb, wa, wb, wd, g, b, alpha):
    n = x.shape[0]
    tm = _pick(n, (512, 256, 128, 64, 32, 16, 8))
    tf = FFN_TF
    grid = (n // tm, D_FF_PAD // tf)
    row = pl.BlockSpec((tm, D_MODEL), lambda i, j: (i, 0))
    vec = pl.BlockSpec((1, D_MODEL), lambda i, j: (0, 0))
    return pl.pallas_call(
        functools.partial(_ffn_kernel, alpha=alpha),
        grid=grid,
        in_specs=[
            row, row,
            pl.BlockSpec((D_MODEL, tf), lambda i, j: (0, j)),
            pl.BlockSpec((D_MODEL, tf), lambda i, j: (0, j)),
            pl.BlockSpec((tf, D_MODEL), lambda i, j: (j, 0)),
            vec, vec,
        ],
        out_specs=[row, row],
        out_shape=[jax.ShapeDtypeStruct((n, D_MODEL), F32), jax.ShapeDtypeStruct((n, D_MODEL), BF16)],
        scratch_shapes=[pltpu.VMEM((tm, D_MODEL), F32)],
        compiler_params=_cparams(("parallel", "arbitrary")),
        name="ffn",
    )(x, xb, wa, wb, wd, g, b)


def _proj_kernel(x_ref, w_ref, o_ref, *, act):
    acc = _dot(x_ref[...], w_ref[...])
    if act == "sigmoid":
        acc = _sigmoid(acc)
    o_ref[...] = acc


def _proj(xb, w, tn, act=None, name="proj", lead=(), m=None):
    n = xb.shape[0]
    m = w.shape[-1] if m is None else m
    tm = _pick(n, (1024, 512, 256, 128, 64, 32, 16, 8))
    return pl.pallas_call(
        functools.partial(_proj_kernel, act=act),
        grid=(n // tm, m // tn),
        in_specs=[
            pl.BlockSpec((tm, D_MODEL), lambda i, j: (i, 0)),
            pl.BlockSpec((None,) * len(lead) + (D_MODEL, tn), lambda i, j: lead + (0, j)),
        ],
        out_specs=pl.BlockSpec((tm, tn), lambda i, j: (i, j)),
        out_shape=jax.ShapeDtypeStruct((n, m), F32),
        compiler_params=_cparams(("parallel", "arbitrary")),
        name=name,
    )(xb, w)


def _rope_tables(pos, rot, theta, width):
    half = rot // 2
    freq = jnp.power(theta, -jnp.arange(half, dtype=F32) / half)
    ang = pos.astype(F32)[:, None] * freq[None, :]
    cos, sin = jnp.cos(ang), jnp.sin(ang)
    n = pos.shape[0]
    pad = width - rot
    c = jnp.concatenate([cos, cos, jnp.ones((n, pad), F32)], axis=-1)
    s_lo = jnp.concatenate([-sin, jnp.zeros((n, half + pad), F32)], axis=-1)
    s_hi = jnp.concatenate([jnp.zeros((n, half), F32), sin, jnp.zeros((n, pad), F32)], axis=-1)
    rep = LANE // width
    return tuple(jnp.tile(t, (1, rep)) for t in (c, s_lo, s_hi))


def _rope_kernel(xa_ref, xv_ref, xb_ref, ca_ref, la_ref, ha_ref, cb_ref, lb_ref, hb_ref,
                 qa_buf, kall_buf, vall_buf, qkb_buf, qa_ref, kall_ref, vall_ref, qkb_ref, kd_ref, vd_ref):
    del qa_buf, kall_buf, vall_buf, qkb_buf

    def rot(x, c, lo, hi, half):
        return x * c + pltpu.roll(x, LANE - half, 1) * lo + pltpu.roll(x, half, 1) * hi

    ca, la, ha = ca_ref[...], la_ref[...], ha_ref[...]
    nqa = HA * 2 * DA // LANE
    tm = xa_ref.shape[0]
    ks = []
    for t in range(2 * nqa):
        y = rot(xa_ref[:, t * LANE:(t + 1) * LANE], ca, la, ha, ROT_A // 2)
        if t < nqa:
            qa_ref[:, t * LANE:(t + 1) * LANE] = (y * (DA ** -0.5)).astype(BF16)
        else:
            ks.append(y)
            kd_ref[:, (t - nqa) * LANE:(t - nqa + 1) * LANE] = y.astype(BF16)
    kall_ref[...] = jnp.concatenate(ks, axis=1).reshape(tm, HA, 2 * DA)
    v = xv_ref[...]
    vall_ref[...] = v.reshape(tm, HA, DVA)
    vd_ref[...] = v.astype(BF16)
    cb, lb, hb = cb_ref[...], lb_ref[...], hb_ref[...]
    nqb = HB * DKB // LANE
    for t in range(2 * nqb):
        sl = slice(t * LANE, (t + 1) * LANE)
        y = rot(xb_ref[:, sl], cb, lb, hb, DKB // 2)
        if t >= nqb:
            y = y * (DKB ** -0.5)
        qkb_ref[:, sl] = y


def _rope(pab, tabs_a, tabs_b, qa_buf, kall_buf, vall_buf, qkb_buf, row0, nrows, kv_row0):
    tm = _pick(nrows, (256, 128, 64, 32, 16, 8))
    rb0, kb0 = row0 // tm, kv_row0 // tm
    wqk = 2 * HA * 2 * DA
    wb = 2 * HB * DKB
    tab = pl.BlockSpec((tm, LANE), lambda i: (rb0 + i, 0))
    anyspec = pl.BlockSpec(memory_space=pl.ANY)
    bufs = (qa_buf, kall_buf, vall_buf, qkb_buf)
    cache = pl.BlockSpec((tm, HA, DVA), lambda i: (kb0 + i, 0, 0))
    dense = pl.BlockSpec((tm, WA), lambda i: (i, 0))
    return pl.pallas_call(
        _rope_kernel,
        grid=(nrows // tm,),
        in_specs=[
            pl.BlockSpec((tm, wqk), lambda i: (rb0 + i, 0)),
            pl.BlockSpec((tm, WA), lambda i: (rb0 + i, wqk // WA)),
            pl.BlockSpec((tm, wb), lambda i: (rb0 + i, A_COLS // wb)),
            tab, tab, tab, tab, tab, tab,
            anyspec, anyspec, anyspec, anyspec,
        ],
        out_specs=[
            pl.BlockSpec((tm, WA), lambda i: (rb0 + i, 0)),
            cache, cache,
            pl.BlockSpec((tm, wb), lambda i: (rb0 + i, 0)),
            dense, dense,
        ],
        out_shape=[jax.ShapeDtypeStruct(b.shape, b.dtype) for b in bufs]
                  + [jax.ShapeDtypeStruct((nrows, WA), BF16)] * 2,
        input_output_aliases={9: 0, 10: 1, 11: 2, 12: 3},
        compiler_params=_cparams(("parallel",)),
        name="rope",
    )(pab, pab, pab, *tabs_a, *tabs_b, *bufs)


def _diff_lambda(dl_ref, lam_init):
    lv = dl_ref[...]
    s1 = jnp.sum(lv[0:1] * lv[1:2], axis=-1, keepdims=True)
    s2 = jnp.sum(lv[2:3] * lv[3:4], axis=-1, keepdims=True)
    return jnp.exp(s1) - jnp.exp(s2) + lam_init


def _split_maps(q):
    lane = lax.broadcasted_iota(jnp.int32, q.shape, 1)
    zero = jnp.zeros_like(q)
    return jnp.where(lane < DA, q, zero), jnp.where(lane >= DA, q, zero)


def _subln(o, g, scale):
    ms = jnp.mean(o * o, axis=-1, keepdims=True)
    return o * lax.rsqrt(ms + SUBLN_EPS) * g * scale


def _attn_prompt_kernel(q_ref, kb_ref, v_ref, dl_ref, g_ref, o_buf, o_ref, vb_ref, s_ref,
                        m_ref, acc_ref, *, tq, tk, lam_init):
    del o_buf
    qi = pl.program_id(2)

    @pl.when(qi == 0)
    def _():
        vb_ref[:, 0:DVA] = v_ref[...]
        vb_ref[:, DVA:2 * DVA] = jnp.ones(v_ref.shape, BF16)

    lam = _diff_lambda(dl_ref, lam_init)
    qs = _split_maps(q_ref[...])

    def scores(kb, slot):
        k = kb_ref[pl.ds(pl.multiple_of(kb * tk, tk), tk), :]
        for i, qm in enumerate(qs):
            s_ref[slot, i] = _dot_nt(qm, k)

    def consume(ss, start, width):
        v1 = vb_ref[pl.ds(start, width), :]
        stats = []
        for i, s in enumerate(ss):
            m = m_ref[i]
            m_new = jnp.maximum(m, jnp.max(s, axis=-1, keepdims=True))
            m_ref[i] = m_new
            stats.append((jnp.exp(m - m_new), jnp.exp(s - m_new).astype(BF16)))
        pvs = [_dot(p, v1) for _, p in stats]
        for i, ((a, _), pv) in enumerate(zip(stats, pvs)):
            acc_ref[i] = a * acc_ref[i] + pv

    def full_step(kb, slot):
        scores(kb + 1, 1 - slot)
        consume((s_ref[slot, 0], s_ref[slot, 1]), pl.multiple_of(kb * tk, tk), tk)

    q0 = qi * tq
    n_full = q0 // tk
    m_ref[...] = jnp.full(m_ref.shape, -1e30, F32)
    acc_ref[...] = jnp.zeros_like(acc_ref)
    scores(0, 0)

    def two_steps(j, _):
        full_step(2 * j, 0)
        full_step(2 * j + 1, 1)
        return 0

    lax.fori_loop(0, n_full // 2, two_steps, 0)

    @pl.when(n_full % 2 == 1)
    def _():
        full_step(n_full - 1, 0)

    tail = pl.multiple_of(n_full * tk, tk)
    g = g_ref[...]

    for r in range(tk // tq):
        @pl.when(q0 - tail == r * tq)
        def _():
            width = (r + 1) * tq
            row = lax.broadcasted_iota(jnp.int32, (tq, width), 0) // CHUNK
            col = lax.broadcasted_iota(jnp.int32, (tq, width), 1) // CHUNK
            ok = col - row <= r * (tq // CHUNK)
            consume(tuple(jnp.where(ok, s_ref[n_full % 2, i, :, 0:width], -1e30) for i in range(2)),
                    tail, width)
            a1, a2 = acc_ref[0], acc_ref[1]
            o = a1[:, 0:DVA] / a1[:, DVA:2 * DVA] - lam * (a2[:, 0:DVA] / a2[:, DVA:2 * DVA])
            o_ref[...] = _subln(o, g, 1.0 - lam_init).astype(BF16)


def _attn_prompt(qa, kd, vd, dl, g, o_buf, nb, t, lam_init):
    tq = _pick(t, (512, 256, 128, 64))
    tk = _pick(t, (1024, 512, 256, 128, 64))
    nq = t // tq
    kv = pl.BlockSpec((t, 2 * DA), lambda b, h, i: (b, h))
    return pl.pallas_call(
        functools.partial(_attn_prompt_kernel, tq=tq, tk=tk, lam_init=lam_init),
        grid=(nb, HA, nq),
        in_specs=[
            pl.BlockSpec((tq, 2 * DA), lambda b, h, i: (b * nq + i, h)),
            kv, kv,
            pl.BlockSpec((4, DA), lambda b, h, i: (0, 0)),
            pl.BlockSpec((1, DVA), lambda b, h, i: (0, 0)),
            pl.BlockSpec(memory_space=pl.ANY),
        ],
        out_specs=pl.BlockSpec((tq, DVA), lambda b, h, i: (b * nq + i, h)),
        out_shape=jax.ShapeDtypeStruct(o_buf.shape, o_buf.dtype),
        input_output_aliases={5: 0},
        scratch_shapes=[pltpu.VMEM((t, 2 * DVA), BF16),
                        pltpu.VMEM((2, 2, tq, tk), F32),
                        pltpu.VMEM((2, tq, 1), F32), pltpu.VMEM((2, tq, 2 * DVA), F32)],
        compiler_params=_cparams(("parallel", "parallel", "arbitrary")),
        name="attn_prompt",
    )(qa, kd, vd, dl, g, o_buf)


def _attn_sample_kernel(q_ref, kn_ref, vn_ref, ck_ref, cv_ref, dl_ref, g_ref, o_buf, o_ref, *,
                        s_len, lam_init):
    del o_buf
    lam = _diff_lambda(dl_ref, lam_init)
    g = g_ref[...]
    p_len = ck_ref.shape[0]
    kc_all = ck_ref[...].reshape(p_len, WA).astype(BF16)
    vc_all = cv_ref[...].reshape(p_len, WA).astype(BF16)
    for h in range(HA):
        sl = slice(h * 2 * DA, (h + 1) * 2 * DA)
        q1, q2 = _split_maps(q_ref[:, sl])
        qq = jnp.concatenate([q1, q2], axis=0)
        kc = kc_all[:, sl]
        kn = kn_ref[:, sl]
        s_c = _dot_nt(qq, kc)
        s_n = _dot_nt(qq, kn)
        m = jnp.maximum(jnp.max(s_c, axis=-1, keepdims=True), jnp.max(s_n, axis=-1, keepdims=True))
        p_c = jnp.exp(s_c - m)
        p_n = jnp.exp(s_n - m)
        l = jnp.sum(p_c, axis=-1, keepdims=True) + jnp.sum(p_n, axis=-1, keepdims=True)
        o = (_dot(p_c.astype(BF16), vc_all[:, sl])
             + _dot(p_n.astype(BF16), vn_ref[:, sl])) / l
        od = o[:s_len] - lam * o[s_len:]
        o_ref[:, sl] = _subln(od, g, 1.0 - lam_init).astype(BF16)


def _attn_sample(qa, kd, vd, ck, cv, dl, g, o_buf, layer, nbs, s_len, row0, lam_init):
    p_len = ck.shape[2]
    rb0 = row0 // s_len
    new = pl.BlockSpec((s_len, WA), lambda b: (b, 0))
    cache = pl.BlockSpec((None, None, p_len, HA, DVA), lambda b: (layer, b, 0, 0, 0))
    return pl.pallas_call(
        functools.partial(_attn_sample_kernel, s_len=s_len, lam_init=lam_init),
        grid=(nbs,),
        in_specs=[
            pl.BlockSpec((s_len, WA), lambda b: (rb0 + b, 0)),
            new, new, cache, cache,
            pl.BlockSpec((4, DA), lambda b: (0, 0)),
            pl.BlockSpec((1, DVA), lambda b: (0, 0)),
            pl.BlockSpec(memory_space=pl.ANY),
        ],
        out_specs=pl.BlockSpec((s_len, WA), lambda b: (rb0 + b, 0)),
        out_shape=jax.ShapeDtypeStruct(o_buf.shape, o_buf.dtype),
        input_output_aliases={7: 0},
        compiler_params=_cparams(("parallel",)),
        name="attn_sample",
    )(qa, kd, vd, ck, cv, dl, g, o_buf)


def _ret_kernel(q_ref, k_ref, v_ref, gate_ref, r0_ref, o_buf, o_ref, rout_ref, st_ref, *, blk):
    del o_buf
    c = pl.program_id(1)

    @pl.when(c == 0)
    def _():
        st_ref[...] = jnp.zeros_like(st_ref)
        for h in range(HB):
            e = h % 2
            st_ref[h, e * DKB:(e + 1) * DKB, :] = r0_ref[h]

    n_i = lax.broadcasted_iota(jnp.int32, (blk, blk), 0)
    m_i = lax.broadcasted_iota(jnp.int32, (blk, blk), 1)
    diff = (n_i - m_i).astype(F32)
    causal = n_i >= m_i
    n_col = lax.broadcasted_iota(jnp.int32, (blk, LANE), 0).astype(F32)
    lane = lax.broadcasted_iota(jnp.int32, (blk, LANE), 1)
    for h in range(HB):
        log_g = math.log1p(-2.0 ** (-5.0 - h))
        p, e = h // 2, h % 2
        sl = slice(p * LANE, (p + 1) * LANE)
        mine = (lane >= DKB) if e else (lane < DKB)
        q = jnp.where(mine, q_ref[:, sl], 0.0)
        k = jnp.where(mine, k_ref[:, sl], 0.0)
        vsl = slice(h * DVB, (h + 1) * DVB)
        v = v_ref[:, vsl].astype(BF16)
        qb = q.astype(BF16)
        dec = jnp.where(causal, jnp.exp(jnp.maximum(diff, 0.0) * log_g), 0.0)
        s = _dot_nt(qb, k.astype(BF16)) * dec
        intra = _dot(s.astype(BF16), v)
        r_old = st_ref[h]
        cross = _dot(qb, r_old.astype(BF16)) * jnp.exp((n_col + 1.0) * log_g)
        kd = (k * jnp.exp((blk - 1.0 - n_col) * log_g)).astype(BF16)
        r_new = math.exp(blk * log_g) * r_old + _dot_tn(kd, v)
        st_ref[h] = r_new
        o = intra + cross
        ms = jnp.mean(o * o, axis=-1, keepdims=True)
        gt = gate_ref[:, vsl]
        o_ref[:, vsl] = (o * lax.rsqrt(ms + RET_EPS) * (gt * _sigmoid(gt))).astype(BF16)

    @pl.when(c == pl.num_programs(1) - 1)
    def _():
        for h in range(HB):
            e = h % 2
            rout_ref[h] = st_ref[h, e * DKB:(e + 1) * DKB, :]


def _retention(qkb, pab, r0, o_buf, nseg, seg_len, row0, blk):
    nc = seg_len // blk
    rb0 = row0 // blk
    vcol = (A_COLS + 2 * HB * DKB) // WB
    row = lambda b, c: rb0 + b * nc + c
    return pl.pallas_call(
        functools.partial(_ret_kernel, blk=blk),
        grid=(nseg, nc),
        in_specs=[
            pl.BlockSpec((blk, HB * DKB), lambda b, c: (row(b, c), 0)),
            pl.BlockSpec((blk, HB * DKB), lambda b, c: (row(b, c), 1)),
            pl.BlockSpec((blk, WB), lambda b, c: (row(b, c), vcol)),
            pl.BlockSpec((blk, WB), lambda b, c: (row(b, c), vcol + 1)),
            pl.BlockSpec((None, HB, DKB, DVB), lambda b, c: (b, 0, 0, 0)),
            pl.BlockSpec(memory_space=pl.ANY),
        ],
        out_specs=[
            pl.BlockSpec((blk, WB), lambda b, c: (row(b, c), 0)),
            pl.BlockSpec((None, HB, DKB, DVB), lambda b, c: (b, 0, 0, 0)),
        ],
        out_shape=[jax.ShapeDtypeStruct(o_buf.shape, o_buf.dtype),
                   jax.ShapeDtypeStruct((nseg, HB, DKB, DVB), F32)],
        input_output_aliases={5: 0},
        scratch_shapes=[pltpu.VMEM((HB, 2 * DKB, DVB), F32)],
        compiler_params=_cparams(("parallel", "arbitrary")),
        name="retention",
    )(qkb, qkb, pab, pab, r0, o_buf)


def _head_sum(x, ones_bd):
    hi, lo = _split2(x)
    return _dot(hi, ones_bd) + _dot(lo, ones_bd)


def _wkv_kernel(*refs, ck, nsb):
    pc_refs = refs[:nsb]
    (prev_ref, s0_ref, mu_ref, w0_ref, a0_ref, kk_ref, ka_ref, rk_ref, w2_ref, a2_ref, g2_ref,
     lng_ref, lnb_ref, tri_ref, ones_ref, o_ref, sout_ref, last_ref, st_ref, carry_ref) = refs[nsb:]
    c = pl.program_id(1)

    @pl.when(c == 0)
    def _():
        zero = jnp.zeros((NC, NC), F32)
        for g in range(nsb):
            for p in range(HC // 2):
                top = jnp.concatenate([s0_ref[g, 2 * p], zero], axis=1)
                bot = jnp.concatenate([zero, s0_ref[g, 2 * p + 1]], axis=1)
                st_ref[g, p] = jnp.concatenate([top, bot], axis=0)
        carry_ref[...] = prev_ref[...]

    row = lax.broadcasted_iota(jnp.int32, (ck, C_PAD), 0)
    pcs, shifted = [], []
    for g, pc_ref in enumerate(pc_refs):
        pc = pc_ref[...]
        pcs.append(pc)
        shifted.append(jnp.where(row == 0, carry_ref[g], pltpu.roll(pc, 1, 0)))
        carry_ref[g] = pc[ck - 1:ck, :]
        last_ref[g] = pc[ck - 1:ck, :]
    pc_all = jnp.concatenate(pcs, axis=0)
    pcm = pc_all + mu_ref[...] * (jnp.concatenate(shifted, axis=0) - pc_all)
    rc = pcm[:, 0:WC]
    kc = pcm[:, WC:2 * WC]
    vc = pcm[:, 2 * WC:3 * WC]
    lora = pcm[:, 3 * WC:C_PAD]
    z = w0_ref[...] + _dot(jnp.tanh(lora).astype(BF16), w2_ref[...])
    u = -z
    softplus = jnp.maximum(u, 0.0) + jnp.log1p(jnp.exp(-jnp.abs(u)))
    lw = -jnp.exp(-softplus - 0.5)
    a_sig = _sigmoid(a0_ref[...] + _dot(lora.astype(BF16), a2_ref[...]))
    gate = _dot(_sigmoid(lora).astype(BF16), g2_ref[...])
    kk = kc * kk_ref[...]
    kc2 = kc * (1.0 + (a_sig - 1.0) * ka_ref[...])
    rkr = rc * kc2 * rk_ref[...]
    ones_bd = ones_ref[...]

    tri = tri_ref[...]
    l1, l2, l3 = _split3(lw)
    cum = _dot(tri, l1) + (_dot(tri, l2) + _dot(tri, l3))
    total = jnp.concatenate(
        [jnp.broadcast_to(cum[(g + 1) * ck - 1:(g + 1) * ck, :], (ck, WC)) for g in range(nsb)], axis=0)
    e_neg = jnp.exp(-cum)
    e_rem = jnp.exp(total - cum)
    e_exc = jnp.exp(cum - lw)
    r_t = rc * jnp.exp(cum)
    k_t = kc2 * e_neg
    k_p = kc2 * e_rem
    p_tot = jnp.exp(total)

    lane = lax.broadcasted_iota(jnp.int32, (ck, LANE), 1)
    m0 = lane < NC

    def stack(x):
        return jnp.concatenate([jnp.where(m0, x, 0.0), jnp.where(m0, 0.0, x)], axis=0)

    n2 = 2 * ck
    ri = lax.broadcasted_iota(jnp.int32, (n2, n2), 0)
    ci = lax.broadcasted_iota(jnp.int32, (n2, n2), 1)
    same = (ri // ck) == (ci // ck)
    strict = jnp.logical_and(same, ri > ci)
    incl = jnp.logical_and(same, ri >= ci)

    bf = lambda t: t.astype(BF16)
    pairs = range(HC // 2)
    chains = [(g, p) for g in range(nsb) for p in pairs]
    a_s, r_s, b_s, k_s, v_s, bp_s, kp_s, bonus = {}, {}, {}, {}, {}, {}, {}, {}
    for p in pairs:
        sl = slice(p * LANE, (p + 1) * LANE)
        kkt = kk[:, sl]
        kn = kkt / jnp.maximum(jnp.sqrt(_head_sum(kkt * kkt, ones_bd)), 1e-12)
        bn = kn * a_sig[:, sl]
        a_tp = -kn * e_exc[:, sl]
        b_tp = bn * e_neg[:, sl]
        b_pp = bn * e_rem[:, sl]
        bonus_p = _head_sum(rkr[:, sl], ones_bd) * vc[:, sl]
        for g in range(nsb):
            rows = slice(g * ck, (g + 1) * ck)
            a_s[g, p] = bf(stack(a_tp[rows]))
            r_s[g, p] = bf(stack(r_t[rows, sl]))
            b_s[g, p] = bf(stack(b_tp[rows]))
            k_s[g, p] = bf(stack(k_t[rows, sl]))
            v_s[g, p] = _split2(stack(vc[rows, sl]))
            bp_s[g, p] = _split2(stack(b_pp[rows]))
            kp_s[g, p] = _split2(stack(k_p[rows, sl]))
            bonus[g, p] = bonus_p[rows]

    s_old = {gp: st_ref[gp[0], gp[1]] for gp in chains}
    s_b = {gp: bf(s_old[gp]) for gp in chains}
    if n2 % LANE == 0:
        big = {gp: _dot_nt(jnp.concatenate([a_s[gp], r_s[gp]], axis=0),
                           jnp.concatenate([b_s[gp], k_s[gp]], axis=0)) for gp in chains}
        ab = {gp: big[gp][0:n2, 0:n2] for gp in chains}
        ak = {gp: big[gp][0:n2, n2:2 * n2] for gp in chains}
        rb = {gp: big[gp][n2:2 * n2, 0:n2] for gp in chains}
        rk = {gp: big[gp][n2:2 * n2, n2:2 * n2] for gp in chains}
    else:
        ab = {gp: _dot_nt(a_s[gp], b_s[gp]) for gp in chains}
        ak = {gp: _dot_nt(a_s[gp], k_s[gp]) for gp in chains}
        rb = {gp: _dot_nt(r_s[gp], b_s[gp]) for gp in chains}
        rk = {gp: _dot_nt(r_s[gp], k_s[gp]) for gp in chains}
    lk = {gp: bf(jnp.where(strict, ab[gp], 0.0)) for gp in chains}
    l_ak = {gp: bf(jnp.where(strict, ak[gp], 0.0)) for gp in chains}
    l_rb = {gp: bf(jnp.where(incl, rb[gp], 0.0)) for gp in chains}
    l_rk = {gp: bf(jnp.where(incl, rk[gp], 0.0)) for gp in chains}
    x = {gp: _dot_nt(a_s[gp], s_b[gp]) + _dot(l_ak[gp], v_s[gp][0]) for gp in chains}
    y_part = {gp: _dot_nt(r_s[gp], s_b[gp]) + _dot(l_rk[gp], v_s[gp][0]) for gp in chains}
    span = 1
    while span < ck:
        x = {gp: x[gp] + _dot(lk[gp], bf(x[gp])) for gp in chains}
        span *= 2
        if span < ck:
            lk = {gp: bf(_dot(lk[gp], lk[gp])) for gp in chains}
    x_sp = {gp: _split2(x[gp]) for gp in chains}
    y_st = {gp: y_part[gp] + _dot(l_rb[gp], x_sp[gp][0]) for gp in chains}
    for g, p in chains:
        sl = slice(p * LANE, (p + 1) * LANE)
        decay = p_tot[(g + 1) * ck - 1:(g + 1) * ck, sl]
        uv = tuple(jnp.concatenate([xs, vs], axis=0) for xs, vs in zip(x_sp[g, p], v_s[g, p]))
        bk = tuple(jnp.concatenate([bs, ks], axis=0) for bs, ks in zip(bp_s[g, p], kp_s[g, p]))
        st_ref[g, p] = s_old[g, p] * decay + _mm3(uv, bk, _dot_tn)
    for g, p in chains:
        sl = slice(p * LANE, (p + 1) * LANE)
        rows = slice(g * ck, (g + 1) * ck)
        y = y_st[g, p][0:ck] + y_st[g, p][ck:n2]
        mean = _head_sum(y, ones_bd) * (1.0 / NC)
        d = y - mean
        var = _head_sum(d * d, ones_bd) * (1.0 / NC)
        yn = d * lax.rsqrt(var + GN_EPS_C) * lng_ref[:, sl] + lnb_ref[:, sl]
        o_ref[g, :, sl] = ((yn + bonus[g, p]) * gate[rows, sl]).astype(BF16)

    @pl.when(c == pl.num_programs(1) - 1)
    def _():
        for g in range(nsb):
            for p in range(HC // 2):
                s_pair = st_ref[g, p]
                sout_ref[g, 2 * p] = s_pair[0:NC, 0:NC]
                sout_ref[g, 2 * p + 1] = s_pair[NC:2 * NC, NC:2 * NC]


def _wkv(pc, prev, s0, vecs, mats, lng, lnb, ones_bd, nseg, seg_len, row0, ck, nsb):
    nc = seg_len // ck
    rb0 = row0 // ck
    pos = jnp.arange(nsb * ck)
    tri = ((pos[:, None] // ck == pos[None, :] // ck) & (pos[:, None] >= pos[None, :])).astype(BF16)
    pc_spec = lambda g: pl.BlockSpec((ck, C_PAD), lambda b, c: (rb0 + (b * nsb + g) * nc + c, 0))
    vec = lambda w: pl.BlockSpec((1, w), lambda b, c: (0, 0))
    mat = pl.BlockSpec((LORA_PAD, WC), lambda b, c: (0, 0))
    shift = pl.BlockSpec((nsb, 1, C_PAD), lambda b, c: (b, 0, 0))
    st = pl.BlockSpec((nsb, HC, NC, NC), lambda b, c: (b, 0, 0, 0))
    return pl.pallas_call(
        functools.partial(_wkv_kernel, ck=ck, nsb=nsb),
        grid=(nseg // nsb, nc),
        in_specs=[pc_spec(g) for g in range(nsb)] + [
            shift, st,
            vec(C_PAD), vec(WC), vec(WC), vec(WC), vec(WC), vec(WC),
            mat, mat, mat, vec(WC), vec(WC),
            pl.BlockSpec((nsb * ck, nsb * ck), lambda b, c: (0, 0)),
            pl.BlockSpec((LANE, LANE), lambda b, c: (0, 0)),
        ],
        out_specs=[pl.BlockSpec((nsb, ck, WC), lambda b, c: (b, c, 0)), st, shift],
        out_shape=[jax.ShapeDtypeStruct((nseg, seg_len, WC), BF16),
                   jax.ShapeDtypeStruct((nseg, HC, NC, NC), F32),
                   jax.ShapeDtypeStruct((nseg, 1, C_PAD), F32)],
        scratch_shapes=[pltpu.VMEM((nsb, HC // 2, LANE, LANE), F32), pltpu.VMEM((nsb, 1, C_PAD), F32)],
        compiler_params=_cparams(("parallel", "arbitrary")),
        name="wkv",
    )(*([pc] * nsb), prev, s0, *vecs, *mats, lng, lnb, tri, ones_bd)


def _merge_kernel(oa_ref, ob_ref, oc_ref, ga_ref, gb_ref, gc_ref, wa_ref, wb_ref, wc_ref, o_ref):
    o_ref[...] = (ga_ref[...] * _dot(oa_ref[...], wa_ref[...])
                  + gb_ref[...] * _dot(ob_ref[...], wb_ref[...])
                  + gc_ref[...] * _dot(oc_ref[...], wc_ref[...]))


def _merge(oa, ob, oc, gates, wbr):
    n = oa.shape[0]
    tm = _pick(n, (1024, 512, 256, 128, 64, 32, 16, 8))
    tn = 512
    nj = D_MODEL // tn
    bw = oa.shape[1]
    o_spec = pl.BlockSpec((tm, bw), lambda i, j: (i, 0))
    g_spec = lambda k: pl.BlockSpec((tm, tn), lambda i, j: (i, k * nj + j))
    w_spec = pl.BlockSpec((bw, tn), lambda i, j: (0, j))
    return pl.pallas_call(
        _merge_kernel,
        grid=(n // tm, nj),
        in_specs=[o_spec, o_spec, o_spec, g_spec(0), g_spec(1), g_spec(2),
                  w_spec, w_spec, w_spec],
        out_specs=pl.BlockSpec((tm, tn), lambda i, j: (i, j)),
        out_shape=jax.ShapeDtypeStruct((n, D_MODEL), F32),
        compiler_params=_cparams(("parallel", "arbitrary")),
        name="merge",
    )(oa, ob, oc, gates, gates, gates, *wbr)


def _outproj_kernel(m_ref, x_ref, w_ref, g_ref, b_ref, o_ref, ob_ref, *, alpha):
    y = alpha * x_ref[...] + _dot(m_ref[...].astype(BF16), w_ref[...])
    out = _ln_rows(y, g_ref[...], b_ref[...], LN_EPS)
    o_ref[...] = out
    ob_ref[...] = out.astype(BF16)


def _outproj(merged, x, w, g, b, alpha):
    n = x.shape[0]
    tm = _pick(n, (256, 128, 64, 32, 16, 8))
    row = pl.BlockSpec((tm, D_MODEL), lambda i: (i, 0))
    vec = pl.BlockSpec((1, D_MODEL), lambda i: (0, 0))
    return pl.pallas_call(
        functools.partial(_outproj_kernel, alpha=alpha),
        grid=(n // tm,),
        in_specs=[row, row, pl.BlockSpec((D_MODEL, D_MODEL), lambda i: (0, 0)), vec, vec],
        out_specs=[row, row],
        out_shape=[jax.ShapeDtypeStruct((n, D_MODEL), F32), jax.ShapeDtypeStruct((n, D_MODEL), BF16)],
        compiler_params=_cparams(("parallel",)),
        name="outproj",
    )(merged, x, w, g, b)


def kernel(x_prompt, x_sample, cache_k, cache_v, state_ret, state_wkv, state_shift, w_in, w_gate,
           w_branch, w_out, ffn_in, ffn_out, ln_g, ln_b, diff_lambda, subln_g, mu_c, w0, w2, a0, a2,
           g2, k_k, k_a, r_k, lnx_g, lnx_b):
    nb, t, _ = x_prompt.shape
    nbs, s_len, _ = x_sample.shape
    depth = w_in.shape[0]
    p_len = cache_k.shape[2]
    n_p = nb * t
    n_s = nbs * s_len
    n = n_p + n_s
    alpha = (2 * depth) ** 0.25

    x = jnp.concatenate([x_prompt.reshape(n_p, D_MODEL), x_sample.reshape(n_s, D_MODEL)], axis=0)
    xb = x.astype(BF16)

    pos = jnp.concatenate([jnp.tile(jnp.arange(t, dtype=jnp.int32), nb),
                           jnp.tile(p_len + jnp.arange(s_len, dtype=jnp.int32), nbs)])
    tabs_a = _rope_tables(pos, ROT_A, ROPE_THETA, DA)
    tabs_b = _rope_tables(pos, DKB, RET_THETA, DKB)

    ck_all, cv_all = cache_k, cache_v

    eye = jnp.arange(LANE) // NC
    ones_bd = (eye[:, None] == eye[None, :]).astype(BF16)
    ck_p = _pick(t, (CHUNK,))
    ck_s = s_len
    nsb_p = _pick(nb, (2, 1))
    nsb_s = _pick(nbs, (4, 2, 1))
    ret_blk = _pick(t, (256, 128, 64))

    outs = {k: [] for k in ("rp", "wp", "sp", "rs", "ws", "ss")}

    w_in_b = w_in.astype(BF16)

    qa = jnp.zeros((n, WA), BF16)
    qkb = jnp.zeros((n, 2 * HB * DKB), F32)
    k_p, v_p = (jnp.zeros((depth * n_p, HA, DVA), F32) for _ in range(2))
    k_s, v_s = (jnp.zeros((depth * n_s, HA, DVA), F32) for _ in range(2))
    oa, ob = jnp.zeros((n, WA), BF16), jnp.zeros((n, WB), BF16)

    for l in range(depth):
        lam_init = 0.8 - 0.6 * math.exp(-0.3 * l)

        def ffn_weights(i):
            return (*_cast_ffn_in(ffn_in, l, i), _cast_ffn_out(ffn_out, l, i))

        x, xb = _ffn(x, xb, *ffn_weights(0), ln_g[l, 0][None], ln_b[l, 0][None], alpha)

        w_c = _cast_wc(w_in_b, l)
        pab = _proj(xb, w_in_b, 1024, name="proj_ab", lead=(l,), m=A_COLS + B_COLS)
        pc = _proj(xb, w_c, C_PAD // 3, name="proj_c")
        w_g = _cast(w_gate, (l,), D_MODEL, N_BRANCH * D_MODEL)
        gates = _proj(xb, w_g, 1024, act="sigmoid", name="proj_gate")
        qa, k_p, v_p, qkb, kd_p, vd_p = _rope(pab, tabs_a, tabs_b, qa, k_p, v_p, qkb, 0, n_p, l * n_p)
        qa, k_s, v_s, qkb, kd_s, vd_s = _rope(pab, tabs_a, tabs_b, qa, k_s, v_s, qkb, n_p, n_s, l * n_s)

        dl = diff_lambda[l]
        sg = subln_g[l][None]
        oa = _attn_prompt(qa, kd_p, vd_p, dl, sg, oa, nb, t, lam_init)
        oa = _attn_sample(qa, kd_s, vd_s, ck_all, cv_all, dl, sg, oa, l, nbs, s_len, n_p, lam_init)

        ob, ret_p = _retention(qkb, pab, jnp.zeros((nb, HB, DKB, DVB), F32), ob, nb, t, 0, ret_blk)
        ob, ret_s = _retention(qkb, pab, state_ret[l], ob, nbs, s_len, n_p, s_len)

        padc = C_PAD - C_COLS
        vecs = (jnp.pad(mu_c[l], (0, padc))[None], w0[l][None], a0[l][None], k_k[l][None],
                k_a[l][None], r_k[l].reshape(1, WC))
        zrow = lambda r0, r1, m: jnp.pad(m.astype(BF16), ((r0, LORA_PAD - r1), (0, 0)))
        mats = (zrow(0, LORA_W, w2[l]), zrow(LORA_W, LORA_W + LORA_A, a2[l]),
                zrow(LORA_W + LORA_A, LORA_W + LORA_A + LORA_G, g2[l]))
        prev_p = jnp.zeros((nb, 1, C_PAD), F32)
        prev_s = jnp.pad(state_shift[l], ((0, 0), (0, 0), (0, padc)))
        lng, lnb = lnx_g[l][None], lnx_b[l][None]
        oc_p, wkv_p, shift_p = _wkv(pc, prev_p, jnp.zeros((nb, HC, NC, NC), F32), vecs, mats,
                                    lng, lnb, ones_bd, nb, t, 0, ck_p, nsb_p)
        oc_s, wkv_s, shift_s = _wkv(pc, prev_s, state_wkv[l], vecs, mats,
                                    lng, lnb, ones_bd, nbs, s_len, n_p, ck_s, nsb_s)
        oc = jnp.concatenate([oc_p.reshape(n_p, WC), oc_s.reshape(n_s, WC)], axis=0)

        w_br = [_cast(w_branch, (l, k), WA, D_MODEL) for k in range(N_BRANCH)]
        merged = _merge(oa, ob, oc, gates, w_br)
        x, xb = _outproj(merged, x, _cast(w_out, (l,), D_MODEL, D_MODEL), ln_g[l, 1][None],
                         ln_b[l, 1][None], alpha)

        x, xb = _ffn(x, xb, *ffn_weights(1), ln_g[l, 2][None], ln_b[l, 2][None], alpha)

        outs["rp"].append(ret_p)
        outs["rs"].append(ret_s)
        outs["wp"].append(wkv_p)
        outs["ws"].append(wkv_s)
        outs["sp"].append(shift_p[:, :, :C_COLS])
        outs["ss"].append(shift_s[:, :, :C_COLS])

    st = {k: jnp.stack(v) for k, v in outs.items()}
    y_p = x[:n_p].reshape(nb, t, D_MODEL)
    y_s = x[n_p:].reshape(nbs, s_len, D_MODEL)
    kv_p = lambda a: a.reshape(depth, nb, t, HA, DVA)
    kv_s = lambda a: a.reshape(depth, nbs, s_len, HA, DVA)
    return (y_p, y_s, kv_p(k_p), kv_p(v_p), st["rp"], st["wp"], st["sp"],
            kv_s(k_s), kv_s(v_s), st["rs"], st["ws"], st["ss"])
```

```python
import functools
import math

import jax
import jax.numpy as jnp
from jax import lax
from jax.experimental import pallas as pl
from jax.experimental.pallas import tpu as pltpu

F32 = jnp.float32
BF16 = jnp.bfloat16

D_MODEL = 2048
CHUNK = 64
HA, DA, DVA = 8, 64, 128
ROT_A = DA // 4
ROPE_THETA = 500000.0
HB, DKB, DVB = 8, 64, 128
RET_THETA = 10000.0
HC, NC = 16, 64
LORA_W, LORA_A, LORA_G = 64, 64, 160
WA, WB, WC = HA * DVA, HB * DVB, HC * NC
N_BRANCH = 3
A_COLS = 2 * HA * 2 * DA + HA * DVA
B_COLS = 2 * HB * DKB + HB * DVB + WB
C_COLS = 3 * WC + LORA_W + LORA_A + LORA_G
D_FF = 5504
LN_EPS = 1e-5
SUBLN_EPS = 1e-5
RET_EPS = 1e-6
GN_EPS_C = 64e-5

LANE = 128
D_FF_PAD = 5632
FFN_TF = 512
LORA_PAD = 384
C_PAD = 3 * WC + LORA_PAD
VMEM_LIMIT = 56 * 1024 * 1024


def _cparams(sem):
    return pltpu.CompilerParams(dimension_semantics=sem, vmem_limit_bytes=VMEM_LIMIT)


def _pick(n, cands):
    for c in cands:
        if n % c == 0:
            return c
    raise ValueError(f"no tile for {n}")


def _sigmoid(x):
    return 1.0 / (1.0 + jnp.exp(-x))


def _ln_rows(y, g, b, eps):
    mu = jnp.mean(y, axis=-1, keepdims=True)
    d = y - mu
    var = jnp.mean(d * d, axis=-1, keepdims=True)
    return d * lax.rsqrt(var + eps) * g + b


def _dot(a, b):
    return jnp.dot(a, b, preferred_element_type=F32)


def _dot_nt(a, b):
    return lax.dot_general(a, b, (((1,), (1,)), ((), ())), preferred_element_type=F32)


def _dot_tn(a, b):
    return lax.dot_general(a, b, (((0,), (0,)), ((), ())), preferred_element_type=F32)


def _split2(x):
    hi = x.astype(BF16)
    lo = (x - hi.astype(F32)).astype(BF16)
    return hi, lo


def _split3(x):
    hi = x.astype(BF16)
    r1 = x - hi.astype(F32)
    mid = r1.astype(BF16)
    lo = (r1 - mid.astype(F32)).astype(BF16)
    return hi, mid, lo


def _mm3(a, b, dot=_dot):
    ah, al = a
    bh, bl = b
    return dot(ah, bh) + (dot(ah, bl) + dot(al, bh))


def _cast_kernel(w_ref, o_ref):
    o_ref[...] = w_ref[...].astype(BF16)


def _cast(src, lead, rows, width, col_block=0):
    tr = _pick(rows, (256, 128, 64, 32, 16, 8))
    return pl.pallas_call(
        _cast_kernel,
        grid=(rows // tr,),
        in_specs=[pl.BlockSpec((None,) * len(lead) + (tr, width), lambda i: lead + (i, col_block))],
        out_specs=pl.BlockSpec((tr, width), lambda i: (i, 0)),
        out_shape=jax.ShapeDtypeStruct((rows, width), BF16),
        compiler_params=_cparams(("parallel",)),
        name="cast",
    )(src)


def _cast_ffn_in_kernel(a_ref, b_ref, oa_ref, ob_ref):
    for src, dst in ((a_ref, oa_ref), (b_ref, ob_ref)):
        dst[:, 0:D_FF] = src[...].astype(BF16)
        dst[:, D_FF:D_FF_PAD] = jnp.zeros((dst.shape[0], D_FF_PAD - D_FF), BF16)


def _cast_ffn_in(ffn_in, l, i):
    tr = 128
    half = lambda k: pl.BlockSpec((None, None, tr, D_FF), lambda r: (l, i, r, k))
    out = pl.BlockSpec((tr, D_FF_PAD), lambda r: (r, 0))
    return pl.pallas_call(
        _cast_ffn_in_kernel,
        grid=(D_MODEL // tr,),
        in_specs=[half(0), half(1)],
        out_specs=[out, out],
        out_shape=[jax.ShapeDtypeStruct((D_MODEL, D_FF_PAD), BF16)] * 2,
        compiler_params=_cparams(("parallel",)),
        name="cast_ffn_in",
    )(ffn_in, ffn_in)


def _cast_ffn_out_kernel(w_ref, o_ref):
    tr = w_ref.shape[0]
    row = lax.broadcasted_iota(jnp.int32, w_ref.shape, 0) + pl.program_id(0) * tr
    o_ref[...] = jnp.where(row < D_FF, w_ref[...], 0.0).astype(BF16)


def _cast_ffn_out(ffn_out, l, i):
    tr = FFN_TF
    return pl.pallas_call(
        _cast_ffn_out_kernel,
        grid=(D_FF_PAD // tr,),
        in_specs=[pl.BlockSpec((None, None, tr, D_MODEL), lambda r: (l, i, r, 0))],
        out_specs=pl.BlockSpec((tr, D_MODEL), lambda r: (r, 0)),
        out_shape=jax.ShapeDtypeStruct((D_FF_PAD, D_MODEL), BF16),
        compiler_params=_cparams(("parallel",)),
        name="cast_ffn_out",
    )(ffn_out)


def _cast_wc_kernel(w_ref, o_ref):
    lane = lax.broadcasted_iota(jnp.int32, w_ref.shape, 1)
    valid = lane < C_COLS - pl.program_id(1) * LORA_PAD
    o_ref[...] = jnp.where(valid, w_ref[...], 0.0).astype(BF16)


def _cast_wc(w_in, l):
    tr = 512
    cb0 = (A_COLS + B_COLS) // LORA_PAD
    return pl.pallas_call(
        _cast_wc_kernel,
        grid=(D_MODEL // tr, C_PAD // LORA_PAD),
        in_specs=[pl.BlockSpec((None, tr, LORA_PAD), lambda r, j: (l, r, cb0 + j))],
        out_specs=pl.BlockSpec((tr, LORA_PAD), lambda r, j: (r, j)),
        out_shape=jax.ShapeDtypeStruct((D_MODEL, C_PAD), BF16),
        compiler_params=_cparams(("parallel", "parallel")),
        name="cast_wc",
    )(w_in)


def _ffn_kernel(x_ref, xb_ref, wa_ref, wb_ref, wd_ref, g_ref, b_ref, o_ref, ob_ref, acc_ref, *, alpha):
    j = pl.program_id(1)

    @pl.when(j == 0)
    def _():
        acc_ref[...] = jnp.zeros_like(acc_ref)

    xb = xb_ref[...]
    ha = _dot(xb, wa_ref[...])
    hb = _dot(xb, wb_ref[...])
    act = (ha * _sigmoid(ha) * hb).astype(BF16)
    acc_ref[...] += _dot(act, wd_ref[...])

    @pl.when(j == pl.num_programs(1) - 1)
    def _():
        y = alpha * x_ref[...] + 0.5 * acc_ref[...]
        out = _ln_rows(y, g_ref[...], b_ref[...], LN_EPS)
        o_ref[...] = out
        ob_ref[...] = out.astype(BF16)


def _ffn(x, xb, wa, wb, wd, g, b, alpha):
    n = x.shape[0]
    tm = _pick(n, (512, 256, 128, 64, 32, 16, 8))
    tf = FFN_TF
    grid = (n // tm, D_FF_PAD // tf)
    row = pl.BlockSpec((tm, D_MODEL), lambda i, j: (i, 0))
    vec = pl.BlockSpec((1, D_MODEL), lambda i, j: (0, 0))
    return pl.pallas_call(
        functools.partial(_ffn_kernel, alpha=alpha),
        grid=grid,
        in_specs=[
            row, row,
            pl.BlockSpec((D_MODEL, tf), lambda i, j: (0, j)),
            pl.BlockSpec((D_MODEL, tf), lambda i, j: (0, j)),
            pl.BlockSpec((tf, D_MODEL), lambda i, j: (j, 0)),
            vec, vec,
        ],
        out_specs=[row, row],
        out_shape=[jax.ShapeDtypeStruct((n, D_MODEL), F32), jax.ShapeDtypeStruct((n, D_MODEL), BF16)],
        scratch_shapes=[pltpu.VMEM((tm, D_MODEL), F32)],
        compiler_params=_cparams(("parallel", "arbitrary")),
        name="ffn",
    )(x, xb, wa, wb, wd, g, b)


def _proj_kernel(x_ref, w_ref, o_ref):
    o_ref[...] = _dot(x_ref[...], w_ref[...])


def _proj(xb, w, tn, name, lead=(), m=None):
    n = xb.shape[0]
    m = w.shape[-1] if m is None else m
    tm = _pick(n, (1024, 512, 256, 128, 64, 32, 16, 8))
    return pl.pallas_call(
        _proj_kernel,
        grid=(n // tm, m // tn),
        in_specs=[
            pl.BlockSpec((tm, D_MODEL), lambda i, j: (i, 0)),
            pl.BlockSpec((None,) * len(lead) + (D_MODEL, tn), lambda i, j: lead + (0, j)),
        ],
        out_specs=pl.BlockSpec((tm, tn), lambda i, j: (i, j)),
        out_shape=jax.ShapeDtypeStruct((n, m), F32),
        compiler_params=_cparams(("parallel", "arbitrary")),
        name=name,
    )(xb, w)


def _rope_tables(pos, rot, theta, width):
    half = rot // 2
    freq = jnp.power(theta, -jnp.arange(half, dtype=F32) / half)
    ang = pos.astype(F32)[:, None] * freq[None, :]
    cos, sin = jnp.cos(ang), jnp.sin(ang)
    n = pos.shape[0]
    pad = width - rot
    c = jnp.concatenate([cos, cos, jnp.ones((n, pad), F32)], axis=-1)
    s_lo = jnp.concatenate([-sin, jnp.zeros((n, half + pad), F32)], axis=-1)
    s_hi = jnp.concatenate([jnp.zeros((n, half), F32), sin, jnp.zeros((n, pad), F32)], axis=-1)
    rep = LANE // width
    return tuple(jnp.tile(t, (1, rep)) for t in (c, s_lo, s_hi))


def _rope_kernel(xa_ref, xv_ref, xb_ref, ca_ref, la_ref, ha_ref, cb_ref, lb_ref, hb_ref,
                 qa_buf, kall_buf, vall_buf, qkb_buf, qa_ref, kall_ref, vall_ref, qkb_ref, kd_ref, vd_ref):
    del qa_buf, kall_buf, vall_buf, qkb_buf

    def rot(x, c, lo, hi, half):
        return x * c + pltpu.roll(x, LANE - half, 1) * lo + pltpu.roll(x, half, 1) * hi

    ca, la, ha = ca_ref[...], la_ref[...], ha_ref[...]
    nqa = HA * 2 * DA // LANE
    tm = xa_ref.shape[0]
    ks = []
    for t in range(2 * nqa):
        y = rot(xa_ref[:, t * LANE:(t + 1) * LANE], ca, la, ha, ROT_A // 2)
        if t < nqa:
            qa_ref[:, t * LANE:(t + 1) * LANE] = (y * (DA ** -0.5)).astype(BF16)
        else:
            ks.append(y)
            kd_ref[:, (t - nqa) * LANE:(t - nqa + 1) * LANE] = y.astype(BF16)
    kall_ref[...] = jnp.concatenate(ks, axis=1).reshape(tm, HA, 2 * DA)
    v = xv_ref[...]
    vall_ref[...] = v.reshape(tm, HA, DVA)
    vd_ref[...] = v.astype(BF16)
    cb, lb, hb = cb_ref[...], lb_ref[...], hb_ref[...]
    nqb = HB * DKB // LANE
    for t in range(2 * nqb):
        sl = slice(t * LANE, (t + 1) * LANE)
        y = rot(xb_ref[:, sl], cb, lb, hb, DKB // 2)
        if t >= nqb:
            y = y * (DKB ** -0.5)
        qkb_ref[:, sl] = y


def _rope(pab, tabs_a, tabs_b, qa_buf, kall_buf, vall_buf, qkb_buf, row0, nrows, kv_row0):
    tm = _pick(nrows, (256, 128, 64, 32, 16, 8))
    rb0, kb0 = row0 // tm, kv_row0 // tm
    wqk = 2 * HA * 2 * DA
    wb = 2 * HB * DKB
    tab = pl.BlockSpec((tm, LANE), lambda i: (rb0 + i, 0))
    anyspec = pl.BlockSpec(memory_space=pl.ANY)
    bufs = (qa_buf, kall_buf, vall_buf, qkb_buf)
    cache = pl.BlockSpec((tm, HA, DVA), lambda i: (kb0 + i, 0, 0))
    dense = pl.BlockSpec((tm, WA), lambda i: (i, 0))
    return pl.pallas_call(
        _rope_kernel,
        grid=(nrows // tm,),
        in_specs=[
            pl.BlockSpec((tm, wqk), lambda i: (rb0 + i, 0)),
            pl.BlockSpec((tm, WA), lambda i: (rb0 + i, wqk // WA)),
            pl.BlockSpec((tm, wb), lambda i: (rb0 + i, A_COLS // wb)),
            tab, tab, tab, tab, tab, tab,
            anyspec, anyspec, anyspec, anyspec,
        ],
        out_specs=[
            pl.BlockSpec((tm, WA), lambda i: (rb0 + i, 0)),
            cache, cache,
            pl.BlockSpec((tm, wb), lambda i: (rb0 + i, 0)),
            dense, dense,
        ],
        out_shape=[jax.ShapeDtypeStruct(b.shape, b.dtype) for b in bufs]
                  + [jax.ShapeDtypeStruct((nrows, WA), BF16)] * 2,
        input_output_aliases={9: 0, 10: 1, 11: 2, 12: 3},
        compiler_params=_cparams(("parallel",)),
        name="rope",
    )(pab, pab, pab, *tabs_a, *tabs_b, *bufs)


def _diff_lambda(dl_ref, lam_init):
    lv = dl_ref[...]
    s1 = jnp.sum(lv[0:1] * lv[1:2], axis=-1, keepdims=True)
    s2 = jnp.sum(lv[2:3] * lv[3:4], axis=-1, keepdims=True)
    return jnp.exp(s1) - jnp.exp(s2) + lam_init


def _split_maps(q):
    lane = lax.broadcasted_iota(jnp.int32, q.shape, 1)
    zero = jnp.zeros_like(q)
    return jnp.where(lane < DA, q, zero), jnp.where(lane >= DA, q, zero)


def _subln(o, g, scale):
    ms = jnp.mean(o * o, axis=-1, keepdims=True)
    return o * lax.rsqrt(ms + SUBLN_EPS) * g * scale


def _attn_prompt_kernel(q_ref, kb_ref, v_ref, dl_ref, g_ref, o_buf, o_ref, vb_ref, s_ref,
                        m_ref, acc_ref, *, tq, tk, lam_init):
    del o_buf
    qi = pl.program_id(2)

    @pl.when(qi == 0)
    def _():
        vb_ref[:, 0:DVA] = v_ref[...]
        vb_ref[:, DVA:2 * DVA] = jnp.ones(v_ref.shape, BF16)

    lam = _diff_lambda(dl_ref, lam_init)
    qs = _split_maps(q_ref[...])

    def scores(kb, slot):
        k = kb_ref[pl.ds(pl.multiple_of(kb * tk, tk), tk), :]
        for i, qm in enumerate(qs):
            s_ref[slot, i] = _dot_nt(qm, k)

    def consume(ss, start, width):
        v1 = vb_ref[pl.ds(start, width), :]
        stats = []
        for i, s in enumerate(ss):
            m = m_ref[i]
            m_new = jnp.maximum(m, jnp.max(s, axis=-1, keepdims=True))
            m_ref[i] = m_new
            stats.append((jnp.exp(m - m_new), jnp.exp(s - m_new).astype(BF16)))
        pvs = [_dot(p, v1) for _, p in stats]
        for i, ((a, _), pv) in enumerate(zip(stats, pvs)):
            acc_ref[i] = a * acc_ref[i] + pv

    def full_step(kb, slot):
        scores(kb + 1, 1 - slot)
        consume((s_ref[slot, 0], s_ref[slot, 1]), pl.multiple_of(kb * tk, tk), tk)

    q0 = qi * tq
    n_full = q0 // tk
    m_ref[...] = jnp.full(m_ref.shape, -1e30, F32)
    acc_ref[...] = jnp.zeros_like(acc_ref)
    scores(0, 0)

    def two_steps(j, _):
        full_step(2 * j, 0)
        full_step(2 * j + 1, 1)
        return 0

    lax.fori_loop(0, n_full // 2, two_steps, 0)

    @pl.when(n_full % 2 == 1)
    def _():
        full_step(n_full - 1, 0)

    tail = pl.multiple_of(n_full * tk, tk)
    g = g_ref[...]

    for r in range(tk // tq):
        @pl.when(q0 - tail == r * tq)
        def _():
            width = (r + 1) * tq
            row = lax.broadcasted_iota(jnp.int32, (tq, width), 0) // CHUNK
            col = lax.broadcasted_iota(jnp.int32, (tq, width), 1) // CHUNK
            ok = col - row <= r * (tq // CHUNK)
            consume(tuple(jnp.where(ok, s_ref[n_full % 2, i, :, 0:width], -1e30) for i in range(2)),
                    tail, width)
            a1, a2 = acc_ref[0], acc_ref[1]
            o = a1[:, 0:DVA] / a1[:, DVA:2 * DVA] - lam * (a2[:, 0:DVA] / a2[:, DVA:2 * DVA])
            o_ref[...] = _subln(o, g, 1.0 - lam_init).astype(BF16)


def _attn_prompt(qa, kd, vd, dl, g, o_buf, nb, t, lam_init):
    tq = _pick(t, (512, 256, 128, 64))
    tk = _pick(t, (1024, 512, 256, 128, 64))
    nq = t // tq
    kv = pl.BlockSpec((t, 2 * DA), lambda b, h, i: (b, h))
    return pl.pallas_call(
        functools.partial(_attn_prompt_kernel, tq=tq, tk=tk, lam_init=lam_init),
        grid=(nb, HA, nq),
        in_specs=[
            pl.BlockSpec((tq, 2 * DA), lambda b, h, i: (b * nq + i, h)),
            kv, kv,
            pl.BlockSpec((4, DA), lambda b, h, i: (0, 0)),
            pl.BlockSpec((1, DVA), lambda b, h, i: (0, 0)),
            pl.BlockSpec(memory_space=pl.ANY),
        ],
        out_specs=pl.BlockSpec((tq, DVA), lambda b, h, i: (b * nq + i, h)),
        out_shape=jax.ShapeDtypeStruct(o_buf.shape, o_buf.dtype),
        input_output_aliases={5: 0},
        scratch_shapes=[pltpu.VMEM((t, 2 * DVA), BF16),
                        pltpu.VMEM((2, 2, tq, tk), F32),
                        pltpu.VMEM((2, tq, 1), F32), pltpu.VMEM((2, tq, 2 * DVA), F32)],
        compiler_params=_cparams(("parallel", "parallel", "arbitrary")),
        name="attn_prompt",
    )(qa, kd, vd, dl, g, o_buf)


def _attn_sample_kernel(q_ref, kn_ref, vn_ref, ck_ref, cv_ref, dl_ref, g_ref, o_buf, o_ref, *,
                        s_len, lam_init):
    del o_buf
    lam = _diff_lambda(dl_ref, lam_init)
    g = g_ref[...]
    p_len = ck_ref.shape[0]
    kc_all = ck_ref[...].reshape(p_len, WA).astype(BF16)
    vc_all = cv_ref[...].reshape(p_len, WA).astype(BF16)
    heads = range(HA)
    sls = [slice(h * 2 * DA, (h + 1) * 2 * DA) for h in heads]
    qq = [jnp.concatenate(_split_maps(q_ref[:, sl]), axis=0) for sl in sls]
    s_c = [_dot_nt(qq[h], kc_all[:, sls[h]]) for h in heads]
    s_n = [_dot_nt(qq[h], kn_ref[:, sls[h]]) for h in heads]
    p_c, p_n, den = [], [], []
    for h in heads:
        m = jnp.maximum(jnp.max(s_c[h], axis=-1, keepdims=True), jnp.max(s_n[h], axis=-1, keepdims=True))
        e_c = jnp.exp(s_c[h] - m)
        e_n = jnp.exp(s_n[h] - m)
        den.append(jnp.sum(e_c, axis=-1, keepdims=True) + jnp.sum(e_n, axis=-1, keepdims=True))
        p_c.append(e_c.astype(BF16))
        p_n.append(e_n.astype(BF16))
    pv = [_dot(p_c[h], vc_all[:, sls[h]]) + _dot(p_n[h], vn_ref[:, sls[h]]) for h in heads]
    for h in heads:
        o = pv[h] / den[h]
        od = o[:s_len] - lam * o[s_len:]
        o_ref[:, sls[h]] = _subln(od, g, 1.0 - lam_init).astype(BF16)


def _attn_sample(qa, kd, vd, ck, cv, dl, g, o_buf, layer, nbs, s_len, row0, lam_init):
    p_len = ck.shape[2]
    rb0 = row0 // s_len
    new = pl.BlockSpec((s_len, WA), lambda b: (b, 0))
    cache = pl.BlockSpec((None, None, p_len, HA, DVA), lambda b: (layer, b, 0, 0, 0))
    return pl.pallas_call(
        functools.partial(_attn_sample_kernel, s_len=s_len, lam_init=lam_init),
        grid=(nbs,),
        in_specs=[
            pl.BlockSpec((s_len, WA), lambda b: (rb0 + b, 0)),
            new, new, cache, cache,
            pl.BlockSpec((4, DA), lambda b: (0, 0)),
            pl.BlockSpec((1, DVA), lambda b: (0, 0)),
            pl.BlockSpec(memory_space=pl.ANY),
        ],
        out_specs=pl.BlockSpec((s_len, WA), lambda b: (rb0 + b, 0)),
        out_shape=jax.ShapeDtypeStruct(o_buf.shape, o_buf.dtype),
        input_output_aliases={7: 0},
        compiler_params=_cparams(("parallel",)),
        name="attn_sample",
    )(qa, kd, vd, ck, cv, dl, g, o_buf)


def _ret_kernel(q_ref, k_ref, v_ref, gate_ref, r0_ref, o_buf, o_ref, rout_ref, st_ref, *, blk):
    del o_buf
    c = pl.program_id(1)

    @pl.when(c == 0)
    def _():
        st_ref[...] = jnp.zeros_like(st_ref)
        for h in range(HB):
            e = h % 2
            st_ref[h, e * DKB:(e + 1) * DKB, :] = r0_ref[h]

    n_i = lax.broadcasted_iota(jnp.int32, (blk, blk), 0)
    m_i = lax.broadcasted_iota(jnp.int32, (blk, blk), 1)
    diff = (n_i - m_i).astype(F32)
    causal = n_i >= m_i
    n_col = lax.broadcasted_iota(jnp.int32, (blk, LANE), 0).astype(F32)
    lane = lax.broadcasted_iota(jnp.int32, (blk, LANE), 1)
    heads = range(HB)
    log_g = [math.log1p(-2.0 ** (-5.0 - h)) for h in heads]
    vsl = [slice(h * DVB, (h + 1) * DVB) for h in heads]
    qb, kb, kd, v = [], [], [], []
    for h in heads:
        p, e = h // 2, h % 2
        sl = slice(p * LANE, (p + 1) * LANE)
        mine = (lane >= DKB) if e else (lane < DKB)
        k = jnp.where(mine, k_ref[:, sl], 0.0)
        qb.append(jnp.where(mine, q_ref[:, sl], 0.0).astype(BF16))
        kb.append(k.astype(BF16))
        kd.append((k * jnp.exp((blk - 1.0 - n_col) * log_g[h])).astype(BF16))
        v.append(v_ref[:, vsl[h]].astype(BF16))
    r_old = [st_ref[h] for h in heads]
    s = [_dot_nt(qb[h], kb[h]) for h in heads]
    cross = [_dot(qb[h], r_old[h].astype(BF16)) for h in heads]
    grow = [_dot_tn(kd[h], v[h]) for h in heads]
    sd = [(s[h] * jnp.where(causal, jnp.exp(jnp.maximum(diff, 0.0) * log_g[h]), 0.0)).astype(BF16)
          for h in heads]
    intra = [_dot(sd[h], v[h]) for h in heads]
    for h in heads:
        st_ref[h] = math.exp(blk * log_g[h]) * r_old[h] + grow[h]
        o = intra[h] + cross[h] * jnp.exp((n_col + 1.0) * log_g[h])
        ms = jnp.mean(o * o, axis=-1, keepdims=True)
        gt = gate_ref[:, vsl[h]]
        o_ref[:, vsl[h]] = (o * lax.rsqrt(ms + RET_EPS) * (gt * _sigmoid(gt))).astype(BF16)

    @pl.when(c == pl.num_programs(1) - 1)
    def _():
        for h in range(HB):
            e = h % 2
            rout_ref[h] = st_ref[h, e * DKB:(e + 1) * DKB, :]


def _retention(qkb, pab, r0, o_buf, nseg, seg_len, row0, blk):
    nc = seg_len // blk
    rb0 = row0 // blk
    vcol = (A_COLS + 2 * HB * DKB) // WB
    row = lambda b, c: rb0 + b * nc + c
    return pl.pallas_call(
        functools.partial(_ret_kernel, blk=blk),
        grid=(nseg, nc),
        in_specs=[
            pl.BlockSpec((blk, HB * DKB), lambda b, c: (row(b, c), 0)),
            pl.BlockSpec((blk, HB * DKB), lambda b, c: (row(b, c), 1)),
            pl.BlockSpec((blk, WB), lambda b, c: (row(b, c), vcol)),
            pl.BlockSpec((blk, WB), lambda b, c: (row(b, c), vcol + 1)),
            pl.BlockSpec((None, HB, DKB, DVB), lambda b, c: (b, 0, 0, 0)),
            pl.BlockSpec(memory_space=pl.ANY),
        ],
        out_specs=[
            pl.BlockSpec((blk, WB), lambda b, c: (row(b, c), 0)),
            pl.BlockSpec((None, HB, DKB, DVB), lambda b, c: (b, 0, 0, 0)),
        ],
        out_shape=[jax.ShapeDtypeStruct(o_buf.shape, o_buf.dtype),
                   jax.ShapeDtypeStruct((nseg, HB, DKB, DVB), F32)],
        input_output_aliases={5: 0},
        scratch_shapes=[pltpu.VMEM((HB, 2 * DKB, DVB), F32)],
        compiler_params=_cparams(("parallel", "arbitrary")),
        name="retention",
    )(qkb, qkb, pab, pab, r0, o_buf)


def _head_sum(x, ones_bd):
    hi, lo = _split2(x)
    return _dot(hi, ones_bd) + _dot(lo, ones_bd)


def _wkv_kernel(*refs, ck, nsb):
    pc_refs = refs[:nsb]
    (prev_ref, s0_ref, mu_ref, w0_ref, a0_ref, kk_ref, ka_ref, rk_ref, w2_ref, a2_ref, g2_ref,
     lng_ref, lnb_ref, tri_ref, ones_ref, o_ref, sout_ref, last_ref, st_ref, carry_ref) = refs[nsb:]
    c = pl.program_id(1)

    @pl.when(c == 0)
    def _():
        zero = jnp.zeros((NC, NC), F32)
        for g in range(nsb):
            for p in range(HC // 2):
                top = jnp.concatenate([s0_ref[g, 2 * p], zero], axis=1)
                bot = jnp.concatenate([zero, s0_ref[g, 2 * p + 1]], axis=1)
                st_ref[g, p] = jnp.concatenate([top, bot], axis=0)
        carry_ref[...] = prev_ref[...]

    row = lax.broadcasted_iota(jnp.int32, (ck, C_PAD), 0)
    pcs, shifted = [], []
    for g, pc_ref in enumerate(pc_refs):
        pc = pc_ref[...]
        pcs.append(pc)
        shifted.append(jnp.where(row == 0, carry_ref[g], pltpu.roll(pc, 1, 0)))
        carry_ref[g] = pc[ck - 1:ck, :]
        last_ref[g] = pc[ck - 1:ck, :]
    pc_all = jnp.concatenate(pcs, axis=0)
    pcm = pc_all + mu_ref[...] * (jnp.concatenate(shifted, axis=0) - pc_all)
    rc = pcm[:, 0:WC]
    kc = pcm[:, WC:2 * WC]
    vc = pcm[:, 2 * WC:3 * WC]
    lora = pcm[:, 3 * WC:C_PAD]
    z = w0_ref[...] + _dot(jnp.tanh(lora).astype(BF16), w2_ref[...])
    u = -z
    softplus = jnp.maximum(u, 0.0) + jnp.log1p(jnp.exp(-jnp.abs(u)))
    lw = -jnp.exp(-softplus - 0.5)
    a_sig = _sigmoid(a0_ref[...] + _dot(lora.astype(BF16), a2_ref[...]))
    gate = _dot(_sigmoid(lora).astype(BF16), g2_ref[...])
    kk = kc * kk_ref[...]
    kc2 = kc * (1.0 + (a_sig - 1.0) * ka_ref[...])
    rkr = rc * kc2 * rk_ref[...]
    ones_bd = ones_ref[...]

    tri = tri_ref[...]
    l1, l2, l3 = _split3(lw)
    cum = _dot(tri, l1) + (_dot(tri, l2) + _dot(tri, l3))
    total = jnp.concatenate(
        [jnp.broadcast_to(cum[(g + 1) * ck - 1:(g + 1) * ck, :], (ck, WC)) for g in range(nsb)], axis=0)
    e_neg = jnp.exp(-cum)
    e_rem = jnp.exp(total - cum)
    e_exc = jnp.exp(cum - lw)
    r_t = rc * jnp.exp(cum)
    k_t = kc2 * e_neg
    k_p = kc2 * e_rem
    p_tot = jnp.exp(total)

    lane = lax.broadcasted_iota(jnp.int32, (ck, LANE), 1)
    m0 = lane < NC

    def stack(x):
        return jnp.concatenate([jnp.where(m0, x, 0.0), jnp.where(m0, 0.0, x)], axis=0)

    n2 = 2 * ck
    ri = lax.broadcasted_iota(jnp.int32, (n2, n2), 0)
    ci = lax.broadcasted_iota(jnp.int32, (n2, n2), 1)
    same = (ri // ck) == (ci // ck)
    strict = jnp.logical_and(same, ri > ci)
    incl = jnp.logical_and(same, ri >= ci)

    bf = lambda t: t.astype(BF16)
    pairs = range(HC // 2)
    chains = [(g, p) for g in range(nsb) for p in pairs]
    a_s, r_s, b_s, k_s, v_s, bp_s, kp_s, bonus = {}, {}, {}, {}, {}, {}, {}, {}
    for p in pairs:
        sl = slice(p * LANE, (p + 1) * LANE)
        kkt = kk[:, sl]
        kn = kkt / jnp.maximum(jnp.sqrt(_head_sum(kkt * kkt, ones_bd)), 1e-12)
        bn = kn * a_sig[:, sl]
        a_tp = -kn * e_exc[:, sl]
        b_tp = bn * e_neg[:, sl]
        b_pp = bn * e_rem[:, sl]
        bonus_p = _head_sum(rkr[:, sl], ones_bd) * vc[:, sl]
        for g in range(nsb):
            rows = slice(g * ck, (g + 1) * ck)
            a_s[g, p] = bf(stack(a_tp[rows]))
            r_s[g, p] = bf(stack(r_t[rows, sl]))
            b_s[g, p] = bf(stack(b_tp[rows]))
            k_s[g, p] = bf(stack(k_t[rows, sl]))
            v_s[g, p] = _split2(stack(vc[rows, sl]))
            bp_s[g, p] = _split2(stack(b_pp[rows]))
            kp_s[g, p] = _split2(stack(k_p[rows, sl]))
            bonus[g, p] = bonus_p[rows]

    s_old = {gp: st_ref[gp[0], gp[1]] for gp in chains}
    s_b = {gp: bf(s_old[gp]) for gp in chains}
    if n2 % LANE == 0:
        big = {gp: _dot_nt(jnp.concatenate([a_s[gp], r_s[gp]], axis=0),
                           jnp.concatenate([b_s[gp], k_s[gp]], axis=0)) for gp in chains}
        ab = {gp: big[gp][0:n2, 0:n2] for gp in chains}
        ak = {gp: big[gp][0:n2, n2:2 * n2] for gp in chains}
        rb = {gp: big[gp][n2:2 * n2, 0:n2] for gp in chains}
        rk = {gp: big[gp][n2:2 * n2, n2:2 * n2] for gp in chains}
    else:
        ab = {gp: _dot_nt(a_s[gp], b_s[gp]) for gp in chains}
        ak = {gp: _dot_nt(a_s[gp], k_s[gp]) for gp in chains}
        rb = {gp: _dot_nt(r_s[gp], b_s[gp]) for gp in chains}
        rk = {gp: _dot_nt(r_s[gp], k_s[gp]) for gp in chains}
    lk = {gp: bf(jnp.where(strict, ab[gp], 0.0)) for gp in chains}
    l_ak = {gp: bf(jnp.where(strict, ak[gp], 0.0)) for gp in chains}
    l_rb = {gp: bf(jnp.where(incl, rb[gp], 0.0)) for gp in chains}
    l_rk = {gp: bf(jnp.where(incl, rk[gp], 0.0)) for gp in chains}
    x = {gp: _dot_nt(a_s[gp], s_b[gp]) + _dot(l_ak[gp], v_s[gp][0]) for gp in chains}
    y_part = {gp: _dot_nt(r_s[gp], s_b[gp]) + _dot(l_rk[gp], v_s[gp][0]) for gp in chains}
    span = 1
    while span < ck:
        x = {gp: x[gp] + _dot(lk[gp], bf(x[gp])) for gp in chains}
        span *= 2
        if span < ck:
            lk = {gp: bf(_dot(lk[gp], lk[gp])) for gp in chains}
    x_sp = {gp: _split2(x[gp]) for gp in chains}
    y_st = {gp: y_part[gp] + _dot(l_rb[gp], x_sp[gp][0]) for gp in chains}
    for g, p in chains:
        sl = slice(p * LANE, (p + 1) * LANE)
        decay = p_tot[(g + 1) * ck - 1:(g + 1) * ck, sl]
        uv = tuple(jnp.concatenate([xs, vs], axis=0) for xs, vs in zip(x_sp[g, p], v_s[g, p]))
        bk = tuple(jnp.concatenate([bs, ks], axis=0) for bs, ks in zip(bp_s[g, p], kp_s[g, p]))
        st_ref[g, p] = s_old[g, p] * decay + _mm3(uv, bk, _dot_tn)
    for g, p in chains:
        sl = slice(p * LANE, (p + 1) * LANE)
        rows = slice(g * ck, (g + 1) * ck)
        y = y_st[g, p][0:ck] + y_st[g, p][ck:n2]
        mean = _head_sum(y, ones_bd) * (1.0 / NC)
        d = y - mean
        var = _head_sum(d * d, ones_bd) * (1.0 / NC)
        yn = d * lax.rsqrt(var + GN_EPS_C) * lng_ref[:, sl] + lnb_ref[:, sl]
        o_ref[g, :, sl] = ((yn + bonus[g, p]) * gate[rows, sl]).astype(BF16)

    @pl.when(c == pl.num_programs(1) - 1)
    def _():
        for g in range(nsb):
            for p in range(HC // 2):
                s_pair = st_ref[g, p]
                sout_ref[g, 2 * p] = s_pair[0:NC, 0:NC]
                sout_ref[g, 2 * p + 1] = s_pair[NC:2 * NC, NC:2 * NC]


def _wkv(pc, prev, s0, vecs, mats, lng, lnb, ones_bd, nseg, seg_len, row0, ck, nsb):
    nc = seg_len // ck
    rb0 = row0 // ck
    pos = jnp.arange(nsb * ck)
    tri = ((pos[:, None] // ck == pos[None, :] // ck) & (pos[:, None] >= pos[None, :])).astype(BF16)
    pc_spec = lambda g: pl.BlockSpec((ck, C_PAD), lambda b, c: (rb0 + (b * nsb + g) * nc + c, 0))
    vec = lambda w: pl.BlockSpec((1, w), lambda b, c: (0, 0))
    mat = pl.BlockSpec((LORA_PAD, WC), lambda b, c: (0, 0))
    shift = pl.BlockSpec((nsb, 1, C_PAD), lambda b, c: (b, 0, 0))
    st = pl.BlockSpec((nsb, HC, NC, NC), lambda b, c: (b, 0, 0, 0))
    return pl.pallas_call(
        functools.partial(_wkv_kernel, ck=ck, nsb=nsb),
        grid=(nseg // nsb, nc),
        in_specs=[pc_spec(g) for g in range(nsb)] + [
            shift, st,
            vec(C_PAD), vec(WC), vec(WC), vec(WC), vec(WC), vec(WC),
            mat, mat, mat, vec(WC), vec(WC),
            pl.BlockSpec((nsb * ck, nsb * ck), lambda b, c: (0, 0)),
            pl.BlockSpec((LANE, LANE), lambda b, c: (0, 0)),
        ],
        out_specs=[pl.BlockSpec((nsb, ck, WC), lambda b, c: (b, c, 0)), st, shift],
        out_shape=[jax.ShapeDtypeStruct((nseg, seg_len, WC), BF16),
                   jax.ShapeDtypeStruct((nseg, HC, NC, NC), F32),
                   jax.ShapeDtypeStruct((nseg, 1, C_PAD), F32)],
        scratch_shapes=[pltpu.VMEM((nsb, HC // 2, LANE, LANE), F32), pltpu.VMEM((nsb, 1, C_PAD), F32)],
        compiler_params=_cparams(("parallel", "arbitrary")),
        name="wkv",
    )(*([pc] * nsb), prev, s0, *vecs, *mats, lng, lnb, tri, ones_bd)


def _merge_kernel(xb_ref, oa_ref, ob_ref, oc_ref, ga_ref, gb_ref, gc_ref, wa_ref, wb_ref, wc_ref, o_ref):
    xb = xb_ref[...]
    zs = [_dot(xb, g_ref[...]) for g_ref in (ga_ref, gb_ref, gc_ref)]
    ys = [_dot(b_ref[...], w_ref[...])
          for b_ref, w_ref in ((oa_ref, wa_ref), (ob_ref, wb_ref), (oc_ref, wc_ref))]
    o_ref[...] = (_sigmoid(zs[0]) * ys[0] + _sigmoid(zs[1]) * ys[1]) + _sigmoid(zs[2]) * ys[2]


def _merge(xb, oa, ob, oc, w_gate, wbr):
    n = oa.shape[0]
    tm = _pick(n, (1024, 512, 256, 128, 64, 32, 16, 8))
    tn = 512
    nj = D_MODEL // tn
    bw = oa.shape[1]
    o_spec = pl.BlockSpec((tm, bw), lambda i, j: (i, 0))
    g_spec = lambda k: pl.BlockSpec((D_MODEL, tn), lambda i, j: (0, k * nj + j))
    w_spec = pl.BlockSpec((bw, tn), lambda i, j: (0, j))
    return pl.pallas_call(
        _merge_kernel,
        grid=(n // tm, nj),
        in_specs=[pl.BlockSpec((tm, D_MODEL), lambda i, j: (i, 0)), o_spec, o_spec, o_spec,
                  g_spec(0), g_spec(1), g_spec(2), w_spec, w_spec, w_spec],
        out_specs=pl.BlockSpec((tm, tn), lambda i, j: (i, j)),
        out_shape=jax.ShapeDtypeStruct((n, D_MODEL), F32),
        compiler_params=_cparams(("parallel", "arbitrary")),
        name="merge",
    )(xb, oa, ob, oc, w_gate, w_gate, w_gate, *wbr)


def _outproj_kernel(m_ref, x_ref, w_ref, g_ref, b_ref, o_ref, ob_ref, *, alpha):
    y = alpha * x_ref[...] + _dot(m_ref[...].astype(BF16), w_ref[...])
    out = _ln_rows(y, g_ref[...], b_ref[...], LN_EPS)
    o_ref[...] = out
    ob_ref[...] = out.astype(BF16)


def _outproj(merged, x, w, g, b, alpha):
    n = x.shape[0]
    tm = _pick(n, (256, 128, 64, 32, 16, 8))
    row = pl.BlockSpec((tm, D_MODEL), lambda i: (i, 0))
    vec = pl.BlockSpec((1, D_MODEL), lambda i: (0, 0))
    return pl.pallas_call(
        functools.partial(_outproj_kernel, alpha=alpha),
        grid=(n // tm,),
        in_specs=[row, row, pl.BlockSpec((D_MODEL, D_MODEL), lambda i: (0, 0)), vec, vec],
        out_specs=[row, row],
        out_shape=[jax.ShapeDtypeStruct((n, D_MODEL), F32), jax.ShapeDtypeStruct((n, D_MODEL), BF16)],
        compiler_params=_cparams(("parallel",)),
        name="outproj",
    )(merged, x, w, g, b)


def kernel(x_prompt, x_sample, cache_k, cache_v, state_ret, state_wkv, state_shift, w_in, w_gate,
           w_branch, w_out, ffn_in, ffn_out, ln_g, ln_b, diff_lambda, subln_g, mu_c, w0, w2, a0, a2,
           g2, k_k, k_a, r_k, lnx_g, lnx_b):
    nb, t, _ = x_prompt.shape
    nbs, s_len, _ = x_sample.shape
    depth = w_in.shape[0]
    p_len = cache_k.shape[2]
    n_p = nb * t
    n_s = nbs * s_len
    n = n_p + n_s
    alpha = (2 * depth) ** 0.25

    x = jnp.concatenate([x_prompt.reshape(n_p, D_MODEL), x_sample.reshape(n_s, D_MODEL)], axis=0)
    xb = x.astype(BF16)

    pos = jnp.concatenate([jnp.tile(jnp.arange(t, dtype=jnp.int32), nb),
                           jnp.tile(p_len + jnp.arange(s_len, dtype=jnp.int32), nbs)])
    tabs_a = _rope_tables(pos, ROT_A, ROPE_THETA, DA)
    tabs_b = _rope_tables(pos, DKB, RET_THETA, DKB)

    ck_all, cv_all = cache_k, cache_v

    eye = jnp.arange(LANE) // NC
    ones_bd = (eye[:, None] == eye[None, :]).astype(BF16)
    ck_p = _pick(t, (CHUNK,))
    ck_s = s_len
    nsb_p = _pick(nb, (2, 1))
    nsb_s = _pick(nbs, (4, 2, 1))
    ret_blk = _pick(t, (256, 128, 64))

    outs = {k: [] for k in ("rp", "wp", "sp", "rs", "ws", "ss")}

    w_in_b = w_in.astype(BF16)

    qa = jnp.zeros((n, WA), BF16)
    qkb = jnp.zeros((n, 2 * HB * DKB), F32)
    k_p, v_p = (jnp.zeros((depth * n_p, HA, DVA), F32) for _ in range(2))
    k_s, v_s = (jnp.zeros((depth * n_s, HA, DVA), F32) for _ in range(2))
    oa, ob = jnp.zeros((n, WA), BF16), jnp.zeros((n, WB), BF16)

    for l in range(depth):
        lam_init = 0.8 - 0.6 * math.exp(-0.3 * l)

        def ffn_weights(i):
            return (*_cast_ffn_in(ffn_in, l, i), _cast_ffn_out(ffn_out, l, i))

        x, xb = _ffn(x, xb, *ffn_weights(0), ln_g[l, 0][None], ln_b[l, 0][None], alpha)

        w_c = _cast_wc(w_in_b, l)
        pab = _proj(xb, w_in_b, 1024, name="proj_ab", lead=(l,), m=A_COLS + B_COLS)
        pc = _proj(xb, w_c, C_PAD // 3, name="proj_c")
        w_g = _cast(w_gate, (l,), D_MODEL, N_BRANCH * D_MODEL)
        qa, k_p, v_p, qkb, kd_p, vd_p = _rope(pab, tabs_a, tabs_b, qa, k_p, v_p, qkb, 0, n_p, l * n_p)
        qa, k_s, v_s, qkb, kd_s, vd_s = _rope(pab, tabs_a, tabs_b, qa, k_s, v_s, qkb, n_p, n_s, l * n_s)

        dl = diff_lambda[l]
        sg = subln_g[l][None]
        oa = _attn_prompt(qa, kd_p, vd_p, dl, sg, oa, nb, t, lam_init)
        oa = _attn_sample(qa, kd_s, vd_s, ck_all, cv_all, dl, sg, oa, l, nbs, s_len, n_p, lam_init)

        ob, ret_p = _retention(qkb, pab, jnp.zeros((nb, HB, DKB, DVB), F32), ob, nb, t, 0, ret_blk)
        ob, ret_s = _retention(qkb, pab, state_ret[l], ob, nbs, s_len, n_p, s_len)

        padc = C_PAD - C_COLS
        vecs = (jnp.pad(mu_c[l], (0, padc))[None], w0[l][None], a0[l][None], k_k[l][None],
                k_a[l][None], r_k[l].reshape(1, WC))
        zrow = lambda r0, r1, m: jnp.pad(m.astype(BF16), ((r0, LORA_PAD - r1), (0, 0)))
        mats = (zrow(0, LORA_W, w2[l]), zrow(LORA_W, LORA_W + LORA_A, a2[l]),
                zrow(LORA_W + LORA_A, LORA_W + LORA_A + LORA_G, g2[l]))
        prev_p = jnp.zeros((nb, 1, C_PAD), F32)
        prev_s = jnp.pad(state_shift[l], ((0, 0), (0, 0), (0, padc)))
        lng, lnb = lnx_g[l][None], lnx_b[l][None]
        oc_p, wkv_p, shift_p = _wkv(pc, prev_p, jnp.zeros((nb, HC, NC, NC), F32), vecs, mats,
                                    lng, lnb, ones_bd, nb, t, 0, ck_p, nsb_p)
        oc_s, wkv_s, shift_s = _wkv(pc, prev_s, state_wkv[l], vecs, mats,
                                    lng, lnb, ones_bd, nbs, s_len, n_p, ck_s, nsb_s)
        oc = jnp.concatenate([oc_p.reshape(n_p, WC), oc_s.reshape(n_s, WC)], axis=0)

        w_br = [_cast(w_branch, (l, k), WA, D_MODEL) for k in range(N_BRANCH)]
        merged = _merge(xb, oa, ob, oc, w_g, w_br)
        x, xb = _outproj(merged, x, _cast(w_out, (l,), D_MODEL, D_MODEL), ln_g[l, 1][None],
                         ln_b[l, 1][None], alpha)

        x, xb = _ffn(x, xb, *ffn_weights(1), ln_g[l, 2][None], ln_b[l, 2][None], alpha)

        outs["rp"].append(ret_p)
        outs["rs"].append(ret_s)
        outs["wp"].append(wkv_p)
        outs["ws"].append(wkv_s)
        outs["sp"].append(shift_p[:, :, :C_COLS])
        outs["ss"].append(shift_s[:, :, :C_COLS])

    st = {k: jnp.stack(v) for k, v in outs.items()}
    y_p = x[:n_p].reshape(nb, t, D_MODEL)
    y_s = x[n_p:].reshape(nbs, s_len, D_MODEL)
    kv_p = lambda a: a.reshape(depth, nb, t, HA, DVA)
    kv_s = lambda a: a.reshape(depth, nbs, s_len, HA, DVA)
    return (y_p, y_s, kv_p(k_p), kv_p(v_p), st["rp"], st["wp"], st["sp"],
            kv_s(k_s), kv_s(v_s), st["rs"], st["ws"], st["ss"])
```

```python
import functools
import math

import jax
import jax.numpy as jnp
from jax import lax
from jax.experimental import pallas as pl
from jax.experimental.pallas import tpu as pltpu

F32 = jnp.float32
BF16 = jnp.bfloat16

D_MODEL = 2048
CHUNK = 64
HA, DA, DVA = 8, 64, 128
ROT_A = DA // 4
ROPE_THETA = 500000.0
HB, DKB, DVB = 8, 64, 128
RET_THETA = 10000.0
HC, NC = 16, 64
LORA_W, LORA_A, LORA_G = 64, 64, 160
WA, WB, WC = HA * DVA, HB * DVB, HC * NC
N_BRANCH = 3
A_COLS = 2 * HA * 2 * DA + HA * DVA
B_COLS = 2 * HB * DKB + HB * DVB + WB
C_COLS = 3 * WC + LORA_W + LORA_A + LORA_G
D_FF = 5504
LN_EPS = 1e-5
SUBLN_EPS = 1e-5
RET_EPS = 1e-6
GN_EPS_C = 64e-5

LANE = 128
D_FF_PAD = 5632
FFN_TF = 512
LORA_PAD = 384
C_PAD = 3 * WC + LORA_PAD
ATTN_HEADS_PER_STEP = 2
VMEM_LIMIT = 56 * 1024 * 1024


def _cparams(sem):
    return pltpu.CompilerParams(dimension_semantics=sem, vmem_limit_bytes=VMEM_LIMIT)


def _pick(n, cands):
    for c in cands:
        if n % c == 0:
            return c
    raise ValueError(f"no tile for {n}")


def _sigmoid(x):
    return 1.0 / (1.0 + jnp.exp(-x))


def _ln_rows(y, g, b, eps):
    mu = jnp.mean(y, axis=-1, keepdims=True)
    d = y - mu
    var = jnp.mean(d * d, axis=-1, keepdims=True)
    return d * lax.rsqrt(var + eps) * g + b


def _dot(a, b):
    return jnp.dot(a, b, preferred_element_type=F32)


def _dot_nt(a, b):
    return lax.dot_general(a, b, (((1,), (1,)), ((), ())), preferred_element_type=F32)


def _dot_tn(a, b):
    return lax.dot_general(a, b, (((0,), (0,)), ((), ())), preferred_element_type=F32)


def _split2(x):
    hi = x.astype(BF16)
    lo = (x - hi.astype(F32)).astype(BF16)
    return hi, lo


def _split3(x):
    hi = x.astype(BF16)
    r1 = x - hi.astype(F32)
    mid = r1.astype(BF16)
    lo = (r1 - mid.astype(F32)).astype(BF16)
    return hi, mid, lo


def _mm3(a, b, dot=_dot):
    ah, al = a
    bh, bl = b
    return dot(ah, bh) + (dot(ah, bl) + dot(al, bh))


def _cast_kernel(w_ref, o_ref):
    o_ref[...] = w_ref[...].astype(BF16)


def _cast(src, lead, rows, width):
    tr = _pick(rows, (256, 128, 64, 32, 16, 8))
    return pl.pallas_call(
        _cast_kernel,
        grid=(rows // tr,),
        in_specs=[pl.BlockSpec((None,) * len(lead) + (tr, width), lambda i: lead + (i, 0))],
        out_specs=pl.BlockSpec((tr, width), lambda i: (i, 0)),
        out_shape=jax.ShapeDtypeStruct((rows, width), BF16),
        compiler_params=_cparams(("parallel",)),
        name="cast",
    )(src)


def _cast_ffn_in_kernel(a_ref, b_ref, oa_ref, ob_ref):
    for src, dst in ((a_ref, oa_ref), (b_ref, ob_ref)):
        dst[:, 0:D_FF] = src[...].astype(BF16)
        dst[:, D_FF:D_FF_PAD] = jnp.zeros((dst.shape[0], D_FF_PAD - D_FF), BF16)


def _cast_ffn_in(ffn_in, l, i):
    tr = 128
    half = lambda k: pl.BlockSpec((None, None, tr, D_FF), lambda r: (l, i, r, k))
    out = pl.BlockSpec((tr, D_FF_PAD), lambda r: (r, 0))
    return pl.pallas_call(
        _cast_ffn_in_kernel,
        grid=(D_MODEL // tr,),
        in_specs=[half(0), half(1)],
        out_specs=[out, out],
        out_shape=[jax.ShapeDtypeStruct((D_MODEL, D_FF_PAD), BF16)] * 2,
        compiler_params=_cparams(("parallel",)),
        name="cast_ffn_in",
    )(ffn_in, ffn_in)


def _cast_ffn_out_kernel(w_ref, o_ref):
    tr = w_ref.shape[0]
    row = lax.broadcasted_iota(jnp.int32, w_ref.shape, 0) + pl.program_id(0) * tr
    o_ref[...] = jnp.where(row < D_FF, w_ref[...], 0.0).astype(BF16)


def _cast_ffn_out(ffn_out, l, i):
    tr = FFN_TF
    return pl.pallas_call(
        _cast_ffn_out_kernel,
        grid=(D_FF_PAD // tr,),
        in_specs=[pl.BlockSpec((None, None, tr, D_MODEL), lambda r: (l, i, r, 0))],
        out_specs=pl.BlockSpec((tr, D_MODEL), lambda r: (r, 0)),
        out_shape=jax.ShapeDtypeStruct((D_FF_PAD, D_MODEL), BF16),
        compiler_params=_cparams(("parallel",)),
        name="cast_ffn_out",
    )(ffn_out)


def _cast_wc_kernel(w_ref, o_ref):
    lane = lax.broadcasted_iota(jnp.int32, w_ref.shape, 1)
    valid = lane < C_COLS - pl.program_id(1) * LORA_PAD
    o_ref[...] = jnp.where(valid, w_ref[...], 0.0).astype(BF16)


def _cast_wc(w_in, l):
    tr = 512
    cb0 = (A_COLS + B_COLS) // LORA_PAD
    return pl.pallas_call(
        _cast_wc_kernel,
        grid=(D_MODEL // tr, C_PAD // LORA_PAD),
        in_specs=[pl.BlockSpec((None, tr, LORA_PAD), lambda r, j: (l, r, cb0 + j))],
        out_specs=pl.BlockSpec((tr, LORA_PAD), lambda r, j: (r, j)),
        out_shape=jax.ShapeDtypeStruct((D_MODEL, C_PAD), BF16),
        compiler_params=_cparams(("parallel", "parallel")),
        name="cast_wc",
    )(w_in)


def _ffn_kernel(x_ref, xb_ref, wa_ref, wb_ref, wd_ref, g_ref, b_ref, o_ref, ob_ref, acc_ref, *, alpha):
    j = pl.program_id(1)

    @pl.when(j == 0)
    def _():
        acc_ref[...] = jnp.zeros_like(acc_ref)

    xb = xb_ref[...]
    ha = _dot(xb, wa_ref[...])
    hb = _dot(xb, wb_ref[...])
    act = (ha * _sigmoid(ha) * hb).astype(BF16)
    acc_ref[...] += _dot(act, wd_ref[...])

    @pl.when(j == pl.num_programs(1) - 1)
    def _():
        y = alpha * x_ref[...] + 0.5 * acc_ref[...]
        out = _ln_rows(y, g_ref[...], b_ref[...], LN_EPS)
        o_ref[...] = out
        ob_ref[...] = out.astype(BF16)


def _ffn(x, xb, wa, wb, wd, g, b, alpha):
    n = x.shape[0]
    tm = _pick(n, (512, 256, 128, 64, 32, 16, 8))
    tf = FFN_TF
    grid = (n // tm, D_FF_PAD // tf)
    row = pl.BlockSpec((tm, D_MODEL), lambda i, j: (i, 0))
    vec = pl.BlockSpec((1, D_MODEL), lambda i, j: (0, 0))
    return pl.pallas_call(
        functools.partial(_ffn_kernel, alpha=alpha),
        grid=grid,
        in_specs=[
            row, row,
            pl.BlockSpec((D_MODEL, tf), lambda i, j: (0, j)),
            pl.BlockSpec((D_MODEL, tf), lambda i, j: (0, j)),
            pl.BlockSpec((tf, D_MODEL), lambda i, j: (j, 0)),
            vec, vec,
        ],
        out_specs=[row, row],
        out_shape=[jax.ShapeDtypeStruct((n, D_MODEL), F32), jax.ShapeDtypeStruct((n, D_MODEL), BF16)],
        scratch_shapes=[pltpu.VMEM((tm, D_MODEL), F32)],
        compiler_params=_cparams(("parallel", "arbitrary")),
        name="ffn",
    )(x, xb, wa, wb, wd, g, b)


def _proj_kernel(x_ref, w_ref, o_ref):
    o_ref[...] = _dot(x_ref[...], w_ref[...])


def _proj(xb, w, tn, name, lead=(), m=None):
    n = xb.shape[0]
    m = w.shape[-1] if m is None else m
    tm = _pick(n, (1024, 512, 256, 128, 64, 32, 16, 8))
    return pl.pallas_call(
        _proj_kernel,
        grid=(n // tm, m // tn),
        in_specs=[
            pl.BlockSpec((tm, D_MODEL), lambda i, j: (i, 0)),
            pl.BlockSpec((None,) * len(lead) + (D_MODEL, tn), lambda i, j: lead + (0, j)),
        ],
        out_specs=pl.BlockSpec((tm, tn), lambda i, j: (i, j)),
        out_shape=jax.ShapeDtypeStruct((n, m), F32),
        compiler_params=_cparams(("parallel", "arbitrary")),
        name=name,
    )(xb, w)


def _rope_tables(pos, rot, theta, width):
    half = rot // 2
    freq = jnp.power(theta, -jnp.arange(half, dtype=F32) / half)
    ang = pos.astype(F32)[:, None] * freq[None, :]
    cos, sin = jnp.cos(ang), jnp.sin(ang)
    n = pos.shape[0]
    pad = width - rot
    c = jnp.concatenate([cos, cos, jnp.ones((n, pad), F32)], axis=-1)
    s_lo = jnp.concatenate([-sin, jnp.zeros((n, half + pad), F32)], axis=-1)
    s_hi = jnp.concatenate([jnp.zeros((n, half), F32), sin, jnp.zeros((n, pad), F32)], axis=-1)
    rep = LANE // width
    return tuple(jnp.tile(t, (1, rep)) for t in (c, s_lo, s_hi))


def _rope_kernel(xa_ref, xv_ref, xb_ref, ca_ref, la_ref, ha_ref, cb_ref, lb_ref, hb_ref,
                 qa_buf, kall_buf, vall_buf, qkb_buf, qa_ref, kall_ref, vall_ref, qkb_ref, kd_ref, vd_ref):
    del qa_buf, kall_buf, vall_buf, qkb_buf

    def rot(x, c, lo, hi, half):
        return x * c + pltpu.roll(x, LANE - half, 1) * lo + pltpu.roll(x, half, 1) * hi

    ca, la, ha = ca_ref[...], la_ref[...], ha_ref[...]
    nqa = HA * 2 * DA // LANE
    tm = xa_ref.shape[0]
    ks = []
    for t in range(2 * nqa):
        y = rot(xa_ref[:, t * LANE:(t + 1) * LANE], ca, la, ha, ROT_A // 2)
        if t < nqa:
            qa_ref[:, t * LANE:(t + 1) * LANE] = (y * (DA ** -0.5)).astype(BF16)
        else:
            ks.append(y)
            kd_ref[:, (t - nqa) * LANE:(t - nqa + 1) * LANE] = y.astype(BF16)
    kall_ref[...] = jnp.concatenate(ks, axis=1).reshape(tm, HA, 2 * DA)
    v = xv_ref[...]
    vall_ref[...] = v.reshape(tm, HA, DVA)
    vd_ref[...] = v.astype(BF16)
    cb, lb, hb = cb_ref[...], lb_ref[...], hb_ref[...]
    nqb = HB * DKB // LANE
    for t in range(2 * nqb):
        sl = slice(t * LANE, (t + 1) * LANE)
        y = rot(xb_ref[:, sl], cb, lb, hb, DKB // 2)
        if t >= nqb:
            y = y * (DKB ** -0.5)
        qkb_ref[:, sl] = y


def _rope(pab, tabs_a, tabs_b, qa_buf, kall_buf, vall_buf, qkb_buf, row0, nrows, kv_row0):
    tm = _pick(nrows, (256, 128, 64, 32, 16, 8))
    rb0, kb0 = row0 // tm, kv_row0 // tm
    wqk = 2 * HA * 2 * DA
    wb = 2 * HB * DKB
    tab = pl.BlockSpec((tm, LANE), lambda i: (rb0 + i, 0))
    anyspec = pl.BlockSpec(memory_space=pl.ANY)
    bufs = (qa_buf, kall_buf, vall_buf, qkb_buf)
    cache = pl.BlockSpec((tm, HA, DVA), lambda i: (kb0 + i, 0, 0))
    dense = pl.BlockSpec((tm, WA), lambda i: (i, 0))
    return pl.pallas_call(
        _rope_kernel,
        grid=(nrows // tm,),
        in_specs=[
            pl.BlockSpec((tm, wqk), lambda i: (rb0 + i, 0)),
            pl.BlockSpec((tm, WA), lambda i: (rb0 + i, wqk // WA)),
            pl.BlockSpec((tm, wb), lambda i: (rb0 + i, A_COLS // wb)),
            tab, tab, tab, tab, tab, tab,
            anyspec, anyspec, anyspec, anyspec,
        ],
        out_specs=[
            pl.BlockSpec((tm, WA), lambda i: (rb0 + i, 0)),
            cache, cache,
            pl.BlockSpec((tm, wb), lambda i: (rb0 + i, 0)),
            dense, dense,
        ],
        out_shape=[jax.ShapeDtypeStruct(b.shape, b.dtype) for b in bufs]
                  + [jax.ShapeDtypeStruct((nrows, WA), BF16)] * 2,
        input_output_aliases={9: 0, 10: 1, 11: 2, 12: 3},
        compiler_params=_cparams(("parallel",)),
        name="rope",
    )(pab, pab, pab, *tabs_a, *tabs_b, *bufs)


def _diff_lambda(dl_ref, lam_init):
    lv = dl_ref[...]
    s1 = jnp.sum(lv[0:1] * lv[1:2], axis=-1, keepdims=True)
    s2 = jnp.sum(lv[2:3] * lv[3:4], axis=-1, keepdims=True)
    return jnp.exp(s1) - jnp.exp(s2) + lam_init


def _split_maps(q):
    lane = lax.broadcasted_iota(jnp.int32, q.shape, 1)
    zero = jnp.zeros_like(q)
    return jnp.where(lane < DA, q, zero), jnp.where(lane >= DA, q, zero)


def _subln(o, g, scale):
    ms = jnp.mean(o * o, axis=-1, keepdims=True)
    return o * lax.rsqrt(ms + SUBLN_EPS) * g * scale


def _attn_prompt_kernel(q_ref, kb_ref, v_ref, dl_ref, g_ref, o_buf, o_ref, vb_ref, s_ref,
                        m_ref, acc_ref, *, tq, tk, nh, lam_init):
    del o_buf
    qi = pl.program_id(2)
    hsl = [slice(h * 2 * DA, (h + 1) * 2 * DA) for h in range(nh)]
    chains = range(2 * nh)

    @pl.when(qi == 0)
    def _():
        for h in range(nh):
            vb_ref[:, 2 * h * DVA:(2 * h + 1) * DVA] = v_ref[:, hsl[h]]
            vb_ref[:, (2 * h + 1) * DVA:(2 * h + 2) * DVA] = jnp.ones((v_ref.shape[0], DVA), BF16)

    lam = _diff_lambda(dl_ref, lam_init)
    qs = [qm for h in range(nh) for qm in _split_maps(q_ref[:, hsl[h]])]

    def scores(kb, slot):
        rows = pl.ds(pl.multiple_of(kb * tk, tk), tk)
        for c in chains:
            s_ref[slot, c] = _dot_nt(qs[c], kb_ref[rows, hsl[c // 2]])

    def consume(ss, start, width):
        stats = []
        for c in chains:
            m = m_ref[c]
            m_new = jnp.maximum(m, jnp.max(ss[c], axis=-1, keepdims=True))
            m_ref[c] = m_new
            stats.append((jnp.exp(m - m_new), jnp.exp(ss[c] - m_new).astype(BF16)))
        pvs = [_dot(stats[c][1], vb_ref[pl.ds(start, width), (c // 2) * 2 * DVA:(c // 2 + 1) * 2 * DVA])
               for c in chains]
        for c in chains:
            acc_ref[c] = stats[c][0] * acc_ref[c] + pvs[c]

    def full_step(kb, slot):
        scores(kb + 1, 1 - slot)
        consume([s_ref[slot, c] for c in chains], pl.multiple_of(kb * tk, tk), tk)

    q0 = qi * tq
    n_full = q0 // tk
    m_ref[...] = jnp.full(m_ref.shape, -1e30, F32)
    acc_ref[...] = jnp.zeros_like(acc_ref)
    scores(0, 0)

    def two_steps(j, _):
        full_step(2 * j, 0)
        full_step(2 * j + 1, 1)
        return 0

    lax.fori_loop(0, n_full // 2, two_steps, 0)

    @pl.when(n_full % 2 == 1)
    def _():
        full_step(n_full - 1, 0)

    tail = pl.multiple_of(n_full * tk, tk)
    g = g_ref[...]

    for r in range(tk // tq):
        @pl.when(q0 - tail == r * tq)
        def _():
            width = (r + 1) * tq
            row = lax.broadcasted_iota(jnp.int32, (tq, width), 0) // CHUNK
            col = lax.broadcasted_iota(jnp.int32, (tq, width), 1) // CHUNK
            ok = col - row <= r * (tq // CHUNK)
            consume([jnp.where(ok, s_ref[n_full % 2, c, :, 0:width], -1e30) for c in chains], tail, width)
            for h in range(nh):
                a1, a2 = acc_ref[2 * h], acc_ref[2 * h + 1]
                o = a1[:, 0:DVA] / a1[:, DVA:2 * DVA] - lam * (a2[:, 0:DVA] / a2[:, DVA:2 * DVA])
                o_ref[:, hsl[h]] = _subln(o, g, 1.0 - lam_init).astype(BF16)


def _attn_prompt(qa, kd, vd, dl, g, o_buf, nb, t, lam_init):
    tq = _pick(t, (512, 256, 128, 64))
    tk = _pick(t, (1024, 512, 256, 128, 64))
    nh = ATTN_HEADS_PER_STEP
    nq = t // tq
    kv = pl.BlockSpec((t, nh * 2 * DA), lambda b, h, i: (b, h))
    return pl.pallas_call(
        functools.partial(_attn_prompt_kernel, tq=tq, tk=tk, nh=nh, lam_init=lam_init),
        grid=(nb, HA // nh, nq),
        in_specs=[
            pl.BlockSpec((tq, nh * 2 * DA), lambda b, h, i: (b * nq + i, h)),
            kv, kv,
            pl.BlockSpec((4, DA), lambda b, h, i: (0, 0)),
            pl.BlockSpec((1, DVA), lambda b, h, i: (0, 0)),
            pl.BlockSpec(memory_space=pl.ANY),
        ],
        out_specs=pl.BlockSpec((tq, nh * DVA), lambda b, h, i: (b * nq + i, h)),
        out_shape=jax.ShapeDtypeStruct(o_buf.shape, o_buf.dtype),
        input_output_aliases={5: 0},
        scratch_shapes=[pltpu.VMEM((t, nh * 2 * DVA), BF16),
                        pltpu.VMEM((2, 2 * nh, tq, tk), F32),
                        pltpu.VMEM((2 * nh, tq, 1), F32), pltpu.VMEM((2 * nh, tq, 2 * DVA), F32)],
        compiler_params=_cparams(("parallel", "parallel", "arbitrary")),
        name="attn_prompt",
    )(qa, kd, vd, dl, g, o_buf)


def _attn_sample_kernel(q_ref, kn_ref, vn_ref, ck_ref, cv_ref, dl_ref, g_ref, o_buf, o_ref, *,
                        s_len, lam_init):
    del o_buf
    lam = _diff_lambda(dl_ref, lam_init)
    g = g_ref[...]
    p_len = ck_ref.shape[0]
    kc_all = ck_ref[...].reshape(p_len, WA).astype(BF16)
    vc_all = cv_ref[...].reshape(p_len, WA).astype(BF16)
    heads = range(HA)
    sls = [slice(h * 2 * DA, (h + 1) * 2 * DA) for h in heads]
    qq = [jnp.concatenate(_split_maps(q_ref[:, sl]), axis=0) for sl in sls]
    s_c = [_dot_nt(qq[h], kc_all[:, sls[h]]) for h in heads]
    s_n = [_dot_nt(qq[h], kn_ref[:, sls[h]]) for h in heads]
    p_c, p_n, den = [], [], []
    for h in heads:
        m = jnp.maximum(jnp.max(s_c[h], axis=-1, keepdims=True), jnp.max(s_n[h], axis=-1, keepdims=True))
        e_c = jnp.exp(s_c[h] - m)
        e_n = jnp.exp(s_n[h] - m)
        den.append(jnp.sum(e_c, axis=-1, keepdims=True) + jnp.sum(e_n, axis=-1, keepdims=True))
        p_c.append(e_c.astype(BF16))
        p_n.append(e_n.astype(BF16))
    pv = [_dot(p_c[h], vc_all[:, sls[h]]) + _dot(p_n[h], vn_ref[:, sls[h]]) for h in heads]
    for h in heads:
        o = pv[h] / den[h]
        od = o[:s_len] - lam * o[s_len:]
        o_ref[:, sls[h]] = _subln(od, g, 1.0 - lam_init).astype(BF16)


def _attn_sample(qa, kd, vd, ck, cv, dl, g, o_buf, layer, nbs, s_len, row0, lam_init):
    p_len = ck.shape[2]
    rb0 = row0 // s_len
    new = pl.BlockSpec((s_len, WA), lambda b: (b, 0))
    cache = pl.BlockSpec((None, None, p_len, HA, DVA), lambda b: (layer, b, 0, 0, 0))
    return pl.pallas_call(
        functools.partial(_attn_sample_kernel, s_len=s_len, lam_init=lam_init),
        grid=(nbs,),
        in_specs=[
            pl.BlockSpec((s_len, WA), lambda b: (rb0 + b, 0)),
            new, new, cache, cache,
            pl.BlockSpec((4, DA), lambda b: (0, 0)),
            pl.BlockSpec((1, DVA), lambda b: (0, 0)),
            pl.BlockSpec(memory_space=pl.ANY),
        ],
        out_specs=pl.BlockSpec((s_len, WA), lambda b: (rb0 + b, 0)),
        out_shape=jax.ShapeDtypeStruct(o_buf.shape, o_buf.dtype),
        input_output_aliases={7: 0},
        compiler_params=_cparams(("parallel",)),
        name="attn_sample",
    )(qa, kd, vd, ck, cv, dl, g, o_buf)


def _ret_kernel(q_ref, k_ref, v_ref, gate_ref, r0_ref, o_buf, o_ref, rout_ref, st_ref, *, blk):
    del o_buf
    c = pl.program_id(1)

    @pl.when(c == 0)
    def _():
        st_ref[...] = jnp.zeros_like(st_ref)
        for h in range(HB):
            e = h % 2
            st_ref[h, e * DKB:(e + 1) * DKB, :] = r0_ref[h]

    n_i = lax.broadcasted_iota(jnp.int32, (blk, blk), 0)
    m_i = lax.broadcasted_iota(jnp.int32, (blk, blk), 1)
    diff = (n_i - m_i).astype(F32)
    causal = n_i >= m_i
    n_col = lax.broadcasted_iota(jnp.int32, (blk, LANE), 0).astype(F32)
    lane = lax.broadcasted_iota(jnp.int32, (blk, LANE), 1)
    heads = range(HB)
    log_g = [math.log1p(-2.0 ** (-5.0 - h)) for h in heads]
    vsl = [slice(h * DVB, (h + 1) * DVB) for h in heads]
    qb, kb, kd, v = [], [], [], []
    for h in heads:
        p, e = h // 2, h % 2
        sl = slice(p * LANE, (p + 1) * LANE)
        mine = (lane >= DKB) if e else (lane < DKB)
        k = jnp.where(mine, k_ref[:, sl], 0.0)
        qb.append(jnp.where(mine, q_ref[:, sl], 0.0).astype(BF16))
        kb.append(k.astype(BF16))
        kd.append((k * jnp.exp((blk - 1.0 - n_col) * log_g[h])).astype(BF16))
        v.append(v_ref[:, vsl[h]].astype(BF16))
    r_old = [st_ref[h] for h in heads]
    s = [_dot_nt(qb[h], kb[h]) for h in heads]
    cross = [_dot(qb[h], r_old[h].astype(BF16)) for h in heads]
    grow = [_dot_tn(kd[h], v[h]) for h in heads]
    sd = [(s[h] * jnp.where(causal, jnp.exp(jnp.maximum(diff, 0.0) * log_g[h]), 0.0)).astype(BF16)
          for h in heads]
    intra = [_dot(sd[h], v[h]) for h in heads]
    for h in heads:
        st_ref[h] = math.exp(blk * log_g[h]) * r_old[h] + grow[h]
        o = intra[h] + cross[h] * jnp.exp((n_col + 1.0) * log_g[h])
        ms = jnp.mean(o * o, axis=-1, keepdims=True)
        gt = gate_ref[:, vsl[h]]
        o_ref[:, vsl[h]] = (o * lax.rsqrt(ms + RET_EPS) * (gt * _sigmoid(gt))).astype(BF16)

    @pl.when(c == pl.num_programs(1) - 1)
    def _():
        for h in range(HB):
            e = h % 2
            rout_ref[h] = st_ref[h, e * DKB:(e + 1) * DKB, :]


def _retention(qkb, pab, r0, o_buf, nseg, seg_len, row0, blk):
    nc = seg_len // blk
    rb0 = row0 // blk
    vcol = (A_COLS + 2 * HB * DKB) // WB
    row = lambda b, c: rb0 + b * nc + c
    return pl.pallas_call(
        functools.partial(_ret_kernel, blk=blk),
        grid=(nseg, nc),
        in_specs=[
            pl.BlockSpec((blk, HB * DKB), lambda b, c: (row(b, c), 0)),
            pl.BlockSpec((blk, HB * DKB), lambda b, c: (row(b, c), 1)),
            pl.BlockSpec((blk, WB), lambda b, c: (row(b, c), vcol)),
            pl.BlockSpec((blk, WB), lambda b, c: (row(b, c), vcol + 1)),
            pl.BlockSpec((None, HB, DKB, DVB), lambda b, c: (b, 0, 0, 0)),
            pl.BlockSpec(memory_space=pl.ANY),
        ],
        out_specs=[
            pl.BlockSpec((blk, WB), lambda b, c: (row(b, c), 0)),
            pl.BlockSpec((None, HB, DKB, DVB), lambda b, c: (b, 0, 0, 0)),
        ],
        out_shape=[jax.ShapeDtypeStruct(o_buf.shape, o_buf.dtype),
                   jax.ShapeDtypeStruct((nseg, HB, DKB, DVB), F32)],
        input_output_aliases={5: 0},
        scratch_shapes=[pltpu.VMEM((HB, 2 * DKB, DVB), F32)],
        compiler_params=_cparams(("parallel", "arbitrary")),
        name="retention",
    )(qkb, qkb, pab, pab, r0, o_buf)


def _head_sum(x, ones_bd):
    hi, lo = _split2(x)
    return _dot(hi, ones_bd) + _dot(lo, ones_bd)


def _wkv_kernel(*refs, ck, nsb):
    pc_refs = refs[:nsb]
    (prev_ref, s0_ref, mu_ref, w0_ref, a0_ref, kk_ref, ka_ref, rk_ref, w2_ref, a2_ref, g2_ref,
     lng_ref, lnb_ref, tri_ref, ones_ref, o_ref, sout_ref, last_ref, st_ref, carry_ref) = refs[nsb:]
    c = pl.program_id(1)

    @pl.when(c == 0)
    def _():
        zero = jnp.zeros((NC, NC), F32)
        for g in range(nsb):
            for p in range(HC // 2):
                top = jnp.concatenate([s0_ref[g, 2 * p], zero], axis=1)
                bot = jnp.concatenate([zero, s0_ref[g, 2 * p + 1]], axis=1)
                st_ref[g, p] = jnp.concatenate([top, bot], axis=0)
        carry_ref[...] = prev_ref[...]

    row = lax.broadcasted_iota(jnp.int32, (ck, C_PAD), 0)
    pcs, shifted = [], []
    for g, pc_ref in enumerate(pc_refs):
        pc = pc_ref[...]
        pcs.append(pc)
        shifted.append(jnp.where(row == 0, carry_ref[g], pltpu.roll(pc, 1, 0)))
        carry_ref[g] = pc[ck - 1:ck, :]
        last_ref[g] = pc[ck - 1:ck, :]
    pc_all = jnp.concatenate(pcs, axis=0)
    pcm = pc_all + mu_ref[...] * (jnp.concatenate(shifted, axis=0) - pc_all)
    rc = pcm[:, 0:WC]
    kc = pcm[:, WC:2 * WC]
    vc = pcm[:, 2 * WC:3 * WC]
    lora = pcm[:, 3 * WC:C_PAD]
    z = w0_ref[...] + _dot(jnp.tanh(lora).astype(BF16), w2_ref[...])
    u = -z
    softplus = jnp.maximum(u, 0.0) + jnp.log1p(jnp.exp(-jnp.abs(u)))
    lw = -jnp.exp(-softplus - 0.5)
    a_sig = _sigmoid(a0_ref[...] + _dot(lora.astype(BF16), a2_ref[...]))
    gate = _dot(_sigmoid(lora).astype(BF16), g2_ref[...])
    kk = kc * kk_ref[...]
    kc2 = kc * (1.0 + (a_sig - 1.0) * ka_ref[...])
    rkr = rc * kc2 * rk_ref[...]
    ones_bd = ones_ref[...]

    tri = tri_ref[...]
    l1, l2, l3 = _split3(lw)
    cum = _dot(tri, l1) + (_dot(tri, l2) + _dot(tri, l3))
    total = jnp.concatenate(
        [jnp.broadcast_to(cum[(g + 1) * ck - 1:(g + 1) * ck, :], (ck, WC)) for g in range(nsb)], axis=0)
    e_neg = jnp.exp(-cum)
    e_rem = jnp.exp(total - cum)
    e_exc = jnp.exp(cum - lw)
    r_t = rc * jnp.exp(cum)
    k_t = kc2 * e_neg
    k_p = kc2 * e_rem
    p_tot = jnp.exp(total)

    lane = lax.broadcasted_iota(jnp.int32, (ck, LANE), 1)
    m0 = lane < NC

    def stack(x):
        return jnp.concatenate([jnp.where(m0, x, 0.0), jnp.where(m0, 0.0, x)], axis=0)

    n2 = 2 * ck
    ri = lax.broadcasted_iota(jnp.int32, (n2, n2), 0)
    ci = lax.broadcasted_iota(jnp.int32, (n2, n2), 1)
    same = (ri // ck) == (ci // ck)
    strict = jnp.logical_and(same, ri > ci)
    incl = jnp.logical_and(same, ri >= ci)

    bf = lambda t: t.astype(BF16)
    pairs = range(HC // 2)
    chains = [(g, p) for g in range(nsb) for p in pairs]
    a_s, r_s, b_s, k_s, v_s, bp_s, kp_s, bonus = {}, {}, {}, {}, {}, {}, {}, {}
    for p in pairs:
        sl = slice(p * LANE, (p + 1) * LANE)
        kkt = kk[:, sl]
        kn = kkt / jnp.maximum(jnp.sqrt(_head_sum(kkt * kkt, ones_bd)), 1e-12)
        bn = kn * a_sig[:, sl]
        a_tp = -kn * e_exc[:, sl]
        b_tp = bn * e_neg[:, sl]
        b_pp = bn * e_rem[:, sl]
        bonus_p = _head_sum(rkr[:, sl], ones_bd) * vc[:, sl]
        for g in range(nsb):
            rows = slice(g * ck, (g + 1) * ck)
            a_s[g, p] = bf(stack(a_tp[rows]))
            r_s[g, p] = bf(stack(r_t[rows, sl]))
            b_s[g, p] = bf(stack(b_tp[rows]))
            k_s[g, p] = bf(stack(k_t[rows, sl]))
            v_s[g, p] = _split2(stack(vc[rows, sl]))
            bp_s[g, p] = _split2(stack(b_pp[rows]))
            kp_s[g, p] = _split2(stack(k_p[rows, sl]))
            bonus[g, p] = bonus_p[rows]

    s_old = {gp: st_ref[gp[0], gp[1]] for gp in chains}
    s_b = {gp: bf(s_old[gp]) for gp in chains}
    if n2 % LANE == 0:
        big = {gp: _dot_nt(jnp.concatenate([a_s[gp], r_s[gp]], axis=0),
                           jnp.concatenate([b_s[gp], k_s[gp]], axis=0)) for gp in chains}
        ab = {gp: big[gp][0:n2, 0:n2] for gp in chains}
        ak = {gp: big[gp][0:n2, n2:2 * n2] for gp in chains}
        rb = {gp: big[gp][n2:2 * n2, 0:n2] for gp in chains}
        rk = {gp: big[gp][n2:2 * n2, n2:2 * n2] for gp in chains}
    else:
        ab = {gp: _dot_nt(a_s[gp], b_s[gp]) for gp in chains}
        ak = {gp: _dot_nt(a_s[gp], k_s[gp]) for gp in chains}
        rb = {gp: _dot_nt(r_s[gp], b_s[gp]) for gp in chains}
        rk = {gp: _dot_nt(r_s[gp], k_s[gp]) for gp in chains}
    lk = {gp: bf(jnp.where(strict, ab[gp], 0.0)) for gp in chains}
    l_ak = {gp: bf(jnp.where(strict, ak[gp], 0.0)) for gp in chains}
    l_rb = {gp: bf(jnp.where(incl, rb[gp], 0.0)) for gp in chains}
    l_rk = {gp: bf(jnp.where(incl, rk[gp], 0.0)) for gp in chains}
    x = {gp: _dot_nt(a_s[gp], s_b[gp]) + _dot(l_ak[gp], v_s[gp][0]) for gp in chains}
    y_part = {gp: _dot_nt(r_s[gp], s_b[gp]) + _dot(l_rk[gp], v_s[gp][0]) for gp in chains}
    span = 1
    while span < ck:
        x = {gp: x[gp] + _dot(lk[gp], bf(x[gp])) for gp in chains}
        span *= 2
        if span < ck:
            lk = {gp: bf(_dot(lk[gp], lk[gp])) for gp in chains}
    x_sp = {gp: _split2(x[gp]) for gp in chains}
    y_st = {gp: y_part[gp] + _dot(l_rb[gp], x_sp[gp][0]) for gp in chains}
    for g, p in chains:
        sl = slice(p * LANE, (p + 1) * LANE)
        decay = p_tot[(g + 1) * ck - 1:(g + 1) * ck, sl]
        uv = tuple(jnp.concatenate([xs, vs], axis=0) for xs, vs in zip(x_sp[g, p], v_s[g, p]))
        bk = tuple(jnp.concatenate([bs, ks], axis=0) for bs, ks in zip(bp_s[g, p], kp_s[g, p]))
        st_ref[g, p] = s_old[g, p] * decay + _mm3(uv, bk, _dot_tn)
    for g, p in chains:
        sl = slice(p * LANE, (p + 1) * LANE)
        rows = slice(g * ck, (g + 1) * ck)
        y = y_st[g, p][0:ck] + y_st[g, p][ck:n2]
        mean = _head_sum(y, ones_bd) * (1.0 / NC)
        d = y - mean
        var = _head_sum(d * d, ones_bd) * (1.0 / NC)
        yn = d * lax.rsqrt(var + GN_EPS_C) * lng_ref[:, sl] + lnb_ref[:, sl]
        o_ref[g, :, sl] = ((yn + bonus[g, p]) * gate[rows, sl]).astype(BF16)

    @pl.when(c == pl.num_programs(1) - 1)
    def _():
        for g in range(nsb):
            for p in range(HC // 2):
                s_pair = st_ref[g, p]
                sout_ref[g, 2 * p] = s_pair[0:NC, 0:NC]
                sout_ref[g, 2 * p + 1] = s_pair[NC:2 * NC, NC:2 * NC]


def _wkv(pc, prev, s0, vecs, mats, lng, lnb, ones_bd, nseg, seg_len, row0, ck, nsb):
    nc = seg_len // ck
    rb0 = row0 // ck
    pos = jnp.arange(nsb * ck)
    tri = ((pos[:, None] // ck == pos[None, :] // ck) & (pos[:, None] >= pos[None, :])).astype(BF16)
    pc_spec = lambda g: pl.BlockSpec((ck, C_PAD), lambda b, c: (rb0 + (b * nsb + g) * nc + c, 0))
    vec = lambda w: pl.BlockSpec((1, w), lambda b, c: (0, 0))
    mat = pl.BlockSpec((LORA_PAD, WC), lambda b, c: (0, 0))
    shift = pl.BlockSpec((nsb, 1, C_PAD), lambda b, c: (b, 0, 0))
    st = pl.BlockSpec((nsb, HC, NC, NC), lambda b, c: (b, 0, 0, 0))
    return pl.pallas_call(
        functools.partial(_wkv_kernel, ck=ck, nsb=nsb),
        grid=(nseg // nsb, nc),
        in_specs=[pc_spec(g) for g in range(nsb)] + [
            shift, st,
            vec(C_PAD), vec(WC), vec(WC), vec(WC), vec(WC), vec(WC),
            mat, mat, mat, vec(WC), vec(WC),
            pl.BlockSpec((nsb * ck, nsb * ck), lambda b, c: (0, 0)),
            pl.BlockSpec((LANE, LANE), lambda b, c: (0, 0)),
        ],
        out_specs=[pl.BlockSpec((nsb, ck, WC), lambda b, c: (b, c, 0)), st, shift],
        out_shape=[jax.ShapeDtypeStruct((nseg, seg_len, WC), BF16),
                   jax.ShapeDtypeStruct((nseg, HC, NC, NC), F32),
                   jax.ShapeDtypeStruct((nseg, 1, C_PAD), F32)],
        scratch_shapes=[pltpu.VMEM((nsb, HC // 2, LANE, LANE), F32), pltpu.VMEM((nsb, 1, C_PAD), F32)],
        compiler_params=_cparams(("parallel", "arbitrary")),
        name="wkv",
    )(*([pc] * nsb), prev, s0, *vecs, *mats, lng, lnb, tri, ones_bd)


def _merge_kernel(xb_ref, oa_ref, ob_ref, oc_ref, ga_ref, gb_ref, gc_ref, wa_ref, wb_ref, wc_ref, o_ref):
    xb = xb_ref[...]
    zs = [_dot(xb, g_ref[...]) for g_ref in (ga_ref, gb_ref, gc_ref)]
    ys = [_dot(b_ref[...], w_ref[...])
          for b_ref, w_ref in ((oa_ref, wa_ref), (ob_ref, wb_ref), (oc_ref, wc_ref))]
    o_ref[...] = (_sigmoid(zs[0]) * ys[0] + _sigmoid(zs[1]) * ys[1]) + _sigmoid(zs[2]) * ys[2]


def _merge(xb, oa, ob, oc, w_gate, wbr):
    n = oa.shape[0]
    tm = _pick(n, (1024, 512, 256, 128, 64, 32, 16, 8))
    tn = 512
    nj = D_MODEL // tn
    bw = oa.shape[1]
    o_spec = pl.BlockSpec((tm, bw), lambda i, j: (i, 0))
    g_spec = lambda k: pl.BlockSpec((D_MODEL, tn), lambda i, j: (0, k * nj + j))
    w_spec = pl.BlockSpec((bw, tn), lambda i, j: (0, j))
    return pl.pallas_call(
        _merge_kernel,
        grid=(n // tm, nj),
        in_specs=[pl.BlockSpec((tm, D_MODEL), lambda i, j: (i, 0)), o_spec, o_spec, o_spec,
                  g_spec(0), g_spec(1), g_spec(2), w_spec, w_spec, w_spec],
        out_specs=pl.BlockSpec((tm, tn), lambda i, j: (i, j)),
        out_shape=jax.ShapeDtypeStruct((n, D_MODEL), F32),
        compiler_params=_cparams(("parallel", "arbitrary")),
        name="merge",
    )(xb, oa, ob, oc, w_gate, w_gate, w_gate, *wbr)


def _outproj_kernel(m_ref, x_ref, w_ref, g_ref, b_ref, o_ref, ob_ref, *, alpha):
    half = m_ref.shape[0] // 2
    for r in (slice(0, half), slice(half, 2 * half)):
        y = alpha * x_ref[r, :] + _dot(m_ref[r, :].astype(BF16), w_ref[...])
        out = _ln_rows(y, g_ref[...], b_ref[...], LN_EPS)
        o_ref[r, :] = out
        ob_ref[r, :] = out.astype(BF16)


def _outproj(merged, x, w, g, b, alpha):
    n = x.shape[0]
    tm = _pick(n, (512, 256, 128, 64, 32))
    row = pl.BlockSpec((tm, D_MODEL), lambda i: (i, 0))
    vec = pl.BlockSpec((1, D_MODEL), lambda i: (0, 0))
    return pl.pallas_call(
        functools.partial(_outproj_kernel, alpha=alpha),
        grid=(n // tm,),
        in_specs=[row, row, pl.BlockSpec((D_MODEL, D_MODEL), lambda i: (0, 0)), vec, vec],
        out_specs=[row, row],
        out_shape=[jax.ShapeDtypeStruct((n, D_MODEL), F32), jax.ShapeDtypeStruct((n, D_MODEL), BF16)],
        compiler_params=_cparams(("parallel",)),
        name="outproj",
    )(merged, x, w, g, b)


def kernel(x_prompt, x_sample, cache_k, cache_v, state_ret, state_wkv, state_shift, w_in, w_gate,
           w_branch, w_out, ffn_in, ffn_out, ln_g, ln_b, diff_lambda, subln_g, mu_c, w0, w2, a0, a2,
           g2, k_k, k_a, r_k, lnx_g, lnx_b):
    nb, t, _ = x_prompt.shape
    nbs, s_len, _ = x_sample.shape
    depth = w_in.shape[0]
    p_len = cache_k.shape[2]
    n_p = nb * t
    n_s = nbs * s_len
    n = n_p + n_s
    alpha = (2 * depth) ** 0.25

    x = jnp.concatenate([x_prompt.reshape(n_p, D_MODEL), x_sample.reshape(n_s, D_MODEL)], axis=0)
    xb = x.astype(BF16)

    pos = jnp.concatenate([jnp.tile(jnp.arange(t, dtype=jnp.int32), nb),
                           jnp.tile(p_len + jnp.arange(s_len, dtype=jnp.int32), nbs)])
    tabs_a = _rope_tables(pos, ROT_A, ROPE_THETA, DA)
    tabs_b = _rope_tables(pos, DKB, RET_THETA, DKB)

    ck_all, cv_all = cache_k, cache_v

    eye = jnp.arange(LANE) // NC
    ones_bd = (eye[:, None] == eye[None, :]).astype(BF16)
    ck_p = _pick(t, (CHUNK,))
    ck_s = s_len
    nsb_p = _pick(nb, (2, 1))
    nsb_s = _pick(nbs, (4, 2, 1))
    ret_blk = _pick(t, (256, 128, 64))

    outs = {k: [] for k in ("rp", "wp", "sp", "rs", "ws", "ss")}

    w_in_b = w_in.astype(BF16)

    qa = jnp.zeros((n, WA), BF16)
    qkb = jnp.zeros((n, 2 * HB * DKB), F32)
    k_p, v_p = (jnp.zeros((depth * n_p, HA, DVA), F32) for _ in range(2))
    k_s, v_s = (jnp.zeros((depth * n_s, HA, DVA), F32) for _ in range(2))
    oa, ob = jnp.zeros((n, WA), BF16), jnp.zeros((n, WB), BF16)

    for l in range(depth):
        lam_init = 0.8 - 0.6 * math.exp(-0.3 * l)

        def ffn_weights(i):
            return (*_cast_ffn_in(ffn_in, l, i), _cast_ffn_out(ffn_out, l, i))

        x, xb = _ffn(x, xb, *ffn_weights(0), ln_g[l, 0][None], ln_b[l, 0][None], alpha)

        w_c = _cast_wc(w_in_b, l)
        pab = _proj(xb, w_in_b, 1024, name="proj_ab", lead=(l,), m=A_COLS + B_COLS)
        pc = _proj(xb, w_c, C_PAD // 3, name="proj_c")
        w_g = _cast(w_gate, (l,), D_MODEL, N_BRANCH * D_MODEL)
        qa, k_p, v_p, qkb, kd_p, vd_p = _rope(pab, tabs_a, tabs_b, qa, k_p, v_p, qkb, 0, n_p, l * n_p)
        qa, k_s, v_s, qkb, kd_s, vd_s = _rope(pab, tabs_a, tabs_b, qa, k_s, v_s, qkb, n_p, n_s, l * n_s)

        dl = diff_lambda[l]
        sg = subln_g[l][None]
        oa = _attn_prompt(qa, kd_p, vd_p, dl, sg, oa, nb, t, lam_init)
        oa = _attn_sample(qa, kd_s, vd_s, ck_all, cv_all, dl, sg, oa, l, nbs, s_len, n_p, lam_init)

        ob, ret_p = _retention(qkb, pab, jnp.zeros((nb, HB, DKB, DVB), F32), ob, nb, t, 0, ret_blk)
        ob, ret_s = _retention(qkb, pab, state_ret[l], ob, nbs, s_len, n_p, s_len)

        padc = C_PAD - C_COLS
        vecs = (jnp.pad(mu_c[l], (0, padc))[None], w0[l][None], a0[l][None], k_k[l][None],
                k_a[l][None], r_k[l].reshape(1, WC))
        zrow = lambda r0, r1, m: jnp.pad(m.astype(BF16), ((r0, LORA_PAD - r1), (0, 0)))
        mats = (zrow(0, LORA_W, w2[l]), zrow(LORA_W, LORA_W + LORA_A, a2[l]),
                zrow(LORA_W + LORA_A, LORA_W + LORA_A + LORA_G, g2[l]))
        prev_p = jnp.zeros((nb, 1, C_PAD), F32)
        prev_s = jnp.pad(state_shift[l], ((0, 0), (0, 0), (0, padc)))
        lng, lnb = lnx_g[l][None], lnx_b[l][None]
        oc_p, wkv_p, shift_p = _wkv(pc, prev_p, jnp.zeros((nb, HC, NC, NC), F32), vecs, mats,
                                    lng, lnb, ones_bd, nb, t, 0, ck_p, nsb_p)
        oc_s, wkv_s, shift_s = _wkv(pc, prev_s, state_wkv[l], vecs, mats,
                                    lng, lnb, ones_bd, nbs, s_len, n_p, ck_s, nsb_s)
        oc = jnp.concatenate([oc_p.reshape(n_p, WC), oc_s.reshape(n_s, WC)], axis=0)

        w_br = [_cast(w_branch, (l, k), WA, D_MODEL) for k in range(N_BRANCH)]
        merged = _merge(xb, oa, ob, oc, w_g, w_br)
        x, xb = _outproj(merged, x, _cast(w_out, (l,), D_MODEL, D_MODEL), ln_g[l, 1][None],
                         ln_b[l, 1][None], alpha)

        x, xb = _ffn(x, xb, *ffn_weights(1), ln_g[l, 2][None], ln_b[l, 2][None], alpha)

        outs["rp"].append(ret_p)
        outs["rs"].append(ret_s)
        outs["wp"].append(wkv_p)
        outs["ws"].append(wkv_s)
        outs["sp"].append(shift_p[:, :, :C_COLS])
        outs["ss"].append(shift_s[:, :, :C_COLS])

    st = {k: jnp.stack(v) for k, v in outs.items()}
    y_p = x[:n_p].reshape(nb, t, D_MODEL)
    y_s = x[n_p:].reshape(nbs, s_len, D_MODEL)
    kv_p = lambda a: a.reshape(depth, nb, t, HA, DVA)
    kv_s = lambda a: a.reshape(depth, nbs, s_len, HA, DVA)
    return (y_p, y_s, kv_p(k_p), kv_p(v_p), st["rp"], st["wp"], st["sp"],
            kv_s(k_s), kv_s(v_s), st["rs"], st["ws"], st["ss"])
```

```python
import functools
import math

import jax
import jax.numpy as jnp
from jax import lax
from jax.experimental import pallas as pl
from jax.experimental.pallas import tpu as pltpu

F32 = jnp.float32
BF16 = jnp.bfloat16

D_MODEL = 2048
CHUNK = 64
HA, DA, DVA = 8, 64, 128
ROT_A = DA // 4
ROPE_THETA = 500000.0
HB, DKB, DVB = 8, 64, 128
RET_THETA = 10000.0
HC, NC = 16, 64
LORA_W, LORA_A, LORA_G = 64, 64, 160
WA, WB, WC = HA * DVA, HB * DVB, HC * NC
N_BRANCH = 3
A_COLS = 2 * HA * 2 * DA + HA * DVA
B_COLS = 2 * HB * DKB + HB * DVB + WB
C_COLS = 3 * WC + LORA_W + LORA_A + LORA_G
D_FF = 5504
LN_EPS = 1e-5
SUBLN_EPS = 1e-5
RET_EPS = 1e-6
GN_EPS_C = 64e-5

LANE = 128
D_FF_PAD = 5632
FFN_TF = 512
LORA_PAD = 384
C_PAD = 3 * WC + LORA_PAD
ATTN_HEADS_PER_STEP = 2
VMEM_LIMIT = 56 * 1024 * 1024


def _cparams(sem):
    return pltpu.CompilerParams(dimension_semantics=sem, vmem_limit_bytes=VMEM_LIMIT)


def _pick(n, cands):
    for c in cands:
        if n % c == 0:
            return c
    raise ValueError(f"no tile for {n}")


def _sigmoid(x):
    return 1.0 / (1.0 + jnp.exp(-x))


def _ln_rows(y, g, b, eps):
    mu = jnp.mean(y, axis=-1, keepdims=True)
    d = y - mu
    var = jnp.mean(d * d, axis=-1, keepdims=True)
    return d * lax.rsqrt(var + eps) * g + b


def _dot(a, b):
    return jnp.dot(a, b, preferred_element_type=F32)


def _dot_nt(a, b):
    return lax.dot_general(a, b, (((1,), (1,)), ((), ())), preferred_element_type=F32)


def _dot_tn(a, b):
    return lax.dot_general(a, b, (((0,), (0,)), ((), ())), preferred_element_type=F32)


def _split2(x):
    hi = x.astype(BF16)
    lo = (x - hi.astype(F32)).astype(BF16)
    return hi, lo


def _split3(x):
    hi = x.astype(BF16)
    r1 = x - hi.astype(F32)
    mid = r1.astype(BF16)
    lo = (r1 - mid.astype(F32)).astype(BF16)
    return hi, mid, lo


def _mm3(a, b, dot=_dot):
    ah, al = a
    bh, bl = b
    return dot(ah, bh) + (dot(ah, bl) + dot(al, bh))


def _cast_kernel(w_ref, o_ref):
    o_ref[...] = w_ref[...].astype(BF16)


def _cast(src, lead, rows, width):
    tr = _pick(rows, (256, 128, 64, 32, 16, 8))
    return pl.pallas_call(
        _cast_kernel,
        grid=(rows // tr,),
        in_specs=[pl.BlockSpec((None,) * len(lead) + (tr, width), lambda i: lead + (i, 0))],
        out_specs=pl.BlockSpec((tr, width), lambda i: (i, 0)),
        out_shape=jax.ShapeDtypeStruct((rows, width), BF16),
        compiler_params=_cparams(("parallel",)),
        name="cast",
    )(src)


def _cast_ffn_in_kernel(a_ref, b_ref, oa_ref, ob_ref):
    for src, dst in ((a_ref, oa_ref), (b_ref, ob_ref)):
        dst[:, 0:D_FF] = src[...].astype(BF16)
        dst[:, D_FF:D_FF_PAD] = jnp.zeros((dst.shape[0], D_FF_PAD - D_FF), BF16)


def _cast_ffn_in(ffn_in, l, i):
    tr = 128
    half = lambda k: pl.BlockSpec((None, None, tr, D_FF), lambda r: (l, i, r, k))
    out = pl.BlockSpec((tr, D_FF_PAD), lambda r: (r, 0))
    return pl.pallas_call(
        _cast_ffn_in_kernel,
        grid=(D_MODEL // tr,),
        in_specs=[half(0), half(1)],
        out_specs=[out, out],
        out_shape=[jax.ShapeDtypeStruct((D_MODEL, D_FF_PAD), BF16)] * 2,
        compiler_params=_cparams(("parallel",)),
        name="cast_ffn_in",
    )(ffn_in, ffn_in)


def _cast_ffn_out_kernel(w_ref, o_ref):
    tr = w_ref.shape[0]
    row = lax.broadcasted_iota(jnp.int32, w_ref.shape, 0) + pl.program_id(0) * tr
    o_ref[...] = jnp.where(row < D_FF, w_ref[...], 0.0).astype(BF16)


def _cast_ffn_out(ffn_out, l, i):
    tr = FFN_TF
    return pl.pallas_call(
        _cast_ffn_out_kernel,
        grid=(D_FF_PAD // tr,),
        in_specs=[pl.BlockSpec((None, None, tr, D_MODEL), lambda r: (l, i, r, 0))],
        out_specs=pl.BlockSpec((tr, D_MODEL), lambda r: (r, 0)),
        out_shape=jax.ShapeDtypeStruct((D_FF_PAD, D_MODEL), BF16),
        compiler_params=_cparams(("parallel",)),
        name="cast_ffn_out",
    )(ffn_out)


def _cast_wc_kernel(w_ref, o_ref):
    lane = lax.broadcasted_iota(jnp.int32, w_ref.shape, 1)
    valid = lane < C_COLS - pl.program_id(1) * LORA_PAD
    o_ref[...] = jnp.where(valid, w_ref[...], 0.0).astype(BF16)


def _cast_wc(w_in, l):
    tr = 512
    cb0 = (A_COLS + B_COLS) // LORA_PAD
    return pl.pallas_call(
        _cast_wc_kernel,
        grid=(D_MODEL // tr, C_PAD // LORA_PAD),
        in_specs=[pl.BlockSpec((None, tr, LORA_PAD), lambda r, j: (l, r, cb0 + j))],
        out_specs=pl.BlockSpec((tr, LORA_PAD), lambda r, j: (r, j)),
        out_shape=jax.ShapeDtypeStruct((D_MODEL, C_PAD), BF16),
        compiler_params=_cparams(("parallel", "parallel")),
        name="cast_wc",
    )(w_in)


def _ffn_kernel(x_ref, xb_ref, wa_ref, wb_ref, wd_ref, g_ref, b_ref, o_ref, ob_ref, acc_ref, *, alpha):
    j = pl.program_id(1)

    @pl.when(j == 0)
    def _():
        acc_ref[...] = jnp.zeros_like(acc_ref)

    xb = xb_ref[...]
    ha = _dot(xb, wa_ref[...])
    hb = _dot(xb, wb_ref[...])
    act = (ha * _sigmoid(ha) * hb).astype(BF16)
    acc_ref[...] += _dot(act, wd_ref[...])

    @pl.when(j == pl.num_programs(1) - 1)
    def _():
        y = alpha * x_ref[...] + 0.5 * acc_ref[...]
        out = _ln_rows(y, g_ref[...], b_ref[...], LN_EPS)
        o_ref[...] = out
        ob_ref[...] = out.astype(BF16)


def _ffn(x, xb, wa, wb, wd, g, b, alpha):
    n = x.shape[0]
    tm = _pick(n, (512, 256, 128, 64, 32, 16, 8))
    tf = FFN_TF
    grid = (n // tm, D_FF_PAD // tf)
    row = pl.BlockSpec((tm, D_MODEL), lambda i, j: (i, 0))
    vec = pl.BlockSpec((1, D_MODEL), lambda i, j: (0, 0))
    return pl.pallas_call(
        functools.partial(_ffn_kernel, alpha=alpha),
        grid=grid,
        in_specs=[
            row, row,
            pl.BlockSpec((D_MODEL, tf), lambda i, j: (0, j)),
            pl.BlockSpec((D_MODEL, tf), lambda i, j: (0, j)),
            pl.BlockSpec((tf, D_MODEL), lambda i, j: (j, 0)),
            vec, vec,
        ],
        out_specs=[row, row],
        out_shape=[jax.ShapeDtypeStruct((n, D_MODEL), F32), jax.ShapeDtypeStruct((n, D_MODEL), BF16)],
        scratch_shapes=[pltpu.VMEM((tm, D_MODEL), F32)],
        compiler_params=_cparams(("parallel", "arbitrary")),
        name="ffn",
    )(x, xb, wa, wb, wd, g, b)


def _proj_kernel(x_ref, w_ref, o_ref):
    o_ref[...] = _dot(x_ref[...], w_ref[...])


def _proj(xb, w, tn, name, lead=(), m=None):
    n = xb.shape[0]
    m = w.shape[-1] if m is None else m
    tm = _pick(n, (1024, 512, 256, 128, 64, 32, 16, 8))
    return pl.pallas_call(
        _proj_kernel,
        grid=(n // tm, m // tn),
        in_specs=[
            pl.BlockSpec((tm, D_MODEL), lambda i, j: (i, 0)),
            pl.BlockSpec((None,) * len(lead) + (D_MODEL, tn), lambda i, j: lead + (0, j)),
        ],
        out_specs=pl.BlockSpec((tm, tn), lambda i, j: (i, j)),
        out_shape=jax.ShapeDtypeStruct((n, m), F32),
        compiler_params=_cparams(("parallel", "arbitrary")),
        name=name,
    )(xb, w)


def _rope_tables(pos, rot, theta, width):
    half = rot // 2
    freq = jnp.power(theta, -jnp.arange(half, dtype=F32) / half)
    ang = pos.astype(F32)[:, None] * freq[None, :]
    cos, sin = jnp.cos(ang), jnp.sin(ang)
    n = pos.shape[0]
    pad = width - rot
    c = jnp.concatenate([cos, cos, jnp.ones((n, pad), F32)], axis=-1)
    s_lo = jnp.concatenate([-sin, jnp.zeros((n, half + pad), F32)], axis=-1)
    s_hi = jnp.concatenate([jnp.zeros((n, half), F32), sin, jnp.zeros((n, pad), F32)], axis=-1)
    rep = LANE // width
    return tuple(jnp.tile(t, (1, rep)) for t in (c, s_lo, s_hi))


def _rope_kernel(xa_ref, xv_ref, xb_ref, ca_ref, la_ref, ha_ref, cb_ref, lb_ref, hb_ref,
                 qa_buf, kall_buf, vall_buf, qkb_buf, qa_ref, kall_ref, vall_ref, qkb_ref, kd_ref, vd_ref):
    del qa_buf, kall_buf, vall_buf, qkb_buf

    def rot(x, c, lo, hi, half):
        return x * c + pltpu.roll(x, LANE - half, 1) * lo + pltpu.roll(x, half, 1) * hi

    ca, la, ha = ca_ref[...], la_ref[...], ha_ref[...]
    nqa = HA * 2 * DA // LANE
    tm = xa_ref.shape[0]
    ks = []
    for t in range(2 * nqa):
        y = rot(xa_ref[:, t * LANE:(t + 1) * LANE], ca, la, ha, ROT_A // 2)
        if t < nqa:
            qa_ref[:, t * LANE:(t + 1) * LANE] = (y * (DA ** -0.5)).astype(BF16)
        else:
            ks.append(y)
            kd_ref[:, (t - nqa) * LANE:(t - nqa + 1) * LANE] = y.astype(BF16)
    kall_ref[...] = jnp.concatenate(ks, axis=1).reshape(tm, HA, 2 * DA)
    v = xv_ref[...]
    vall_ref[...] = v.reshape(tm, HA, DVA)
    vd_ref[...] = v.astype(BF16)
    cb, lb, hb = cb_ref[...], lb_ref[...], hb_ref[...]
    nqb = HB * DKB // LANE
    for t in range(2 * nqb):
        sl = slice(t * LANE, (t + 1) * LANE)
        y = rot(xb_ref[:, sl], cb, lb, hb, DKB // 2)
        if t >= nqb:
            y = y * (DKB ** -0.5)
        qkb_ref[:, sl] = y


def _rope(pab, tabs_a, tabs_b, qa_buf, kall_buf, vall_buf, qkb_buf, row0, nrows, kv_row0):
    tm = _pick(nrows, (256, 128, 64, 32, 16, 8))
    rb0, kb0 = row0 // tm, kv_row0 // tm
    wqk = 2 * HA * 2 * DA
    wb = 2 * HB * DKB
    tab = pl.BlockSpec((tm, LANE), lambda i: (rb0 + i, 0))
    anyspec = pl.BlockSpec(memory_space=pl.ANY)
    bufs = (qa_buf, kall_buf, vall_buf, qkb_buf)
    cache = pl.BlockSpec((tm, HA, DVA), lambda i: (kb0 + i, 0, 0))
    dense = pl.BlockSpec((tm, WA), lambda i: (i, 0))
    return pl.pallas_call(
        _rope_kernel,
        grid=(nrows // tm,),
        in_specs=[
            pl.BlockSpec((tm, wqk), lambda i: (rb0 + i, 0)),
            pl.BlockSpec((tm, WA), lambda i: (rb0 + i, wqk // WA)),
            pl.BlockSpec((tm, wb), lambda i: (rb0 + i, A_COLS // wb)),
            tab, tab, tab, tab, tab, tab,
            anyspec, anyspec, anyspec, anyspec,
        ],
        out_specs=[
            pl.BlockSpec((tm, WA), lambda i: (rb0 + i, 0)),
            cache, cache,
            pl.BlockSpec((tm, wb), lambda i: (rb0 + i, 0)),
            dense, dense,
        ],
        out_shape=[jax.ShapeDtypeStruct(b.shape, b.dtype) for b in bufs]
                  + [jax.ShapeDtypeStruct((nrows, WA), BF16)] * 2,
        input_output_aliases={9: 0, 10: 1, 11: 2, 12: 3},
        compiler_params=_cparams(("parallel",)),
        name="rope",
    )(pab, pab, pab, *tabs_a, *tabs_b, *bufs)


def _diff_lambda(dl_ref, lam_init):
    lv = dl_ref[...]
    s1 = jnp.sum(lv[0:1] * lv[1:2], axis=-1, keepdims=True)
    s2 = jnp.sum(lv[2:3] * lv[3:4], axis=-1, keepdims=True)
    return jnp.exp(s1) - jnp.exp(s2) + lam_init


def _split_maps(q):
    lane = lax.broadcasted_iota(jnp.int32, q.shape, 1)
    zero = jnp.zeros_like(q)
    return jnp.where(lane < DA, q, zero), jnp.where(lane >= DA, q, zero)


def _subln(o, g, scale):
    ms = jnp.mean(o * o, axis=-1, keepdims=True)
    return o * lax.rsqrt(ms + SUBLN_EPS) * g * scale


def _attn_prompt_kernel(q_ref, kb_ref, v_ref, dl_ref, g_ref, o_buf, o_ref, vb_ref, s_ref,
                        m_ref, acc_ref, *, tq, tk, nh, lam_init):
    del o_buf
    qi = pl.program_id(2)
    hsl = [slice(h * 2 * DA, (h + 1) * 2 * DA) for h in range(nh)]
    chains = range(2 * nh)

    @pl.when(qi == 0)
    def _():
        for h in range(nh):
            vb_ref[:, 2 * h * DVA:(2 * h + 1) * DVA] = v_ref[:, hsl[h]]
            vb_ref[:, (2 * h + 1) * DVA:(2 * h + 2) * DVA] = jnp.ones((v_ref.shape[0], DVA), BF16)

    lam = _diff_lambda(dl_ref, lam_init)
    qs = [qm for h in range(nh) for qm in _split_maps(q_ref[:, hsl[h]])]

    def scores(kb, slot):
        rows = pl.ds(pl.multiple_of(kb * tk, tk), tk)
        for c in chains:
            s_ref[slot, c] = _dot_nt(qs[c], kb_ref[rows, hsl[c // 2]])

    def consume(ss, start, width):
        stats = []
        for c in chains:
            m = m_ref[c]
            m_new = jnp.maximum(m, jnp.max(ss[c], axis=-1, keepdims=True))
            m_ref[c] = m_new
            stats.append((jnp.exp(m - m_new), jnp.exp(ss[c] - m_new).astype(BF16)))
        pvs = [_dot(stats[c][1], vb_ref[pl.ds(start, width), (c // 2) * 2 * DVA:(c // 2 + 1) * 2 * DVA])
               for c in chains]
        for c in chains:
            acc_ref[c] = stats[c][0] * acc_ref[c] + pvs[c]

    def full_step(kb, slot):
        scores(kb + 1, 1 - slot)
        consume([s_ref[slot, c] for c in chains], pl.multiple_of(kb * tk, tk), tk)

    q0 = qi * tq
    n_full = q0 // tk
    m_ref[...] = jnp.full(m_ref.shape, -1e30, F32)
    acc_ref[...] = jnp.zeros_like(acc_ref)
    scores(0, 0)

    def two_steps(j, _):
        full_step(2 * j, 0)
        full_step(2 * j + 1, 1)
        return 0

    lax.fori_loop(0, n_full // 2, two_steps, 0)

    @pl.when(n_full % 2 == 1)
    def _():
        full_step(n_full - 1, 0)

    tail = pl.multiple_of(n_full * tk, tk)
    g = g_ref[...]

    for r in range(tk // tq):
        @pl.when(q0 - tail == r * tq)
        def _():
            width = (r + 1) * tq
            row = lax.broadcasted_iota(jnp.int32, (tq, width), 0) // CHUNK
            col = lax.broadcasted_iota(jnp.int32, (tq, width), 1) // CHUNK
            ok = col - row <= r * (tq // CHUNK)
            consume([jnp.where(ok, s_ref[n_full % 2, c, :, 0:width], -1e30) for c in chains], tail, width)
            for h in range(nh):
                a1, a2 = acc_ref[2 * h], acc_ref[2 * h + 1]
                o = a1[:, 0:DVA] / a1[:, DVA:2 * DVA] - lam * (a2[:, 0:DVA] / a2[:, DVA:2 * DVA])
                o_ref[:, hsl[h]] = _subln(o, g, 1.0 - lam_init).astype(BF16)


def _attn_prompt(qa, kd, vd, dl, g, o_buf, nb, t, lam_init):
    tq = _pick(t, (512, 256, 128, 64))
    tk = _pick(t, (1024, 512, 256, 128, 64))
    nh = ATTN_HEADS_PER_STEP
    nq = t // tq
    kv = pl.BlockSpec((t, nh * 2 * DA), lambda b, h, i: (b, h))
    return pl.pallas_call(
        functools.partial(_attn_prompt_kernel, tq=tq, tk=tk, nh=nh, lam_init=lam_init),
        grid=(nb, HA // nh, nq),
        in_specs=[
            pl.BlockSpec((tq, nh * 2 * DA), lambda b, h, i: (b * nq + i, h)),
            kv, kv,
            pl.BlockSpec((4, DA), lambda b, h, i: (0, 0)),
            pl.BlockSpec((1, DVA), lambda b, h, i: (0, 0)),
            pl.BlockSpec(memory_space=pl.ANY),
        ],
        out_specs=pl.BlockSpec((tq, nh * DVA), lambda b, h, i: (b * nq + i, h)),
        out_shape=jax.ShapeDtypeStruct(o_buf.shape, o_buf.dtype),
        input_output_aliases={5: 0},
        scratch_shapes=[pltpu.VMEM((t, nh * 2 * DVA), BF16),
                        pltpu.VMEM((2, 2 * nh, tq, tk), F32),
                        pltpu.VMEM((2 * nh, tq, 1), F32), pltpu.VMEM((2 * nh, tq, 2 * DVA), F32)],
        compiler_params=_cparams(("parallel", "parallel", "arbitrary")),
        name="attn_prompt",
    )(qa, kd, vd, dl, g, o_buf)


def _attn_sample_kernel(q_ref, kn_ref, vn_ref, ck_ref, cv_ref, dl_ref, g_ref, o_buf, o_ref, *,
                        s_len, lam_init):
    del o_buf
    lam = _diff_lambda(dl_ref, lam_init)
    g = g_ref[...]
    p_len = ck_ref.shape[0]
    kc_all = ck_ref[...].reshape(p_len, WA).astype(BF16)
    vc_all = cv_ref[...].reshape(p_len, WA).astype(BF16)
    heads = range(HA)
    sls = [slice(h * 2 * DA, (h + 1) * 2 * DA) for h in heads]
    qq = [jnp.concatenate(_split_maps(q_ref[:, sl]), axis=0) for sl in sls]
    s_c = [_dot_nt(qq[h], kc_all[:, sls[h]]) for h in heads]
    s_n = [_dot_nt(qq[h], kn_ref[:, sls[h]]) for h in heads]
    p_c, p_n, den = [], [], []
    for h in heads:
        m = jnp.maximum(jnp.max(s_c[h], axis=-1, keepdims=True), jnp.max(s_n[h], axis=-1, keepdims=True))
        e_c = jnp.exp(s_c[h] - m)
        e_n = jnp.exp(s_n[h] - m)
        den.append(jnp.sum(e_c, axis=-1, keepdims=True) + jnp.sum(e_n, axis=-1, keepdims=True))
        p_c.append(e_c.astype(BF16))
        p_n.append(e_n.astype(BF16))
    pv = [_dot(p_c[h], vc_all[:, sls[h]]) + _dot(p_n[h], vn_ref[:, sls[h]]) for h in heads]
    for h in heads:
        o = pv[h] / den[h]
        od = o[:s_len] - lam * o[s_len:]
        o_ref[:, sls[h]] = _subln(od, g, 1.0 - lam_init).astype(BF16)


def _attn_sample(qa, kd, vd, ck, cv, dl, g, o_buf, layer, nbs, s_len, row0, lam_init):
    p_len = ck.shape[2]
    rb0 = row0 // s_len
    new = pl.BlockSpec((s_len, WA), lambda b: (b, 0))
    cache = pl.BlockSpec((None, None, p_len, HA, DVA), lambda b: (layer, b, 0, 0, 0))
    return pl.pallas_call(
        functools.partial(_attn_sample_kernel, s_len=s_len, lam_init=lam_init),
        grid=(nbs,),
        in_specs=[
            pl.BlockSpec((s_len, WA), lambda b: (rb0 + b, 0)),
            new, new, cache, cache,
            pl.BlockSpec((4, DA), lambda b: (0, 0)),
            pl.BlockSpec((1, DVA), lambda b: (0, 0)),
            pl.BlockSpec(memory_space=pl.ANY),
        ],
        out_specs=pl.BlockSpec((s_len, WA), lambda b: (rb0 + b, 0)),
        out_shape=jax.ShapeDtypeStruct(o_buf.shape, o_buf.dtype),
        input_output_aliases={7: 0},
        compiler_params=_cparams(("parallel",)),
        name="attn_sample",
    )(qa, kd, vd, ck, cv, dl, g, o_buf)


def _ret_kernel(q_ref, k_ref, v_ref, gate_ref, r0_ref, o_buf, o_ref, rout_ref, st_ref, dec_ref, *, blk):
    del o_buf
    c = pl.program_id(1)

    @pl.when(c == 0)
    def _():
        st_ref[...] = jnp.zeros_like(st_ref)
        for h in range(HB):
            e = h % 2
            st_ref[h, e * DKB:(e + 1) * DKB, :] = r0_ref[h]
        n_i = lax.broadcasted_iota(jnp.int32, (blk, blk), 0)
        m_i = lax.broadcasted_iota(jnp.int32, (blk, blk), 1)
        diff = jnp.maximum(n_i - m_i, 0).astype(F32)
        for h in range(HB):
            dec_ref[h] = jnp.where(n_i >= m_i, jnp.exp(diff * math.log1p(-2.0 ** (-5.0 - h))), 0.0)

    n_col = lax.broadcasted_iota(jnp.int32, (blk, LANE), 0).astype(F32)
    lane = lax.broadcasted_iota(jnp.int32, (blk, LANE), 1)
    heads = range(HB)
    log_g = [math.log1p(-2.0 ** (-5.0 - h)) for h in heads]
    vsl = [slice(h * DVB, (h + 1) * DVB) for h in heads]
    qb, kb, kd, v = [], [], [], []
    for h in heads:
        p, e = h // 2, h % 2
        sl = slice(p * LANE, (p + 1) * LANE)
        mine = (lane >= DKB) if e else (lane < DKB)
        k = jnp.where(mine, k_ref[:, sl], 0.0)
        qb.append(jnp.where(mine, q_ref[:, sl], 0.0).astype(BF16))
        kb.append(k.astype(BF16))
        kd.append((k * jnp.exp((blk - 1.0 - n_col) * log_g[h])).astype(BF16))
        v.append(v_ref[:, vsl[h]].astype(BF16))
    r_old = [st_ref[h] for h in heads]
    s = [_dot_nt(qb[h], kb[h]) for h in heads]
    cross = [_dot(qb[h], r_old[h].astype(BF16)) for h in heads]
    grow = [_dot_tn(kd[h], v[h]) for h in heads]
    sd = [(s[h] * dec_ref[h]).astype(BF16) for h in heads]
    intra = [_dot(sd[h], v[h]) for h in heads]
    for h in heads:
        st_ref[h] = math.exp(blk * log_g[h]) * r_old[h] + grow[h]
        o = intra[h] + cross[h] * jnp.exp((n_col + 1.0) * log_g[h])
        ms = jnp.mean(o * o, axis=-1, keepdims=True)
        gt = gate_ref[:, vsl[h]]
        o_ref[:, vsl[h]] = (o * lax.rsqrt(ms + RET_EPS) * (gt * _sigmoid(gt))).astype(BF16)

    @pl.when(c == pl.num_programs(1) - 1)
    def _():
        for h in range(HB):
            e = h % 2
            rout_ref[h] = st_ref[h, e * DKB:(e + 1) * DKB, :]


def _retention(qkb, pab, r0, o_buf, nseg, seg_len, row0, blk):
    nc = seg_len // blk
    rb0 = row0 // blk
    vcol = (A_COLS + 2 * HB * DKB) // WB
    row = lambda b, c: rb0 + b * nc + c
    return pl.pallas_call(
        functools.partial(_ret_kernel, blk=blk),
        grid=(nseg, nc),
        in_specs=[
            pl.BlockSpec((blk, HB * DKB), lambda b, c: (row(b, c), 0)),
            pl.BlockSpec((blk, HB * DKB), lambda b, c: (row(b, c), 1)),
            pl.BlockSpec((blk, WB), lambda b, c: (row(b, c), vcol)),
            pl.BlockSpec((blk, WB), lambda b, c: (row(b, c), vcol + 1)),
            pl.BlockSpec((None, HB, DKB, DVB), lambda b, c: (b, 0, 0, 0)),
            pl.BlockSpec(memory_space=pl.ANY),
        ],
        out_specs=[
            pl.BlockSpec((blk, WB), lambda b, c: (row(b, c), 0)),
            pl.BlockSpec((None, HB, DKB, DVB), lambda b, c: (b, 0, 0, 0)),
        ],
        out_shape=[jax.ShapeDtypeStruct(o_buf.shape, o_buf.dtype),
                   jax.ShapeDtypeStruct((nseg, HB, DKB, DVB), F32)],
        input_output_aliases={5: 0},
        scratch_shapes=[pltpu.VMEM((HB, 2 * DKB, DVB), F32), pltpu.VMEM((HB, blk, blk), F32)],
        compiler_params=_cparams(("parallel", "arbitrary")),
        name="retention",
    )(qkb, qkb, pab, pab, r0, o_buf)


def _head_sum(x, ones_bd):
    hi, lo = _split2(x)
    return _dot(hi, ones_bd) + _dot(lo, ones_bd)


def _wkv_kernel(*refs, ck, nsb):
    pc_refs = refs[:nsb]
    (prev_ref, s0_ref, mu_ref, w0_ref, a0_ref, kk_ref, ka_ref, rk_ref, w2_ref, a2_ref, g2_ref,
     lng_ref, lnb_ref, tri_ref, ones_ref, o_ref, sout_ref, last_ref, st_ref, carry_ref) = refs[nsb:]
    c = pl.program_id(1)

    @pl.when(c == 0)
    def _():
        zero = jnp.zeros((NC, NC), F32)
        for g in range(nsb):
            for p in range(HC // 2):
                top = jnp.concatenate([s0_ref[g, 2 * p], zero], axis=1)
                bot = jnp.concatenate([zero, s0_ref[g, 2 * p + 1]], axis=1)
                st_ref[g, p] = jnp.concatenate([top, bot], axis=0)
        carry_ref[...] = prev_ref[...]

    row = lax.broadcasted_iota(jnp.int32, (ck, C_PAD), 0)
    pcs, shifted = [], []
    for g, pc_ref in enumerate(pc_refs):
        pc = pc_ref[...]
        pcs.append(pc)
        shifted.append(jnp.where(row == 0, carry_ref[g], pltpu.roll(pc, 1, 0)))
        carry_ref[g] = pc[ck - 1:ck, :]
        last_ref[g] = pc[ck - 1:ck, :]
    pc_all = jnp.concatenate(pcs, axis=0)
    pcm = pc_all + mu_ref[...] * (jnp.concatenate(shifted, axis=0) - pc_all)
    rc = pcm[:, 0:WC]
    kc = pcm[:, WC:2 * WC]
    vc = pcm[:, 2 * WC:3 * WC]
    lora = pcm[:, 3 * WC:C_PAD]
    z = w0_ref[...] + _dot(jnp.tanh(lora).astype(BF16), w2_ref[...])
    u = -z
    softplus = jnp.maximum(u, 0.0) + jnp.log1p(jnp.exp(-jnp.abs(u)))
    lw = -jnp.exp(-softplus - 0.5)
    a_sig = _sigmoid(a0_ref[...] + _dot(lora.astype(BF16), a2_ref[...]))
    gate = _dot(_sigmoid(lora).astype(BF16), g2_ref[...])
    kk = kc * kk_ref[...]
    kc2 = kc * (1.0 + (a_sig - 1.0) * ka_ref[...])
    rkr = rc * kc2 * rk_ref[...]
    ones_bd = ones_ref[...]

    tri = tri_ref[...]
    l1, l2, l3 = _split3(lw)
    cum = _dot(tri, l1) + (_dot(tri, l2) + _dot(tri, l3))
    total = jnp.concatenate(
        [jnp.broadcast_to(cum[(g + 1) * ck - 1:(g + 1) * ck, :], (ck, WC)) for g in range(nsb)], axis=0)
    e_neg = jnp.exp(-cum)
    e_rem = jnp.exp(total - cum)
    e_exc = jnp.exp(cum - lw)
    r_t = rc * jnp.exp(cum)
    k_t = kc2 * e_neg
    k_p = kc2 * e_rem
    p_tot = jnp.exp(total)

    lane = lax.broadcasted_iota(jnp.int32, (ck, LANE), 1)
    m0 = lane < NC

    def stack(x):
        return jnp.concatenate([jnp.where(m0, x, 0.0), jnp.where(m0, 0.0, x)], axis=0)

    n2 = 2 * ck
    ri = lax.broadcasted_iota(jnp.int32, (n2, n2), 0)
    ci = lax.broadcasted_iota(jnp.int32, (n2, n2), 1)
    same = (ri // ck) == (ci // ck)
    strict = jnp.logical_and(same, ri > ci)
    incl = jnp.logical_and(same, ri >= ci)

    bf = lambda t: t.astype(BF16)
    pairs = range(HC // 2)
    chains = [(g, p) for g in range(nsb) for p in pairs]
    a_s, r_s, b_s, k_s, v_s, bp_s, kp_s, bonus = {}, {}, {}, {}, {}, {}, {}, {}
    for p in pairs:
        sl = slice(p * LANE, (p + 1) * LANE)
        kkt = kk[:, sl]
        kn = kkt / jnp.maximum(jnp.sqrt(_head_sum(kkt * kkt, ones_bd)), 1e-12)
        bn = kn * a_sig[:, sl]
        a_tp = -kn * e_exc[:, sl]
        b_tp = bn * e_neg[:, sl]
        b_pp = bn * e_rem[:, sl]
        bonus_p = _head_sum(rkr[:, sl], ones_bd) * vc[:, sl]
        for g in range(nsb):
            rows = slice(g * ck, (g + 1) * ck)
            a_s[g, p] = bf(stack(a_tp[rows]))
            r_s[g, p] = bf(stack(r_t[rows, sl]))
            b_s[g, p] = bf(stack(b_tp[rows]))
            k_s[g, p] = bf(stack(k_t[rows, sl]))
            v_s[g, p] = _split2(stack(vc[rows, sl]))
            bp_s[g, p] = _split2(stack(b_pp[rows]))
            kp_s[g, p] = _split2(stack(k_p[rows, sl]))
            bonus[g, p] = bonus_p[rows]

    s_old = {gp: st_ref[gp[0], gp[1]] for gp in chains}
    s_b = {gp: bf(s_old[gp]) for gp in chains}
    if n2 % LANE == 0:
        big = {gp: _dot_nt(jnp.concatenate([a_s[gp], r_s[gp]], axis=0),
                           jnp.concatenate([b_s[gp], k_s[gp]], axis=0)) for gp in chains}
        ab = {gp: big[gp][0:n2, 0:n2] for gp in chains}
        ak = {gp: big[gp][0:n2, n2:2 * n2] for gp in chains}
        rb = {gp: big[gp][n2:2 * n2, 0:n2] for gp in chains}
        rk = {gp: big[gp][n2:2 * n2, n2:2 * n2] for gp in chains}
    else:
        ab = {gp: _dot_nt(a_s[gp], b_s[gp]) for gp in chains}
        ak = {gp: _dot_nt(a_s[gp], k_s[gp]) for gp in chains}
        rb = {gp: _dot_nt(r_s[gp], b_s[gp]) for gp in chains}
        rk = {gp: _dot_nt(r_s[gp], k_s[gp]) for gp in chains}
    lk = {gp: bf(jnp.where(strict, ab[gp], 0.0)) for gp in chains}
    l_ak = {gp: bf(jnp.where(strict, ak[gp], 0.0)) for gp in chains}
    l_rb = {gp: bf(jnp.where(incl, rb[gp], 0.0)) for gp in chains}
    l_rk = {gp: bf(jnp.where(incl, rk[gp], 0.0)) for gp in chains}
    if n2 % LANE == 0:
        sv = {gp: jnp.concatenate([bf(s_old[gp].T), v_s[gp][0]], axis=0) for gp in chains}
        x = {gp: _dot(jnp.concatenate([a_s[gp], l_ak[gp]], axis=1), sv[gp]) for gp in chains}
        y_part = {gp: _dot(jnp.concatenate([r_s[gp], l_rk[gp]], axis=1), sv[gp]) for gp in chains}
    else:
        x = {gp: _dot_nt(a_s[gp], s_b[gp]) + _dot(l_ak[gp], v_s[gp][0]) for gp in chains}
        y_part = {gp: _dot_nt(r_s[gp], s_b[gp]) + _dot(l_rk[gp], v_s[gp][0]) for gp in chains}
    span = 1
    while span < ck:
        x = {gp: x[gp] + _dot(lk[gp], bf(x[gp])) for gp in chains}
        span *= 2
        if span < ck:
            lk = {gp: bf(_dot(lk[gp], lk[gp])) for gp in chains}
    x_sp = {gp: _split2(x[gp]) for gp in chains}
    y_st = {gp: y_part[gp] + _dot(l_rb[gp], x_sp[gp][0]) for gp in chains}
    for g, p in chains:
        sl = slice(p * LANE, (p + 1) * LANE)
        decay = p_tot[(g + 1) * ck - 1:(g + 1) * ck, sl]
        uv = tuple(jnp.concatenate([xs, vs], axis=0) for xs, vs in zip(x_sp[g, p], v_s[g, p]))
        bk = tuple(jnp.concatenate([bs, ks], axis=0) for bs, ks in zip(bp_s[g, p], kp_s[g, p]))
        st_ref[g, p] = s_old[g, p] * decay + _mm3(uv, bk, _dot_tn)
    for g, p in chains:
        sl = slice(p * LANE, (p + 1) * LANE)
        rows = slice(g * ck, (g + 1) * ck)
        y = y_st[g, p][0:ck] + y_st[g, p][ck:n2]
        mean = _head_sum(y, ones_bd) * (1.0 / NC)
        d = y - mean
        var = _head_sum(d * d, ones_bd) * (1.0 / NC)
        yn = d * lax.rsqrt(var + GN_EPS_C) * lng_ref[:, sl] + lnb_ref[:, sl]
        o_ref[g, :, sl] = ((yn + bonus[g, p]) * gate[rows, sl]).astype(BF16)

    @pl.when(c == pl.num_programs(1) - 1)
    def _():
        for g in range(nsb):
            for p in range(HC // 2):
                s_pair = st_ref[g, p]
                sout_ref[g, 2 * p] = s_pair[0:NC, 0:NC]
                sout_ref[g, 2 * p + 1] = s_pair[NC:2 * NC, NC:2 * NC]


def _wkv(pc, prev, s0, vecs, mats, lng, lnb, ones_bd, nseg, seg_len, row0, ck, nsb):
    nc = seg_len // ck
    rb0 = row0 // ck
    pos = jnp.arange(nsb * ck)
    tri = ((pos[:, None] // ck == pos[None, :] // ck) & (pos[:, None] >= pos[None, :])).astype(BF16)
    pc_spec = lambda g: pl.BlockSpec((ck, C_PAD), lambda b, c: (rb0 + (b * nsb + g) * nc + c, 0))
    vec = lambda w: pl.BlockSpec((1, w), lambda b, c: (0, 0))
    mat = pl.BlockSpec((LORA_PAD, WC), lambda b, c: (0, 0))
    shift = pl.BlockSpec((nsb, 1, C_PAD), lambda b, c: (b, 0, 0))
    st = pl.BlockSpec((nsb, HC, NC, NC), lambda b, c: (b, 0, 0, 0))
    return pl.pallas_call(
        functools.partial(_wkv_kernel, ck=ck, nsb=nsb),
        grid=(nseg // nsb, nc),
        in_specs=[pc_spec(g) for g in range(nsb)] + [
            shift, st,
            vec(C_PAD), vec(WC), vec(WC), vec(WC), vec(WC), vec(WC),
            mat, mat, mat, vec(WC), vec(WC),
            pl.BlockSpec((nsb * ck, nsb * ck), lambda b, c: (0, 0)),
            pl.BlockSpec((LANE, LANE), lambda b, c: (0, 0)),
        ],
        out_specs=[pl.BlockSpec((nsb, ck, WC), lambda b, c: (b, c, 0)), st, shift],
        out_shape=[jax.ShapeDtypeStruct((nseg, seg_len, WC), BF16),
                   jax.ShapeDtypeStruct((nseg, HC, NC, NC), F32),
                   jax.ShapeDtypeStruct((nseg, 1, C_PAD), F32)],
        scratch_shapes=[pltpu.VMEM((nsb, HC // 2, LANE, LANE), F32), pltpu.VMEM((nsb, 1, C_PAD), F32)],
        compiler_params=_cparams(("parallel", "arbitrary")),
        name="wkv",
    )(*([pc] * nsb), prev, s0, *vecs, *mats, lng, lnb, tri, ones_bd)


def _merge_kernel(xb_ref, oa_ref, ob_ref, oc_ref, ga_ref, gb_ref, gc_ref, wa_ref, wb_ref, wc_ref, o_ref):
    xb = xb_ref[...]
    zs = [_dot(xb, g_ref[...]) for g_ref in (ga_ref, gb_ref, gc_ref)]
    ys = [_dot(b_ref[...], w_ref[...])
          for b_ref, w_ref in ((oa_ref, wa_ref), (ob_ref, wb_ref), (oc_ref, wc_ref))]
    o_ref[...] = (_sigmoid(zs[0]) * ys[0] + _sigmoid(zs[1]) * ys[1]) + _sigmoid(zs[2]) * ys[2]


def _merge(xb, oa, ob, oc, w_gate, wbr):
    n = oa.shape[0]
    tm = _pick(n, (1024, 512, 256, 128, 64, 32, 16, 8))
    tn = 512
    nj = D_MODEL // tn
    bw = oa.shape[1]
    o_spec = pl.BlockSpec((tm, bw), lambda i, j: (i, 0))
    g_spec = lambda k: pl.BlockSpec((D_MODEL, tn), lambda i, j: (0, k * nj + j))
    w_spec = pl.BlockSpec((bw, tn), lambda i, j: (0, j))
    return pl.pallas_call(
        _merge_kernel,
        grid=(n // tm, nj),
        in_specs=[pl.BlockSpec((tm, D_MODEL), lambda i, j: (i, 0)), o_spec, o_spec, o_spec,
                  g_spec(0), g_spec(1), g_spec(2), w_spec, w_spec, w_spec],
        out_specs=pl.BlockSpec((tm, tn), lambda i, j: (i, j)),
        out_shape=jax.ShapeDtypeStruct((n, D_MODEL), F32),
        compiler_params=_cparams(("parallel", "arbitrary")),
        name="merge",
    )(xb, oa, ob, oc, w_gate, w_gate, w_gate, *wbr)


def _outproj_kernel(m_ref, x_ref, w_ref, g_ref, b_ref, o_ref, ob_ref, *, alpha):
    half = m_ref.shape[0] // 2
    for r in (slice(0, half), slice(half, 2 * half)):
        y = alpha * x_ref[r, :] + _dot(m_ref[r, :].astype(BF16), w_ref[...])
        out = _ln_rows(y, g_ref[...], b_ref[...], LN_EPS)
        o_ref[r, :] = out
        ob_ref[r, :] = out.astype(BF16)


def _outproj(merged, x, w, g, b, alpha):
    n = x.shape[0]
    tm = _pick(n, (512, 256, 128, 64, 32))
    row = pl.BlockSpec((tm, D_MODEL), lambda i: (i, 0))
    vec = pl.BlockSpec((1, D_MODEL), lambda i: (0, 0))
    return pl.pallas_call(
        functools.partial(_outproj_kernel, alpha=alpha),
        grid=(n // tm,),
        in_specs=[row, row, pl.BlockSpec((D_MODEL, D_MODEL), lambda i: (0, 0)), vec, vec],
        out_specs=[row, row],
        out_shape=[jax.ShapeDtypeStruct((n, D_MODEL), F32), jax.ShapeDtypeStruct((n, D_MODEL), BF16)],
        compiler_params=_cparams(("parallel",)),
        name="outproj",
    )(merged, x, w, g, b)


def kernel(x_prompt, x_sample, cache_k, cache_v, state_ret, state_wkv, state_shift, w_in, w_gate,
           w_branch, w_out, ffn_in, ffn_out, ln_g, ln_b, diff_lambda, subln_g, mu_c, w0, w2, a0, a2,
           g2, k_k, k_a, r_k, lnx_g, lnx_b):
    nb, t, _ = x_prompt.shape
    nbs, s_len, _ = x_sample.shape
    depth = w_in.shape[0]
    p_len = cache_k.shape[2]
    n_p = nb * t
    n_s = nbs * s_len
    n = n_p + n_s
    alpha = (2 * depth) ** 0.25

    x = jnp.concatenate([x_prompt.reshape(n_p, D_MODEL), x_sample.reshape(n_s, D_MODEL)], axis=0)
    xb = x.astype(BF16)

    pos = jnp.concatenate([jnp.tile(jnp.arange(t, dtype=jnp.int32), nb),
                           jnp.tile(p_len + jnp.arange(s_len, dtype=jnp.int32), nbs)])
    tabs_a = _rope_tables(pos, ROT_A, ROPE_THETA, DA)
    tabs_b = _rope_tables(pos, DKB, RET_THETA, DKB)

    ck_all, cv_all = cache_k, cache_v

    eye = jnp.arange(LANE) // NC
    ones_bd = (eye[:, None] == eye[None, :]).astype(BF16)
    ck_p = _pick(t, (CHUNK,))
    ck_s = s_len
    nsb_p = _pick(nb, (2, 1))
    nsb_s = _pick(nbs, (4, 2, 1))
    ret_blk = _pick(t, (256, 128, 64))

    outs = {k: [] for k in ("rp", "wp", "sp", "rs", "ws", "ss")}

    w_in_b = w_in.astype(BF16)

    qa = jnp.zeros((n, WA), BF16)
    qkb = jnp.zeros((n, 2 * HB * DKB), F32)
    k_p, v_p = (jnp.zeros((depth * n_p, HA, DVA), F32) for _ in range(2))
    k_s, v_s = (jnp.zeros((depth * n_s, HA, DVA), F32) for _ in range(2))
    oa, ob = jnp.zeros((n, WA), BF16), jnp.zeros((n, WB), BF16)

    for l in range(depth):
        lam_init = 0.8 - 0.6 * math.exp(-0.3 * l)

        def ffn_weights(i):
            return (*_cast_ffn_in(ffn_in, l, i), _cast_ffn_out(ffn_out, l, i))

        x, xb = _ffn(x, xb, *ffn_weights(0), ln_g[l, 0][None], ln_b[l, 0][None], alpha)

        w_c = _cast_wc(w_in_b, l)
        pab = _proj(xb, w_in_b, 1024, name="proj_ab", lead=(l,), m=A_COLS + B_COLS)
        pc = _proj(xb, w_c, C_PAD // 3, name="proj_c")
        w_g = _cast(w_gate, (l,), D_MODEL, N_BRANCH * D_MODEL)
        qa, k_p, v_p, qkb, kd_p, vd_p = _rope(pab, tabs_a, tabs_b, qa, k_p, v_p, qkb, 0, n_p, l * n_p)
        qa, k_s, v_s, qkb, kd_s, vd_s = _rope(pab, tabs_a, tabs_b, qa, k_s, v_s, qkb, n_p, n_s, l * n_s)

        dl = diff_lambda[l]
        sg = subln_g[l][None]
        oa = _attn_prompt(qa, kd_p, vd_p, dl, sg, oa, nb, t, lam_init)
        oa = _attn_sample(qa, kd_s, vd_s, ck_all, cv_all, dl, sg, oa, l, nbs, s_len, n_p, lam_init)

        ob, ret_p = _retention(qkb, pab, jnp.zeros((nb, HB, DKB, DVB), F32), ob, nb, t, 0, ret_blk)
        ob, ret_s = _retention(qkb, pab, state_ret[l], ob, nbs, s_len, n_p, s_len)

        padc = C_PAD - C_COLS
        vecs = (jnp.pad(mu_c[l], (0, padc))[None], w0[l][None], a0[l][None], k_k[l][None],
                k_a[l][None], r_k[l].reshape(1, WC))
        zrow = lambda r0, r1, m: jnp.pad(m.astype(BF16), ((r0, LORA_PAD - r1), (0, 0)))
        mats = (zrow(0, LORA_W, w2[l]), zrow(LORA_W, LORA_W + LORA_A, a2[l]),
                zrow(LORA_W + LORA_A, LORA_W + LORA_A + LORA_G, g2[l]))
        prev_p = jnp.zeros((nb, 1, C_PAD), F32)
        prev_s = jnp.pad(state_shift[l], ((0, 0), (0, 0), (0, padc)))
        lng, lnb = lnx_g[l][None], lnx_b[l][None]
        oc_p, wkv_p, shift_p = _wkv(pc, prev_p, jnp.zeros((nb, HC, NC, NC), F32), vecs, mats,
                                    lng, lnb, ones_bd, nb, t, 0, ck_p, nsb_p)
        oc_s, wkv_s, shift_s = _wkv(pc, prev_s, state_wkv[l], vecs, mats,
                                    lng, lnb, ones_bd, nbs, s_len, n_p, ck_s, nsb_s)
        oc = jnp.concatenate([oc_p.reshape(n_p, WC), oc_s.reshape(n_s, WC)], axis=0)

        w_br = [_cast(w_branch, (l, k), WA, D_MODEL) for k in range(N_BRANCH)]
        merged = _merge(xb, oa, ob, oc, w_g, w_br)
        x, xb = _outproj(merged, x, _cast(w_out, (l,), D_MODEL, D_MODEL), ln_g[l, 1][None],
                         ln_b[l, 1][None], alpha)

        x, xb = _ffn(x, xb, *ffn_weights(1), ln_g[l, 2][None], ln_b[l, 2][None], alpha)

        outs["rp"].append(ret_p)
        outs["rs"].append(ret_s)
        outs["wp"].append(wkv_p)
        outs["ws"].append(wkv_s)
        outs["sp"].append(shift_p[:, :, :C_COLS])
        outs["ss"].append(shift_s[:, :, :C_COLS])

    st = {k: jnp.stack(v) for k, v in outs.items()}
    y_p = x[:n_p].reshape(nb, t, D_MODEL)
    y_s = x[n_p:].reshape(nbs, s_len, D_MODEL)
    kv_p = lambda a: a.reshape(depth, nb, t, HA, DVA)
    kv_s = lambda a: a.reshape(depth, nbs, s_len, HA, DVA)
    return (y_p, y_s, kv_p(k_p), kv_p(v_p), st["rp"], st["wp"], st["sp"],
            kv_s(k_s), kv_s(v_s), st["rs"], st["ws"], st["ss"])
```

```python
import functools
import math

import jax
import jax.numpy as jnp
from jax import lax
from jax.experimental import pallas as pl
from jax.experimental.pallas import tpu as pltpu

F32 = jnp.float32
BF16 = jnp.bfloat16

D_MODEL = 2048
CHUNK = 64
HA, DA, DVA = 8, 64, 128
ROT_A = DA // 4
ROPE_THETA = 500000.0
HB, DKB, DVB = 8, 64, 128
RET_THETA = 10000.0
HC, NC = 16, 64
LORA_W, LORA_A, LORA_G = 64, 64, 160
WA, WB, WC = HA * DVA, HB * DVB, HC * NC
N_BRANCH = 3
A_COLS = 2 * HA * 2 * DA + HA * DVA
B_COLS = 2 * HB * DKB + HB * DVB + WB
C_COLS = 3 * WC + LORA_W + LORA_A + LORA_G
D_FF = 5504
LN_EPS = 1e-5
SUBLN_EPS = 1e-5
RET_EPS = 1e-6
GN_EPS_C = 64e-5

LANE = 128
D_FF_PAD = 5632
FFN_TF = 512
LORA_PAD = 384
C_PAD = 3 * WC + LORA_PAD
ATTN_HEADS_PER_STEP = 2
VMEM_LIMIT = 56 * 1024 * 1024


def _cparams(sem):
    return pltpu.CompilerParams(dimension_semantics=sem, vmem_limit_bytes=VMEM_LIMIT)


def _pick(n, cands):
    for c in cands:
        if n % c == 0:
            return c
    raise ValueError(f"no tile for {n}")


def _sigmoid(x):
    return 1.0 / (1.0 + jnp.exp(-x))


def _ln_rows(y, g, b, eps):
    mu = jnp.mean(y, axis=-1, keepdims=True)
    d = y - mu
    var = jnp.mean(d * d, axis=-1, keepdims=True)
    return d * lax.rsqrt(var + eps) * g + b


def _dot(a, b):
    return jnp.dot(a, b, preferred_element_type=F32)


def _dot_nt(a, b):
    return lax.dot_general(a, b, (((1,), (1,)), ((), ())), preferred_element_type=F32)


def _dot_tn(a, b):
    return lax.dot_general(a, b, (((0,), (0,)), ((), ())), preferred_element_type=F32)


def _split2(x):
    hi = x.astype(BF16)
    lo = (x - hi.astype(F32)).astype(BF16)
    return hi, lo


def _split3(x):
    hi = x.astype(BF16)
    r1 = x - hi.astype(F32)
    mid = r1.astype(BF16)
    lo = (r1 - mid.astype(F32)).astype(BF16)
    return hi, mid, lo


def _mm3(a, b, dot=_dot):
    ah, al = a
    bh, bl = b
    return dot(ah, bh) + (dot(ah, bl) + dot(al, bh))


def _cast_kernel(w_ref, o_ref):
    o_ref[...] = w_ref[...].astype(BF16)


def _cast(src, lead, rows, width):
    tr = _pick(rows, (256, 128, 64, 32, 16, 8))
    return pl.pallas_call(
        _cast_kernel,
        grid=(rows // tr,),
        in_specs=[pl.BlockSpec((None,) * len(lead) + (tr, width), lambda i: lead + (i, 0))],
        out_specs=pl.BlockSpec((tr, width), lambda i: (i, 0)),
        out_shape=jax.ShapeDtypeStruct((rows, width), BF16),
        compiler_params=_cparams(("parallel",)),
        name="cast",
    )(src)


def _cast_ffn_in_kernel(a_ref, b_ref, oa_ref, ob_ref):
    for src, dst in ((a_ref, oa_ref), (b_ref, ob_ref)):
        dst[:, 0:D_FF] = src[...].astype(BF16)
        dst[:, D_FF:D_FF_PAD] = jnp.zeros((dst.shape[0], D_FF_PAD - D_FF), BF16)


def _cast_ffn_in(ffn_in, l, i):
    tr = 128
    half = lambda k: pl.BlockSpec((None, None, tr, D_FF), lambda r: (l, i, r, k))
    out = pl.BlockSpec((tr, D_FF_PAD), lambda r: (r, 0))
    return pl.pallas_call(
        _cast_ffn_in_kernel,
        grid=(D_MODEL // tr,),
        in_specs=[half(0), half(1)],
        out_specs=[out, out],
        out_shape=[jax.ShapeDtypeStruct((D_MODEL, D_FF_PAD), BF16)] * 2,
        compiler_params=_cparams(("parallel",)),
        name="cast_ffn_in",
    )(ffn_in, ffn_in)


def _cast_ffn_out_kernel(w_ref, o_ref):
    tr = w_ref.shape[0]
    row = lax.broadcasted_iota(jnp.int32, w_ref.shape, 0) + pl.program_id(0) * tr
    o_ref[...] = jnp.where(row < D_FF, w_ref[...], 0.0).astype(BF16)


def _cast_ffn_out(ffn_out, l, i):
    tr = FFN_TF
    return pl.pallas_call(
        _cast_ffn_out_kernel,
        grid=(D_FF_PAD // tr,),
        in_specs=[pl.BlockSpec((None, None, tr, D_MODEL), lambda r: (l, i, r, 0))],
        out_specs=pl.BlockSpec((tr, D_MODEL), lambda r: (r, 0)),
        out_shape=jax.ShapeDtypeStruct((D_FF_PAD, D_MODEL), BF16),
        compiler_params=_cparams(("parallel",)),
        name="cast_ffn_out",
    )(ffn_out)


def _cast_wc_kernel(w_ref, o_ref):
    lane = lax.broadcasted_iota(jnp.int32, w_ref.shape, 1)
    valid = lane < C_COLS - pl.program_id(1) * LORA_PAD
    o_ref[...] = jnp.where(valid, w_ref[...], 0.0).astype(BF16)


def _cast_wc(w_in, l):
    tr = 512
    cb0 = (A_COLS + B_COLS) // LORA_PAD
    return pl.pallas_call(
        _cast_wc_kernel,
        grid=(D_MODEL // tr, C_PAD // LORA_PAD),
        in_specs=[pl.BlockSpec((None, tr, LORA_PAD), lambda r, j: (l, r, cb0 + j))],
        out_specs=pl.BlockSpec((tr, LORA_PAD), lambda r, j: (r, j)),
        out_shape=jax.ShapeDtypeStruct((D_MODEL, C_PAD), BF16),
        compiler_params=_cparams(("parallel", "parallel")),
        name="cast_wc",
    )(w_in)


def _ffn_kernel(x_ref, xb_ref, wa_ref, wb_ref, wd_ref, g_ref, b_ref, o_ref, ob_ref, acc_ref, *, alpha):
    j = pl.program_id(1)

    @pl.when(j == 0)
    def _():
        acc_ref[...] = jnp.zeros_like(acc_ref)

    xb = xb_ref[...]
    ha = _dot(xb, wa_ref[...])
    hb = _dot(xb, wb_ref[...])
    act = (ha * _sigmoid(ha) * hb).astype(BF16)
    acc_ref[...] += _dot(act, wd_ref[...])

    @pl.when(j == pl.num_programs(1) - 1)
    def _():
        y = alpha * x_ref[...] + 0.5 * acc_ref[...]
        out = _ln_rows(y, g_ref[...], b_ref[...], LN_EPS)
        o_ref[...] = out
        ob_ref[...] = out.astype(BF16)


def _ffn(x, xb, wa, wb, wd, g, b, alpha):
    n = x.shape[0]
    tm = _pick(n, (512, 256, 128, 64, 32, 16, 8))
    tf = FFN_TF
    grid = (n // tm, D_FF_PAD // tf)
    row = pl.BlockSpec((tm, D_MODEL), lambda i, j: (i, 0))
    vec = pl.BlockSpec((1, D_MODEL), lambda i, j: (0, 0))
    return pl.pallas_call(
        functools.partial(_ffn_kernel, alpha=alpha),
        grid=grid,
        in_specs=[
            row, row,
            pl.BlockSpec((D_MODEL, tf), lambda i, j: (0, j)),
            pl.BlockSpec((D_MODEL, tf), lambda i, j: (0, j)),
            pl.BlockSpec((tf, D_MODEL), lambda i, j: (j, 0)),
            vec, vec,
        ],
        out_specs=[row, row],
        out_shape=[jax.ShapeDtypeStruct((n, D_MODEL), F32), jax.ShapeDtypeStruct((n, D_MODEL), BF16)],
        scratch_shapes=[pltpu.VMEM((tm, D_MODEL), F32)],
        compiler_params=_cparams(("parallel", "arbitrary")),
        name="ffn",
    )(x, xb, wa, wb, wd, g, b)


def _proj_kernel(x_ref, w_ref, o_ref):
    o_ref[...] = _dot(x_ref[...], w_ref[...])


def _proj(xb, w, tn, name, lead=(), m=None):
    n = xb.shape[0]
    m = w.shape[-1] if m is None else m
    tm = _pick(n, (1024, 512, 256, 128, 64, 32, 16, 8))
    return pl.pallas_call(
        _proj_kernel,
        grid=(n // tm, m // tn),
        in_specs=[
            pl.BlockSpec((tm, D_MODEL), lambda i, j: (i, 0)),
            pl.BlockSpec((None,) * len(lead) + (D_MODEL, tn), lambda i, j: lead + (0, j)),
        ],
        out_specs=pl.BlockSpec((tm, tn), lambda i, j: (i, j)),
        out_shape=jax.ShapeDtypeStruct((n, m), F32),
        compiler_params=_cparams(("parallel", "arbitrary")),
        name=name,
    )(xb, w)


def _rope_tables(pos, rot, theta, width):
    half = rot // 2
    freq = jnp.power(theta, -jnp.arange(half, dtype=F32) / half)
    ang = pos.astype(F32)[:, None] * freq[None, :]
    cos, sin = jnp.cos(ang), jnp.sin(ang)
    n = pos.shape[0]
    pad = width - rot
    c = jnp.concatenate([cos, cos, jnp.ones((n, pad), F32)], axis=-1)
    s_lo = jnp.concatenate([-sin, jnp.zeros((n, half + pad), F32)], axis=-1)
    s_hi = jnp.concatenate([jnp.zeros((n, half), F32), sin, jnp.zeros((n, pad), F32)], axis=-1)
    rep = LANE // width
    return tuple(jnp.tile(t, (1, rep)) for t in (c, s_lo + s_hi))


def _rope_kernel(xa_ref, xv_ref, xb_ref, ca_ref, sa_ref, cb_ref, sb_ref,
                 qa_buf, kall_buf, vall_buf, qkb_buf, qa_ref, kall_ref, vall_ref, qkb_ref, kd_ref, vd_ref):
    del qa_buf, kall_buf, vall_buf, qkb_buf

    lane = lax.broadcasted_iota(jnp.int32, (xa_ref.shape[0], LANE), 1)

    def partner(half, width):
        p = lane % width
        return jnp.where(p < half, lane + half, jnp.where(p < 2 * half, lane - half, lane))

    def rot(x, c, s, idx):
        return x * c + jnp.take_along_axis(x, idx, axis=1) * s

    ca, sa = ca_ref[...], sa_ref[...]
    idx_a = partner(ROT_A // 2, DA)
    nqa = HA * 2 * DA // LANE
    tm = xa_ref.shape[0]
    ks = []
    for t in range(2 * nqa):
        y = rot(xa_ref[:, t * LANE:(t + 1) * LANE], ca, sa, idx_a)
        if t < nqa:
            qa_ref[:, t * LANE:(t + 1) * LANE] = (y * (DA ** -0.5)).astype(BF16)
        else:
            ks.append(y)
            kd_ref[:, (t - nqa) * LANE:(t - nqa + 1) * LANE] = y.astype(BF16)
    kall_ref[...] = jnp.concatenate(ks, axis=1).reshape(tm, HA, 2 * DA)
    v = xv_ref[...]
    vall_ref[...] = v.reshape(tm, HA, DVA)
    vd_ref[...] = v.astype(BF16)
    cb, sb = cb_ref[...], sb_ref[...]
    idx_b = partner(DKB // 2, DKB)
    nqb = HB * DKB // LANE
    for t in range(2 * nqb):
        sl = slice(t * LANE, (t + 1) * LANE)
        y = rot(xb_ref[:, sl], cb, sb, idx_b)
        if t >= nqb:
            y = y * (DKB ** -0.5)
        qkb_ref[:, sl] = y


def _rope(pab, tabs_a, tabs_b, qa_buf, kall_buf, vall_buf, qkb_buf, row0, nrows, kv_row0):
    tm = _pick(nrows, (256, 128, 64, 32, 16, 8))
    rb0, kb0 = row0 // tm, kv_row0 // tm
    wqk = 2 * HA * 2 * DA
    wb = 2 * HB * DKB
    tab = pl.BlockSpec((tm, LANE), lambda i: (rb0 + i, 0))
    anyspec = pl.BlockSpec(memory_space=pl.ANY)
    bufs = (qa_buf, kall_buf, vall_buf, qkb_buf)
    cache = pl.BlockSpec((tm, HA, DVA), lambda i: (kb0 + i, 0, 0))
    dense = pl.BlockSpec((tm, WA), lambda i: (i, 0))
    return pl.pallas_call(
        _rope_kernel,
        grid=(nrows // tm,),
        in_specs=[
            pl.BlockSpec((tm, wqk), lambda i: (rb0 + i, 0)),
            pl.BlockSpec((tm, WA), lambda i: (rb0 + i, wqk // WA)),
            pl.BlockSpec((tm, wb), lambda i: (rb0 + i, A_COLS // wb)),
            tab, tab, tab, tab,
            anyspec, anyspec, anyspec, anyspec,
        ],
        out_specs=[
            pl.BlockSpec((tm, WA), lambda i: (rb0 + i, 0)),
            cache, cache,
            pl.BlockSpec((tm, wb), lambda i: (rb0 + i, 0)),
            dense, dense,
        ],
        out_shape=[jax.ShapeDtypeStruct(b.shape, b.dtype) for b in bufs]
                  + [jax.ShapeDtypeStruct((nrows, WA), BF16)] * 2,
        input_output_aliases={7: 0, 8: 1, 9: 2, 10: 3},
        compiler_params=_cparams(("parallel",)),
        name="rope",
    )(pab, pab, pab, *tabs_a, *tabs_b, *bufs)


def _diff_lambda(dl_ref, lam_init):
    lv = dl_ref[...]
    s1 = jnp.sum(lv[0:1] * lv[1:2], axis=-1, keepdims=True)
    s2 = jnp.sum(lv[2:3] * lv[3:4], axis=-1, keepdims=True)
    return jnp.exp(s1) - jnp.exp(s2) + lam_init


def _split_maps(q):
    lane = lax.broadcasted_iota(jnp.int32, q.shape, 1)
    zero = jnp.zeros_like(q)
    return jnp.where(lane < DA, q, zero), jnp.where(lane >= DA, q, zero)


def _subln(o, g, scale):
    ms = jnp.mean(o * o, axis=-1, keepdims=True)
    return o * lax.rsqrt(ms + SUBLN_EPS) * g * scale


def _attn_prompt_kernel(q_ref, kb_ref, v_ref, dl_ref, g_ref, o_buf, o_ref, vb_ref, s_ref,
                        m_ref, acc_ref, *, tq, tk, nh, lam_init):
    del o_buf
    qi = pl.program_id(2)
    hsl = [slice(h * 2 * DA, (h + 1) * 2 * DA) for h in range(nh)]
    chains = range(2 * nh)

    @pl.when(qi == 0)
    def _():
        for h in range(nh):
            vb_ref[:, 2 * h * DVA:(2 * h + 1) * DVA] = v_ref[:, hsl[h]]
            vb_ref[:, (2 * h + 1) * DVA:(2 * h + 2) * DVA] = jnp.ones((v_ref.shape[0], DVA), BF16)

    lam = _diff_lambda(dl_ref, lam_init)
    qs = [qm for h in range(nh) for qm in _split_maps(q_ref[:, hsl[h]])]

    def scores(kb, slot):
        rows = pl.ds(pl.multiple_of(kb * tk, tk), tk)
        for c in chains:
            s_ref[slot, c] = _dot_nt(qs[c], kb_ref[rows, hsl[c // 2]])

    def consume(ss, start, width):
        stats = []
        for c in chains:
            m = m_ref[c]
            m_new = jnp.maximum(m, jnp.max(ss[c], axis=-1, keepdims=True))
            m_ref[c] = m_new
            stats.append((jnp.exp(m - m_new), jnp.exp(ss[c] - m_new).astype(BF16)))
        pvs = [_dot(stats[c][1], vb_ref[pl.ds(start, width), (c // 2) * 2 * DVA:(c // 2 + 1) * 2 * DVA])
               for c in chains]
        for c in chains:
            acc_ref[c] = stats[c][0] * acc_ref[c] + pvs[c]

    def full_step(kb, slot):
        scores(kb + 1, 1 - slot)
        consume([s_ref[slot, c] for c in chains], pl.multiple_of(kb * tk, tk), tk)

    q0 = qi * tq
    n_full = q0 // tk
    m_ref[...] = jnp.full(m_ref.shape, -1e30, F32)
    acc_ref[...] = jnp.zeros_like(acc_ref)
    scores(0, 0)

    def two_steps(j, _):
        full_step(2 * j, 0)
        full_step(2 * j + 1, 1)
        return 0

    lax.fori_loop(0, n_full // 2, two_steps, 0)

    @pl.when(n_full % 2 == 1)
    def _():
        full_step(n_full - 1, 0)

    tail = pl.multiple_of(n_full * tk, tk)
    g = g_ref[...]

    for r in range(tk // tq):
        @pl.when(q0 - tail == r * tq)
        def _():
            width = (r + 1) * tq
            row = lax.broadcasted_iota(jnp.int32, (tq, width), 0) // CHUNK
            col = lax.broadcasted_iota(jnp.int32, (tq, width), 1) // CHUNK
            ok = col - row <= r * (tq // CHUNK)
            consume([jnp.where(ok, s_ref[n_full % 2, c, :, 0:width], -1e30) for c in chains], tail, width)
            for h in range(nh):
                a1, a2 = acc_ref[2 * h], acc_ref[2 * h + 1]
                o = a1[:, 0:DVA] / a1[:, DVA:2 * DVA] - lam * (a2[:, 0:DVA] / a2[:, DVA:2 * DVA])
                o_ref[:, hsl[h]] = _subln(o, g, 1.0 - lam_init).astype(BF16)


def _attn_prompt(qa, kd, vd, dl, g, o_buf, nb, t, lam_init):
    tq = _pick(t, (512, 256, 128, 64))
    tk = _pick(t, (1024, 512, 256, 128, 64))
    nh = ATTN_HEADS_PER_STEP
    nq = t // tq
    kv = pl.BlockSpec((t, nh * 2 * DA), lambda b, h, i: (b, h))
    return pl.pallas_call(
        functools.partial(_attn_prompt_kernel, tq=tq, tk=tk, nh=nh, lam_init=lam_init),
        grid=(nb, HA // nh, nq),
        in_specs=[
            pl.BlockSpec((tq, nh * 2 * DA), lambda b, h, i: (b * nq + i, h)),
            kv, kv,
            pl.BlockSpec((4, DA), lambda b, h, i: (0, 0)),
            pl.BlockSpec((1, DVA), lambda b, h, i: (0, 0)),
            pl.BlockSpec(memory_space=pl.ANY),
        ],
        out_specs=pl.BlockSpec((tq, nh * DVA), lambda b, h, i: (b * nq + i, h)),
        out_shape=jax.ShapeDtypeStruct(o_buf.shape, o_buf.dtype),
        input_output_aliases={5: 0},
        scratch_shapes=[pltpu.VMEM((t, nh * 2 * DVA), BF16),
                        pltpu.VMEM((2, 2 * nh, tq, tk), F32),
                        pltpu.VMEM((2 * nh, tq, 1), F32), pltpu.VMEM((2 * nh, tq, 2 * DVA), F32)],
        compiler_params=_cparams(("parallel", "parallel", "arbitrary")),
        name="attn_prompt",
    )(qa, kd, vd, dl, g, o_buf)


def _attn_sample_kernel(q_ref, kn_ref, vn_ref, ck_ref, cv_ref, dl_ref, g_ref, o_buf, o_ref, *,
                        s_len, lam_init):
    del o_buf
    lam = _diff_lambda(dl_ref, lam_init)
    g = g_ref[...]
    p_len = ck_ref.shape[0]
    kc_all = ck_ref[...].reshape(p_len, WA).astype(BF16)
    vc_all = cv_ref[...].reshape(p_len, WA).astype(BF16)
    heads = range(HA)
    sls = [slice(h * 2 * DA, (h + 1) * 2 * DA) for h in heads]
    qq = [jnp.concatenate(_split_maps(q_ref[:, sl]), axis=0) for sl in sls]
    s_c = [_dot_nt(qq[h], kc_all[:, sls[h]]) for h in heads]
    s_n = [_dot_nt(qq[h], kn_ref[:, sls[h]]) for h in heads]
    p_c, p_n, den = [], [], []
    for h in heads:
        m = jnp.maximum(jnp.max(s_c[h], axis=-1, keepdims=True), jnp.max(s_n[h], axis=-1, keepdims=True))
        e_c = jnp.exp(s_c[h] - m)
        e_n = jnp.exp(s_n[h] - m)
        den.append(jnp.sum(e_c, axis=-1, keepdims=True) + jnp.sum(e_n, axis=-1, keepdims=True))
        p_c.append(e_c.astype(BF16))
        p_n.append(e_n.astype(BF16))
    pv = [_dot(p_c[h], vc_all[:, sls[h]]) + _dot(p_n[h], vn_ref[:, sls[h]]) for h in heads]
    for h in heads:
        o = pv[h] / den[h]
        od = o[:s_len] - lam * o[s_len:]
        o_ref[:, sls[h]] = _subln(od, g, 1.0 - lam_init).astype(BF16)


def _attn_sample(qa, kd, vd, ck, cv, dl, g, o_buf, layer, nbs, s_len, row0, lam_init):
    p_len = ck.shape[2]
    rb0 = row0 // s_len
    new = pl.BlockSpec((s_len, WA), lambda b: (b, 0))
    cache = pl.BlockSpec((None, None, p_len, HA, DVA), lambda b: (layer, b, 0, 0, 0))
    return pl.pallas_call(
        functools.partial(_attn_sample_kernel, s_len=s_len, lam_init=lam_init),
        grid=(nbs,),
        in_specs=[
            pl.BlockSpec((s_len, WA), lambda b: (rb0 + b, 0)),
            new, new, cache, cache,
            pl.BlockSpec((4, DA), lambda b: (0, 0)),
            pl.BlockSpec((1, DVA), lambda b: (0, 0)),
            pl.BlockSpec(memory_space=pl.ANY),
        ],
        out_specs=pl.BlockSpec((s_len, WA), lambda b: (rb0 + b, 0)),
        out_shape=jax.ShapeDtypeStruct(o_buf.shape, o_buf.dtype),
        input_output_aliases={7: 0},
        compiler_params=_cparams(("parallel",)),
        name="attn_sample",
    )(qa, kd, vd, ck, cv, dl, g, o_buf)


def _ret_kernel(q_ref, k_ref, v_ref, gate_ref, r0_ref, o_buf, o_ref, rout_ref, st_ref, dec_ref, *, blk):
    del o_buf
    c = pl.program_id(1)

    @pl.when(c == 0)
    def _():
        st_ref[...] = jnp.zeros_like(st_ref)
        for h in range(HB):
            e = h % 2
            st_ref[h, e * DKB:(e + 1) * DKB, :] = r0_ref[h]
        n_i = lax.broadcasted_iota(jnp.int32, (blk, blk), 0)
        m_i = lax.broadcasted_iota(jnp.int32, (blk, blk), 1)
        diff = jnp.maximum(n_i - m_i, 0).astype(F32)
        for h in range(HB):
            dec_ref[h] = jnp.where(n_i >= m_i, jnp.exp(diff * math.log1p(-2.0 ** (-5.0 - h))), 0.0)

    n_col = lax.broadcasted_iota(jnp.int32, (blk, LANE), 0).astype(F32)
    lane = lax.broadcasted_iota(jnp.int32, (blk, LANE), 1)
    heads = range(HB)
    log_g = [math.log1p(-2.0 ** (-5.0 - h)) for h in heads]
    vsl = [slice(h * DVB, (h + 1) * DVB) for h in heads]
    qb, kb, kd, v = [], [], [], []
    for h in heads:
        p, e = h // 2, h % 2
        sl = slice(p * LANE, (p + 1) * LANE)
        mine = (lane >= DKB) if e else (lane < DKB)
        k = jnp.where(mine, k_ref[:, sl], 0.0)
        qb.append(jnp.where(mine, q_ref[:, sl], 0.0).astype(BF16))
        kb.append(k.astype(BF16))
        kd.append((k * jnp.exp((blk - 1.0 - n_col) * log_g[h])).astype(BF16))
        v.append(v_ref[:, vsl[h]].astype(BF16))
    r_old = [st_ref[h] for h in heads]
    s = [_dot_nt(qb[h], kb[h]) for h in heads]
    cross = [_dot(qb[h], r_old[h].astype(BF16)) for h in heads]
    grow = [_dot_tn(kd[h], v[h]) for h in heads]
    sd = [(s[h] * dec_ref[h]).astype(BF16) for h in heads]
    intra = [_dot(sd[h], v[h]) for h in heads]
    for h in heads:
        st_ref[h] = math.exp(blk * log_g[h]) * r_old[h] + grow[h]
        o = intra[h] + cross[h] * jnp.exp((n_col + 1.0) * log_g[h])
        ms = jnp.mean(o * o, axis=-1, keepdims=True)
        gt = gate_ref[:, vsl[h]]
        o_ref[:, vsl[h]] = (o * lax.rsqrt(ms + RET_EPS) * (gt * _sigmoid(gt))).astype(BF16)

    @pl.when(c == pl.num_programs(1) - 1)
    def _():
        for h in range(HB):
            e = h % 2
            rout_ref[h] = st_ref[h, e * DKB:(e + 1) * DKB, :]


def _retention(qkb, pab, r0, o_buf, nseg, seg_len, row0, blk):
    nc = seg_len // blk
    rb0 = row0 // blk
    vcol = (A_COLS + 2 * HB * DKB) // WB
    row = lambda b, c: rb0 + b * nc + c
    return pl.pallas_call(
        functools.partial(_ret_kernel, blk=blk),
        grid=(nseg, nc),
        in_specs=[
            pl.BlockSpec((blk, HB * DKB), lambda b, c: (row(b, c), 0)),
            pl.BlockSpec((blk, HB * DKB), lambda b, c: (row(b, c), 1)),
            pl.BlockSpec((blk, WB), lambda b, c: (row(b, c), vcol)),
            pl.BlockSpec((blk, WB), lambda b, c: (row(b, c), vcol + 1)),
            pl.BlockSpec((None, HB, DKB, DVB), lambda b, c: (b, 0, 0, 0)),
            pl.BlockSpec(memory_space=pl.ANY),
        ],
        out_specs=[
            pl.BlockSpec((blk, WB), lambda b, c: (row(b, c), 0)),
            pl.BlockSpec((None, HB, DKB, DVB), lambda b, c: (b, 0, 0, 0)),
        ],
        out_shape=[jax.ShapeDtypeStruct(o_buf.shape, o_buf.dtype),
                   jax.ShapeDtypeStruct((nseg, HB, DKB, DVB), F32)],
        input_output_aliases={5: 0},
        scratch_shapes=[pltpu.VMEM((HB, 2 * DKB, DVB), F32), pltpu.VMEM((HB, blk, blk), F32)],
        compiler_params=_cparams(("parallel", "arbitrary")),
        name="retention",
    )(qkb, qkb, pab, pab, r0, o_buf)


def _head_sum(x, ones_bd):
    hi, lo = _split2(x)
    return _dot(hi, ones_bd) + _dot(lo, ones_bd)


def _wkv_kernel(*refs, ck, nsb):
    pc_refs = refs[:nsb]
    (prev_ref, s0_ref, mu_ref, w0_ref, a0_ref, kk_ref, ka_ref, rk_ref, w2_ref, a2_ref, g2_ref,
     lng_ref, lnb_ref, tri_ref, ones_ref, o_ref, sout_ref, last_ref, st_ref, carry_ref) = refs[nsb:]
    c = pl.program_id(1)

    @pl.when(c == 0)
    def _():
        zero = jnp.zeros((NC, NC), F32)
        for g in range(nsb):
            for p in range(HC // 2):
                top = jnp.concatenate([s0_ref[g, 2 * p], zero], axis=1)
                bot = jnp.concatenate([zero, s0_ref[g, 2 * p + 1]], axis=1)
                st_ref[g, p] = jnp.concatenate([top, bot], axis=0)
        carry_ref[...] = prev_ref[...]

    row = lax.broadcasted_iota(jnp.int32, (ck, C_PAD), 0)
    pcs, shifted = [], []
    for g, pc_ref in enumerate(pc_refs):
        pc = pc_ref[...]
        pcs.append(pc)
        shifted.append(jnp.where(row == 0, carry_ref[g], pltpu.roll(pc, 1, 0)))
        carry_ref[g] = pc[ck - 1:ck, :]
        last_ref[g] = pc[ck - 1:ck, :]
    pc_all = jnp.concatenate(pcs, axis=0)
    pcm = pc_all + mu_ref[...] * (jnp.concatenate(shifted, axis=0) - pc_all)
    rc = pcm[:, 0:WC]
    kc = pcm[:, WC:2 * WC]
    vc = pcm[:, 2 * WC:3 * WC]
    lora = pcm[:, 3 * WC:C_PAD]
    z = w0_ref[...] + _dot(jnp.tanh(lora).astype(BF16), w2_ref[...])
    u = -z
    softplus = jnp.maximum(u, 0.0) + jnp.log1p(jnp.exp(-jnp.abs(u)))
    lw = -jnp.exp(-softplus - 0.5)
    a_sig = _sigmoid(a0_ref[...] + _dot(lora.astype(BF16), a2_ref[...]))
    gate = _dot(_sigmoid(lora).astype(BF16), g2_ref[...])
    kk = kc * kk_ref[...]
    kc2 = kc * (1.0 + (a_sig - 1.0) * ka_ref[...])
    rkr = rc * kc2 * rk_ref[...]
    ones_bd = ones_ref[...]

    tri = tri_ref[...]
    l1, l2, l3 = _split3(lw)
    cum = _dot(tri, l1) + (_dot(tri, l2) + _dot(tri, l3))
    total = jnp.concatenate(
        [jnp.broadcast_to(cum[(g + 1) * ck - 1:(g + 1) * ck, :], (ck, WC)) for g in range(nsb)], axis=0)
    e_neg = jnp.exp(-cum)
    e_rem = jnp.exp(total - cum)
    e_exc = jnp.exp(cum - lw)
    r_t = rc * jnp.exp(cum)
    k_t = kc2 * e_neg
    k_p = kc2 * e_rem
    p_tot = jnp.exp(total)

    lane = lax.broadcasted_iota(jnp.int32, (ck, LANE), 1)
    m0 = lane < NC

    def stack(x):
        return jnp.concatenate([jnp.where(m0, x, 0.0), jnp.where(m0, 0.0, x)], axis=0)

    n2 = 2 * ck
    ri = lax.broadcasted_iota(jnp.int32, (n2, n2), 0)
    ci = lax.broadcasted_iota(jnp.int32, (n2, n2), 1)
    same = (ri // ck) == (ci // ck)
    strict = jnp.logical_and(same, ri > ci)
    incl = jnp.logical_and(same, ri >= ci)

    bf = lambda t: t.astype(BF16)
    pairs = range(HC // 2)
    chains = [(g, p) for g in range(nsb) for p in pairs]
    a_s, r_s, b_s, k_s, v_s, bp_s, kp_s, bonus = {}, {}, {}, {}, {}, {}, {}, {}
    for p in pairs:
        sl = slice(p * LANE, (p + 1) * LANE)
        kkt = kk[:, sl]
        kn = kkt / jnp.maximum(jnp.sqrt(_head_sum(kkt * kkt, ones_bd)), 1e-12)
        bn = kn * a_sig[:, sl]
        a_tp = -kn * e_exc[:, sl]
        b_tp = bn * e_neg[:, sl]
        b_pp = bn * e_rem[:, sl]
        bonus_p = _head_sum(rkr[:, sl], ones_bd) * vc[:, sl]
        for g in range(nsb):
            rows = slice(g * ck, (g + 1) * ck)
            a_s[g, p] = bf(stack(a_tp[rows]))
            r_s[g, p] = bf(stack(r_t[rows, sl]))
            b_s[g, p] = bf(stack(b_tp[rows]))
            k_s[g, p] = bf(stack(k_t[rows, sl]))
            v_s[g, p] = _split2(stack(vc[rows, sl]))
            bp_s[g, p] = _split2(stack(b_pp[rows]))
            kp_s[g, p] = _split2(stack(k_p[rows, sl]))
            bonus[g, p] = bonus_p[rows]

    s_old = {gp: st_ref[gp[0], gp[1]] for gp in chains}
    s_b = {gp: bf(s_old[gp]) for gp in chains}
    if n2 % LANE == 0:
        big = {gp: _dot_nt(jnp.concatenate([a_s[gp], r_s[gp]], axis=0),
                           jnp.concatenate([b_s[gp], k_s[gp]], axis=0)) for gp in chains}
        ab = {gp: big[gp][0:n2, 0:n2] for gp in chains}
        ak = {gp: big[gp][0:n2, n2:2 * n2] for gp in chains}
        rb = {gp: big[gp][n2:2 * n2, 0:n2] for gp in chains}
        rk = {gp: big[gp][n2:2 * n2, n2:2 * n2] for gp in chains}
    else:
        ab = {gp: _dot_nt(a_s[gp], b_s[gp]) for gp in chains}
        ak = {gp: _dot_nt(a_s[gp], k_s[gp]) for gp in chains}
        rb = {gp: _dot_nt(r_s[gp], b_s[gp]) for gp in chains}
        rk = {gp: _dot_nt(r_s[gp], k_s[gp]) for gp in chains}
    lk = {gp: bf(jnp.where(strict, ab[gp], 0.0)) for gp in chains}
    l_ak = {gp: bf(jnp.where(strict, ak[gp], 0.0)) for gp in chains}
    l_rb = {gp: bf(jnp.where(incl, rb[gp], 0.0)) for gp in chains}
    l_rk = {gp: bf(jnp.where(incl, rk[gp], 0.0)) for gp in chains}
    if n2 % LANE == 0:
        sv = {gp: jnp.concatenate([bf(s_old[gp].T), v_s[gp][0]], axis=0) for gp in chains}
        x = {gp: _dot(jnp.concatenate([a_s[gp], l_ak[gp]], axis=1), sv[gp]) for gp in chains}
        y_part = {gp: _dot(jnp.concatenate([r_s[gp], l_rk[gp]], axis=1), sv[gp]) for gp in chains}
    else:
        x = {gp: _dot_nt(a_s[gp], s_b[gp]) + _dot(l_ak[gp], v_s[gp][0]) for gp in chains}
        y_part = {gp: _dot_nt(r_s[gp], s_b[gp]) + _dot(l_rk[gp], v_s[gp][0]) for gp in chains}
    span = 1
    while span < ck:
        x = {gp: x[gp] + _dot(lk[gp], bf(x[gp])) for gp in chains}
        span *= 2
        if span < ck:
            lk = {gp: bf(_dot(lk[gp], lk[gp])) for gp in chains}
    x_sp = {gp: _split2(x[gp]) for gp in chains}
    y_st = {gp: y_part[gp] + _dot(l_rb[gp], x_sp[gp][0]) for gp in chains}
    for g, p in chains:
        sl = slice(p * LANE, (p + 1) * LANE)
        decay = p_tot[(g + 1) * ck - 1:(g + 1) * ck, sl]
        uv = tuple(jnp.concatenate([xs, vs], axis=0) for xs, vs in zip(x_sp[g, p], v_s[g, p]))
        bk = tuple(jnp.concatenate([bs, ks], axis=0) for bs, ks in zip(bp_s[g, p], kp_s[g, p]))
        st_ref[g, p] = s_old[g, p] * decay + _mm3(uv, bk, _dot_tn)
    for g, p in chains:
        sl = slice(p * LANE, (p + 1) * LANE)
        rows = slice(g * ck, (g + 1) * ck)
        y = y_st[g, p][0:ck] + y_st[g, p][ck:n2]
        mean = _head_sum(y, ones_bd) * (1.0 / NC)
        d = y - mean
        var = _head_sum(d * d, ones_bd) * (1.0 / NC)
        yn = d * lax.rsqrt(var + GN_EPS_C) * lng_ref[:, sl] + lnb_ref[:, sl]
        o_ref[g, :, sl] = ((yn + bonus[g, p]) * gate[rows, sl]).astype(BF16)

    @pl.when(c == pl.num_programs(1) - 1)
    def _():
        for g in range(nsb):
            for p in range(HC // 2):
                s_pair = st_ref[g, p]
                sout_ref[g, 2 * p] = s_pair[0:NC, 0:NC]
                sout_ref[g, 2 * p + 1] = s_pair[NC:2 * NC, NC:2 * NC]


def _wkv(pc, prev, s0, vecs, mats, lng, lnb, ones_bd, nseg, seg_len, row0, ck, nsb):
    nc = seg_len // ck
    rb0 = row0 // ck
    pos = jnp.arange(nsb * ck)
    tri = ((pos[:, None] // ck == pos[None, :] // ck) & (pos[:, None] >= pos[None, :])).astype(BF16)
    pc_spec = lambda g: pl.BlockSpec((ck, C_PAD), lambda b, c: (rb0 + (b * nsb + g) * nc + c, 0))
    vec = lambda w: pl.BlockSpec((1, w), lambda b, c: (0, 0))
    mat = pl.BlockSpec((LORA_PAD, WC), lambda b, c: (0, 0))
    shift = pl.BlockSpec((nsb, 1, C_PAD), lambda b, c: (b, 0, 0))
    st = pl.BlockSpec((nsb, HC, NC, NC), lambda b, c: (b, 0, 0, 0))
    return pl.pallas_call(
        functools.partial(_wkv_kernel, ck=ck, nsb=nsb),
        grid=(nseg // nsb, nc),
        in_specs=[pc_spec(g) for g in range(nsb)] + [
            shift, st,
            vec(C_PAD), vec(WC), vec(WC), vec(WC), vec(WC), vec(WC),
            mat, mat, mat, vec(WC), vec(WC),
            pl.BlockSpec((nsb * ck, nsb * ck), lambda b, c: (0, 0)),
            pl.BlockSpec((LANE, LANE), lambda b, c: (0, 0)),
        ],
        out_specs=[pl.BlockSpec((nsb, ck, WC), lambda b, c: (b, c, 0)), st, shift],
        out_shape=[jax.ShapeDtypeStruct((nseg, seg_len, WC), BF16),
                   jax.ShapeDtypeStruct((nseg, HC, NC, NC), F32),
                   jax.ShapeDtypeStruct((nseg, 1, C_PAD), F32)],
        scratch_shapes=[pltpu.VMEM((nsb, HC // 2, LANE, LANE), F32), pltpu.VMEM((nsb, 1, C_PAD), F32)],
        compiler_params=_cparams(("parallel", "arbitrary")),
        name="wkv",
    )(*([pc] * nsb), prev, s0, *vecs, *mats, lng, lnb, tri, ones_bd)


def _merge_kernel(xb_ref, oa_ref, ob_ref, oc_ref, ga_ref, gb_ref, gc_ref, wa_ref, wb_ref, wc_ref, o_ref):
    xb = xb_ref[...]
    zs = [_dot(xb, g_ref[...]) for g_ref in (ga_ref, gb_ref, gc_ref)]
    ys = [_dot(b_ref[...], w_ref[...])
          for b_ref, w_ref in ((oa_ref, wa_ref), (ob_ref, wb_ref), (oc_ref, wc_ref))]
    o_ref[...] = (_sigmoid(zs[0]) * ys[0] + _sigmoid(zs[1]) * ys[1]) + _sigmoid(zs[2]) * ys[2]


def _merge(xb, oa, ob, oc, w_gate, wbr):
    n = oa.shape[0]
    tm = _pick(n, (1024, 512, 256, 128, 64, 32, 16, 8))
    tn = 512
    nj = D_MODEL // tn
    bw = oa.shape[1]
    o_spec = pl.BlockSpec((tm, bw), lambda i, j: (i, 0))
    g_spec = lambda k: pl.BlockSpec((D_MODEL, tn), lambda i, j: (0, k * nj + j))
    w_spec = pl.BlockSpec((bw, tn), lambda i, j: (0, j))
    return pl.pallas_call(
        _merge_kernel,
        grid=(n // tm, nj),
        in_specs=[pl.BlockSpec((tm, D_MODEL), lambda i, j: (i, 0)), o_spec, o_spec, o_spec,
                  g_spec(0), g_spec(1), g_spec(2), w_spec, w_spec, w_spec],
        out_specs=pl.BlockSpec((tm, tn), lambda i, j: (i, j)),
        out_shape=jax.ShapeDtypeStruct((n, D_MODEL), F32),
        compiler_params=_cparams(("parallel", "arbitrary")),
        name="merge",
    )(xb, oa, ob, oc, w_gate, w_gate, w_gate, *wbr)


def _outproj_kernel(m_ref, x_ref, w_ref, g_ref, b_ref, o_ref, ob_ref, *, alpha):
    half = m_ref.shape[0] // 2
    for r in (slice(0, half), slice(half, 2 * half)):
        y = alpha * x_ref[r, :] + _dot(m_ref[r, :].astype(BF16), w_ref[...])
        out = _ln_rows(y, g_ref[...], b_ref[...], LN_EPS)
        o_ref[r, :] = out
        ob_ref[r, :] = out.astype(BF16)


def _outproj(merged, x, w, g, b, alpha):
    n = x.shape[0]
    tm = _pick(n, (512, 256, 128, 64, 32))
    row = pl.BlockSpec((tm, D_MODEL), lambda i: (i, 0))
    vec = pl.BlockSpec((1, D_MODEL), lambda i: (0, 0))
    return pl.pallas_call(
        functools.partial(_outproj_kernel, alpha=alpha),
        grid=(n // tm,),
        in_specs=[row, row, pl.BlockSpec((D_MODEL, D_MODEL), lambda i: (0, 0)), vec, vec],
        out_specs=[row, row],
        out_shape=[jax.ShapeDtypeStruct((n, D_MODEL), F32), jax.ShapeDtypeStruct((n, D_MODEL), BF16)],
        compiler_params=_cparams(("parallel",)),
        name="outproj",
    )(merged, x, w, g, b)


def kernel(x_prompt, x_sample, cache_k, cache_v, state_ret, state_wkv, state_shift, w_in, w_gate,
           w_branch, w_out, ffn_in, ffn_out, ln_g, ln_b, diff_lambda, subln_g, mu_c, w0, w2, a0, a2,
           g2, k_k, k_a, r_k, lnx_g, lnx_b):
    nb, t, _ = x_prompt.shape
    nbs, s_len, _ = x_sample.shape
    depth = w_in.shape[0]
    p_len = cache_k.shape[2]
    n_p = nb * t
    n_s = nbs * s_len
    n = n_p + n_s
    alpha = (2 * depth) ** 0.25

    x = jnp.concatenate([x_prompt.reshape(n_p, D_MODEL), x_sample.reshape(n_s, D_MODEL)], axis=0)
    xb = x.astype(BF16)

    pos = jnp.concatenate([jnp.tile(jnp.arange(t, dtype=jnp.int32), nb),
                           jnp.tile(p_len + jnp.arange(s_len, dtype=jnp.int32), nbs)])
    tabs_a = _rope_tables(pos, ROT_A, ROPE_THETA, DA)
    tabs_b = _rope_tables(pos, DKB, RET_THETA, DKB)

    ck_all, cv_all = cache_k, cache_v

    eye = jnp.arange(LANE) // NC
    ones_bd = (eye[:, None] == eye[None, :]).astype(BF16)
    ck_p = _pick(t, (CHUNK,))
    ck_s = s_len
    nsb_p = _pick(nb, (2, 1))
    nsb_s = _pick(nbs, (4, 2, 1))
    ret_blk = _pick(t, (256, 128, 64))

    outs = {k: [] for k in ("rp", "wp", "sp", "rs", "ws", "ss")}

    w_in_b = w_in.astype(BF16)

    qa = jnp.zeros((n, WA), BF16)
    qkb = jnp.zeros((n, 2 * HB * DKB), F32)
    k_p, v_p = (jnp.zeros((depth * n_p, HA, DVA), F32) for _ in range(2))
    k_s, v_s = (jnp.zeros((depth * n_s, HA, DVA), F32) for _ in range(2))
    oa, ob = jnp.zeros((n, WA), BF16), jnp.zeros((n, WB), BF16)

    for l in range(depth):
        lam_init = 0.8 - 0.6 * math.exp(-0.3 * l)

        def ffn_weights(i):
            return (*_cast_ffn_in(ffn_in, l, i), _cast_ffn_out(ffn_out, l, i))

        x, xb = _ffn(x, xb, *ffn_weights(0), ln_g[l, 0][None], ln_b[l, 0][None], alpha)

        w_c = _cast_wc(w_in_b, l)
        pab = _proj(xb, w_in_b, 1024, name="proj_ab", lead=(l,), m=A_COLS + B_COLS)
        pc = _proj(xb, w_c, C_PAD // 3, name="proj_c")
        w_g = _cast(w_gate, (l,), D_MODEL, N_BRANCH * D_MODEL)
        qa, k_p, v_p, qkb, kd_p, vd_p = _rope(pab, tabs_a, tabs_b, qa, k_p, v_p, qkb, 0, n_p, l * n_p)
        qa, k_s, v_s, qkb, kd_s, vd_s = _rope(pab, tabs_a, tabs_b, qa, k_s, v_s, qkb, n_p, n_s, l * n_s)

        dl = diff_lambda[l]
        sg = subln_g[l][None]
        oa = _attn_prompt(qa, kd_p, vd_p, dl, sg, oa, nb, t, lam_init)
        oa = _attn_sample(qa, kd_s, vd_s, ck_all, cv_all, dl, sg, oa, l, nbs, s_len, n_p, lam_init)

        ob, ret_p = _retention(qkb, pab, jnp.zeros((nb, HB, DKB, DVB), F32), ob, nb, t, 0, ret_blk)
        ob, ret_s = _retention(qkb, pab, state_ret[l], ob, nbs, s_len, n_p, s_len)

        padc = C_PAD - C_COLS
        vecs = (jnp.pad(mu_c[l], (0, padc))[None], w0[l][None], a0[l][None], k_k[l][None],
                k_a[l][None], r_k[l].reshape(1, WC))
        zrow = lambda r0, r1, m: jnp.pad(m.astype(BF16), ((r0, LORA_PAD - r1), (0, 0)))
        mats = (zrow(0, LORA_W, w2[l]), zrow(LORA_W, LORA_W + LORA_A, a2[l]),
                zrow(LORA_W + LORA_A, LORA_W + LORA_A + LORA_G, g2[l]))
        prev_p = jnp.zeros((nb, 1, C_PAD), F32)
        prev_s = jnp.pad(state_shift[l], ((0, 0), (0, 0), (0, padc)))
        lng, lnb = lnx_g[l][None], lnx_b[l][None]
        oc_p, wkv_p, shift_p = _wkv(pc, prev_p, jnp.zeros((nb, HC, NC, NC), F32), vecs, mats,
                                    lng, lnb, ones_bd, nb, t, 0, ck_p, nsb_p)
        oc_s, wkv_s, shift_s = _wkv(pc, prev_s, state_wkv[l], vecs, mats,
                                    lng, lnb, ones_bd, nbs, s_len, n_p, ck_s, nsb_s)
        oc = jnp.concatenate([oc_p.reshape(n_p, WC), oc_s.reshape(n_s, WC)], axis=0)

        w_br = [_cast(w_branch, (l, k), WA, D_MODEL) for k in range(N_BRANCH)]
        merged = _merge(xb, oa, ob, oc, w_g, w_br)
        x, xb = _outproj(merged, x, _cast(w_out, (l,), D_MODEL, D_MODEL), ln_g[l, 1][None],
                         ln_b[l, 1][None], alpha)

        x, xb = _ffn(x, xb, *ffn_weights(1), ln_g[l, 2][None], ln_b[l, 2][None], alpha)

        outs["rp"].append(ret_p)
        outs["rs"].append(ret_s)
        outs["wp"].append(wkv_p)
        outs["ws"].append(wkv_s)
        outs["sp"].append(shift_p[:, :, :C_COLS])
        outs["ss"].append(shift_s[:, :, :C_COLS])

    st = {k: jnp.stack(v) for k, v in outs.items()}
    y_p = x[:n_p].reshape(nb, t, D_MODEL)
    y_s = x[n_p:].reshape(nbs, s_len, D_MODEL)
    kv_p = lambda a: a.reshape(depth, nb, t, HA, DVA)
    kv_s = lambda a: a.reshape(depth, nbs, s_len, HA, DVA)
    return (y_p, y_s, kv_p(k_p), kv_p(v_p), st["rp"], st["wp"], st["sp"],
            kv_s(k_s), kv_s(v_s), st["rs"], st["ws"], st["ss"])
```

```python
import functools
import math

import jax
import jax.numpy as jnp
from jax import lax
from jax.experimental import pallas as pl
from jax.experimental.pallas import tpu as pltpu

F32 = jnp.float32
BF16 = jnp.bfloat16

D_MODEL = 2048
CHUNK = 64
HA, DA, DVA = 8, 64, 128
ROT_A = DA // 4
ROPE_THETA = 500000.0
HB, DKB, DVB = 8, 64, 128
RET_THETA = 10000.0
HC, NC = 16, 64
LORA_W, LORA_A, LORA_G = 64, 64, 160
WA, WB, WC = HA * DVA, HB * DVB, HC * NC
N_BRANCH = 3
A_COLS = 2 * HA * 2 * DA + HA * DVA
B_COLS = 2 * HB * DKB + HB * DVB + WB
C_COLS = 3 * WC + LORA_W + LORA_A + LORA_G
D_FF = 5504
LN_EPS = 1e-5
SUBLN_EPS = 1e-5
RET_EPS = 1e-6
GN_EPS_C = 64e-5

LANE = 128
D_FF_PAD = 5632
FFN_TF = 512
LORA_PAD = 384
C_PAD = 3 * WC + LORA_PAD
ATTN_HEADS_PER_STEP = 2
VMEM_LIMIT = 56 * 1024 * 1024


def _cparams(sem):
    return pltpu.CompilerParams(dimension_semantics=sem, vmem_limit_bytes=VMEM_LIMIT)


def _pick(n, cands):
    for c in cands:
        if n % c == 0:
            return c
    raise ValueError(f"no tile for {n}")


def _sigmoid(x):
    return 1.0 / (1.0 + jnp.exp(-x))


def _ln_rows(y, g, b, eps):
    mu = jnp.mean(y, axis=-1, keepdims=True)
    d = y - mu
    var = jnp.mean(d * d, axis=-1, keepdims=True)
    return d * lax.rsqrt(var + eps) * g + b


def _dot(a, b):
    return jnp.dot(a, b, preferred_element_type=F32)


def _dot_nt(a, b):
    return lax.dot_general(a, b, (((1,), (1,)), ((), ())), preferred_element_type=F32)


def _dot_tn(a, b):
    return lax.dot_general(a, b, (((0,), (0,)), ((), ())), preferred_element_type=F32)


def _split2(x):
    hi = x.astype(BF16)
    lo = (x - hi.astype(F32)).astype(BF16)
    return hi, lo


def _split3(x):
    hi = x.astype(BF16)
    r1 = x - hi.astype(F32)
    mid = r1.astype(BF16)
    lo = (r1 - mid.astype(F32)).astype(BF16)
    return hi, mid, lo


def _mm3(a, b, dot=_dot):
    ah, al = a
    bh, bl = b
    return dot(ah, bh) + (dot(ah, bl) + dot(al, bh))


def _cast_kernel(w_ref, o_ref):
    o_ref[...] = w_ref[...].astype(BF16)


def _cast(src, lead, rows, width):
    tr = _pick(rows, (256, 128, 64, 32, 16, 8))
    return pl.pallas_call(
        _cast_kernel,
        grid=(rows // tr,),
        in_specs=[pl.BlockSpec((None,) * len(lead) + (tr, width), lambda i: lead + (i, 0))],
        out_specs=pl.BlockSpec((tr, width), lambda i: (i, 0)),
        out_shape=jax.ShapeDtypeStruct((rows, width), BF16),
        compiler_params=_cparams(("parallel",)),
        name="cast",
    )(src)


def _cast_ffn_in_kernel(a_ref, b_ref, oa_ref, ob_ref):
    for src, dst in ((a_ref, oa_ref), (b_ref, ob_ref)):
        dst[:, 0:D_FF] = src[...].astype(BF16)
        dst[:, D_FF:D_FF_PAD] = jnp.zeros((dst.shape[0], D_FF_PAD - D_FF), BF16)


def _cast_ffn_in(ffn_in, l, i):
    tr = 128
    half = lambda k: pl.BlockSpec((None, None, tr, D_FF), lambda r: (l, i, r, k))
    out = pl.BlockSpec((tr, D_FF_PAD), lambda r: (r, 0))
    return pl.pallas_call(
        _cast_ffn_in_kernel,
        grid=(D_MODEL // tr,),
        in_specs=[half(0), half(1)],
        out_specs=[out, out],
        out_shape=[jax.ShapeDtypeStruct((D_MODEL, D_FF_PAD), BF16)] * 2,
        compiler_params=_cparams(("parallel",)),
        name="cast_ffn_in",
    )(ffn_in, ffn_in)


def _cast_ffn_out_kernel(w_ref, o_ref):
    tr = w_ref.shape[0]
    row = lax.broadcasted_iota(jnp.int32, w_ref.shape, 0) + pl.program_id(0) * tr
    o_ref[...] = jnp.where(row < D_FF, w_ref[...], 0.0).astype(BF16)


def _cast_ffn_out(ffn_out, l, i):
    tr = FFN_TF
    return pl.pallas_call(
        _cast_ffn_out_kernel,
        grid=(D_FF_PAD // tr,),
        in_specs=[pl.BlockSpec((None, None, tr, D_MODEL), lambda r: (l, i, r, 0))],
        out_specs=pl.BlockSpec((tr, D_MODEL), lambda r: (r, 0)),
        out_shape=jax.ShapeDtypeStruct((D_FF_PAD, D_MODEL), BF16),
        compiler_params=_cparams(("parallel",)),
        name="cast_ffn_out",
    )(ffn_out)


def _cast_wc_kernel(w_ref, o_ref):
    lane = lax.broadcasted_iota(jnp.int32, w_ref.shape, 1)
    valid = lane < C_COLS - pl.program_id(1) * LORA_PAD
    o_ref[...] = jnp.where(valid, w_ref[...], 0.0).astype(BF16)


def _cast_wc(w_in, l):
    tr = 512
    cb0 = (A_COLS + B_COLS) // LORA_PAD
    return pl.pallas_call(
        _cast_wc_kernel,
        grid=(D_MODEL // tr, C_PAD // LORA_PAD),
        in_specs=[pl.BlockSpec((None, tr, LORA_PAD), lambda r, j: (l, r, cb0 + j))],
        out_specs=pl.BlockSpec((tr, LORA_PAD), lambda r, j: (r, j)),
        out_shape=jax.ShapeDtypeStruct((D_MODEL, C_PAD), BF16),
        compiler_params=_cparams(("parallel", "parallel")),
        name="cast_wc",
    )(w_in)


def _ffn_kernel(x_ref, xb_ref, wa_ref, wb_ref, wd_ref, g_ref, b_ref, o_ref, ob_ref, acc_ref, *, alpha):
    j = pl.program_id(1)

    @pl.when(j == 0)
    def _():
        acc_ref[...] = jnp.zeros_like(acc_ref)

    xb = xb_ref[...]
    ha = _dot(xb, wa_ref[...])
    hb = _dot(xb, wb_ref[...])
    act = (ha * _sigmoid(ha) * hb).astype(BF16)
    acc_ref[...] += _dot(act, wd_ref[...])

    @pl.when(j == pl.num_programs(1) - 1)
    def _():
        y = alpha * x_ref[...] + 0.5 * acc_ref[...]
        out = _ln_rows(y, g_ref[...], b_ref[...], LN_EPS)
        o_ref[...] = out
        ob_ref[...] = out.astype(BF16)


def _ffn(x, xb, wa, wb, wd, g, b, alpha):
    n = x.shape[0]
    tm = _pick(n, (512, 256, 128, 64, 32, 16, 8))
    tf = FFN_TF
    grid = (n // tm, D_FF_PAD // tf)
    row = pl.BlockSpec((tm, D_MODEL), lambda i, j: (i, 0))
    vec = pl.BlockSpec((1, D_MODEL), lambda i, j: (0, 0))
    return pl.pallas_call(
        functools.partial(_ffn_kernel, alpha=alpha),
        grid=grid,
        in_specs=[
            row, row,
            pl.BlockSpec((D_MODEL, tf), lambda i, j: (0, j)),
            pl.BlockSpec((D_MODEL, tf), lambda i, j: (0, j)),
            pl.BlockSpec((tf, D_MODEL), lambda i, j: (j, 0)),
            vec, vec,
        ],
        out_specs=[row, row],
        out_shape=[jax.ShapeDtypeStruct((n, D_MODEL), F32), jax.ShapeDtypeStruct((n, D_MODEL), BF16)],
        scratch_shapes=[pltpu.VMEM((tm, D_MODEL), F32)],
        compiler_params=_cparams(("parallel", "arbitrary")),
        name="ffn",
    )(x, xb, wa, wb, wd, g, b)


def _proj_kernel(x_ref, w_ref, o_ref):
    o_ref[...] = _dot(x_ref[...], w_ref[...])


def _proj(xb, w, tn, name, lead=(), m=None):
    n = xb.shape[0]
    m = w.shape[-1] if m is None else m
    tm = _pick(n, (1024, 512, 256, 128, 64, 32, 16, 8))
    return pl.pallas_call(
        _proj_kernel,
        grid=(n // tm, m // tn),
        in_specs=[
            pl.BlockSpec((tm, D_MODEL), lambda i, j: (i, 0)),
            pl.BlockSpec((None,) * len(lead) + (D_MODEL, tn), lambda i, j: lead + (0, j)),
        ],
        out_specs=pl.BlockSpec((tm, tn), lambda i, j: (i, j)),
        out_shape=jax.ShapeDtypeStruct((n, m), F32),
        compiler_params=_cparams(("parallel", "arbitrary")),
        name=name,
    )(xb, w)


def _rope_tables(pos, rot, theta, width):
    half = rot // 2
    freq = jnp.power(theta, -jnp.arange(half, dtype=F32) / half)
    ang = pos.astype(F32)[:, None] * freq[None, :]
    cos, sin = jnp.cos(ang), jnp.sin(ang)
    n = pos.shape[0]
    pad = width - rot
    c = jnp.concatenate([cos, cos, jnp.ones((n, pad), F32)], axis=-1)
    s_lo = jnp.concatenate([-sin, jnp.zeros((n, half + pad), F32)], axis=-1)
    s_hi = jnp.concatenate([jnp.zeros((n, half), F32), sin, jnp.zeros((n, pad), F32)], axis=-1)
    rep = LANE // width
    return tuple(jnp.tile(t, (1, rep)) for t in (c, s_lo + s_hi))


def _rope_kernel(xa_ref, xv_ref, xb_ref, ca_ref, sa_ref, cb_ref, sb_ref,
                 qa_buf, kall_buf, vall_buf, qkb_buf, qa_ref, kall_ref, vall_ref, qkb_ref, kd_ref, vd_ref):
    del qa_buf, kall_buf, vall_buf, qkb_buf

    lane = lax.broadcasted_iota(jnp.int32, (xa_ref.shape[0], LANE), 1)

    def partner(half, width):
        p = lane % width
        return jnp.where(p < half, lane + half, jnp.where(p < 2 * half, lane - half, lane))

    def rot(x, c, s, idx):
        return x * c + jnp.take_along_axis(x, idx, axis=1) * s

    ca, sa = ca_ref[...], sa_ref[...]
    idx_a = partner(ROT_A // 2, DA)
    nqa = HA * 2 * DA // LANE
    tm = xa_ref.shape[0]
    ks = []
    for t in range(2 * nqa):
        y = rot(xa_ref[:, t * LANE:(t + 1) * LANE], ca, sa, idx_a)
        if t < nqa:
            qa_ref[:, t * LANE:(t + 1) * LANE] = (y * (DA ** -0.5)).astype(BF16)
        else:
            ks.append(y)
            kd_ref[:, (t - nqa) * LANE:(t - nqa + 1) * LANE] = y.astype(BF16)
    kall_ref[...] = jnp.concatenate(ks, axis=1).reshape(tm, HA, 2 * DA)
    v = xv_ref[...]
    vall_ref[...] = v.reshape(tm, HA, DVA)
    vd_ref[...] = v.astype(BF16)
    cb, sb = cb_ref[...], sb_ref[...]
    idx_b = partner(DKB // 2, DKB)
    nqb = HB * DKB // LANE
    for t in range(2 * nqb):
        sl = slice(t * LANE, (t + 1) * LANE)
        y = rot(xb_ref[:, sl], cb, sb, idx_b)
        if t >= nqb:
            y = y * (DKB ** -0.5)
        qkb_ref[:, sl] = y


def _rope(pab, tabs_a, tabs_b, qa_buf, kall_buf, vall_buf, qkb_buf, row0, nrows, kv_row0):
    tm = _pick(nrows, (256, 128, 64, 32, 16, 8))
    rb0, kb0 = row0 // tm, kv_row0 // tm
    wqk = 2 * HA * 2 * DA
    wb = 2 * HB * DKB
    tab = pl.BlockSpec((tm, LANE), lambda i: (rb0 + i, 0))
    anyspec = pl.BlockSpec(memory_space=pl.ANY)
    bufs = (qa_buf, kall_buf, vall_buf, qkb_buf)
    cache = pl.BlockSpec((tm, HA, DVA), lambda i: (kb0 + i, 0, 0))
    dense = pl.BlockSpec((tm, WA), lambda i: (i, 0))
    return pl.pallas_call(
        _rope_kernel,
        grid=(nrows // tm,),
        in_specs=[
            pl.BlockSpec((tm, wqk), lambda i: (rb0 + i, 0)),
            pl.BlockSpec((tm, WA), lambda i: (rb0 + i, wqk // WA)),
            pl.BlockSpec((tm, wb), lambda i: (rb0 + i, A_COLS // wb)),
            tab, tab, tab, tab,
            anyspec, anyspec, anyspec, anyspec,
        ],
        out_specs=[
            pl.BlockSpec((tm, WA), lambda i: (rb0 + i, 0)),
            cache, cache,
            pl.BlockSpec((tm, wb), lambda i: (rb0 + i, 0)),
            dense, dense,
        ],
        out_shape=[jax.ShapeDtypeStruct(b.shape, b.dtype) for b in bufs]
                  + [jax.ShapeDtypeStruct((nrows, WA), BF16)] * 2,
        input_output_aliases={7: 0, 8: 1, 9: 2, 10: 3},
        compiler_params=_cparams(("parallel",)),
        name="rope",
    )(pab, pab, pab, *tabs_a, *tabs_b, *bufs)


def _diff_lambda(dl_ref, lam_init):
    lv = dl_ref[...]
    s1 = jnp.sum(lv[0:1] * lv[1:2], axis=-1, keepdims=True)
    s2 = jnp.sum(lv[2:3] * lv[3:4], axis=-1, keepdims=True)
    return jnp.exp(s1) - jnp.exp(s2) + lam_init


def _split_maps(q):
    lane = lax.broadcasted_iota(jnp.int32, q.shape, 1)
    zero = jnp.zeros_like(q)
    return jnp.where(lane < DA, q, zero), jnp.where(lane >= DA, q, zero)


def _subln(o, g, scale):
    ms = jnp.mean(o * o, axis=-1, keepdims=True)
    return o * lax.rsqrt(ms + SUBLN_EPS) * g * scale


def _attn_prompt_kernel(q_ref, kb_ref, v_ref, dl_ref, g_ref, o_buf, o_ref, vb_ref, s_ref,
                        m_ref, acc_ref, *, tq, tk, nh, lam_init):
    del o_buf
    qi = pl.program_id(2)
    hsl = [slice(h * 2 * DA, (h + 1) * 2 * DA) for h in range(nh)]
    chains = range(2 * nh)

    @pl.when(qi == 0)
    def _():
        for h in range(nh):
            vb_ref[:, 2 * h * DVA:(2 * h + 1) * DVA] = v_ref[:, hsl[h]]
            vb_ref[:, (2 * h + 1) * DVA:(2 * h + 2) * DVA] = jnp.ones((v_ref.shape[0], DVA), BF16)

    lam = _diff_lambda(dl_ref, lam_init)
    qs = [qm for h in range(nh) for qm in _split_maps(q_ref[:, hsl[h]])]

    def scores(kb, slot):
        rows = pl.ds(pl.multiple_of(kb * tk, tk), tk)
        for c in chains:
            s_ref[slot, c] = _dot_nt(qs[c], kb_ref[rows, hsl[c // 2]])

    def consume(ss, start, width):
        stats = []
        for c in chains:
            m = m_ref[c]
            m_new = jnp.maximum(m, jnp.max(ss[c], axis=-1, keepdims=True))
            m_ref[c] = m_new
            stats.append((jnp.exp(m - m_new), jnp.exp(ss[c] - m_new).astype(BF16)))
        pvs = [_dot(stats[c][1], vb_ref[pl.ds(start, width), (c // 2) * 2 * DVA:(c // 2 + 1) * 2 * DVA])
               for c in chains]
        for c in chains:
            acc_ref[c] = stats[c][0] * acc_ref[c] + pvs[c]

    def full_step(kb, slot):
        scores(kb + 1, 1 - slot)
        consume([s_ref[slot, c] for c in chains], pl.multiple_of(kb * tk, tk), tk)

    q0 = qi * tq
    n_full = q0 // tk
    m_ref[...] = jnp.full(m_ref.shape, -1e30, F32)
    acc_ref[...] = jnp.zeros_like(acc_ref)
    scores(0, 0)

    def two_steps(j, _):
        full_step(2 * j, 0)
        full_step(2 * j + 1, 1)
        return 0

    lax.fori_loop(0, n_full // 2, two_steps, 0)

    @pl.when(n_full % 2 == 1)
    def _():
        full_step(n_full - 1, 0)

    tail = pl.multiple_of(n_full * tk, tk)
    g = g_ref[...]

    for r in range(tk // tq):
        @pl.when(q0 - tail == r * tq)
        def _():
            width = (r + 1) * tq
            row = lax.broadcasted_iota(jnp.int32, (tq, width), 0) // CHUNK
            col = lax.broadcasted_iota(jnp.int32, (tq, width), 1) // CHUNK
            ok = col - row <= r * (tq // CHUNK)
            consume([jnp.where(ok, s_ref[n_full % 2, c, :, 0:width], -1e30) for c in chains], tail, width)
            for h in range(nh):
                a1, a2 = acc_ref[2 * h], acc_ref[2 * h + 1]
                o = a1[:, 0:DVA] / a1[:, DVA:2 * DVA] - lam * (a2[:, 0:DVA] / a2[:, DVA:2 * DVA])
                o_ref[:, hsl[h]] = _subln(o, g, 1.0 - lam_init).astype(BF16)


def _attn_prompt(qa, kd, vd, dl, g, o_buf, nb, t, lam_init):
    tq = _pick(t, (512, 256, 128, 64))
    tk = _pick(t, (1024, 512, 256, 128, 64))
    nh = ATTN_HEADS_PER_STEP
    nq = t // tq
    kv = pl.BlockSpec((t, nh * 2 * DA), lambda b, h, i: (b, h))
    return pl.pallas_call(
        functools.partial(_attn_prompt_kernel, tq=tq, tk=tk, nh=nh, lam_init=lam_init),
        grid=(nb, HA // nh, nq),
        in_specs=[
            pl.BlockSpec((tq, nh * 2 * DA), lambda b, h, i: (b * nq + i, h)),
            kv, kv,
            pl.BlockSpec((4, DA), lambda b, h, i: (0, 0)),
            pl.BlockSpec((1, DVA), lambda b, h, i: (0, 0)),
            pl.BlockSpec(memory_space=pl.ANY),
        ],
        out_specs=pl.BlockSpec((tq, nh * DVA), lambda b, h, i: (b * nq + i, h)),
        out_shape=jax.ShapeDtypeStruct(o_buf.shape, o_buf.dtype),
        input_output_aliases={5: 0},
        scratch_shapes=[pltpu.VMEM((t, nh * 2 * DVA), BF16),
                        pltpu.VMEM((2, 2 * nh, tq, tk), F32),
                        pltpu.VMEM((2 * nh, tq, 1), F32), pltpu.VMEM((2 * nh, tq, 2 * DVA), F32)],
        compiler_params=_cparams(("parallel", "parallel", "arbitrary")),
        name="attn_prompt",
    )(qa, kd, vd, dl, g, o_buf)


def _attn_sample_kernel(q_ref, kn_ref, vn_ref, ck_ref, cv_ref, dl_ref, g_ref, o_buf, o_ref, *,
                        s_len, lam_init):
    del o_buf
    lam = _diff_lambda(dl_ref, lam_init)
    g = g_ref[...]
    p_len = ck_ref.shape[0]
    kc_all = ck_ref[...].reshape(p_len, WA).astype(BF16)
    vc_all = cv_ref[...].reshape(p_len, WA).astype(BF16)
    heads = range(HA)
    sls = [slice(h * 2 * DA, (h + 1) * 2 * DA) for h in heads]
    qq = [jnp.concatenate(_split_maps(q_ref[:, sl]), axis=0) for sl in sls]
    s_c = [_dot_nt(qq[h], kc_all[:, sls[h]]) for h in heads]
    s_n = [_dot_nt(qq[h], kn_ref[:, sls[h]]) for h in heads]
    p_c, p_n, den = [], [], []
    for h in heads:
        m = jnp.maximum(jnp.max(s_c[h], axis=-1, keepdims=True), jnp.max(s_n[h], axis=-1, keepdims=True))
        e_c = jnp.exp(s_c[h] - m)
        e_n = jnp.exp(s_n[h] - m)
        den.append(jnp.sum(e_c, axis=-1, keepdims=True) + jnp.sum(e_n, axis=-1, keepdims=True))
        p_c.append(e_c.astype(BF16))
        p_n.append(e_n.astype(BF16))
    pv = [_dot(p_c[h], vc_all[:, sls[h]]) + _dot(p_n[h], vn_ref[:, sls[h]]) for h in heads]
    for h in heads:
        o = pv[h] / den[h]
        od = o[:s_len] - lam * o[s_len:]
        o_ref[:, sls[h]] = _subln(od, g, 1.0 - lam_init).astype(BF16)


def _attn_sample(qa, kd, vd, ck, cv, dl, g, o_buf, layer, nbs, s_len, row0, lam_init):
    p_len = ck.shape[2]
    rb0 = row0 // s_len
    new = pl.BlockSpec((s_len, WA), lambda b: (b, 0))
    cache = pl.BlockSpec((None, None, p_len, HA, DVA), lambda b: (layer, b, 0, 0, 0))
    return pl.pallas_call(
        functools.partial(_attn_sample_kernel, s_len=s_len, lam_init=lam_init),
        grid=(nbs,),
        in_specs=[
            pl.BlockSpec((s_len, WA), lambda b: (rb0 + b, 0)),
            new, new, cache, cache,
            pl.BlockSpec((4, DA), lambda b: (0, 0)),
            pl.BlockSpec((1, DVA), lambda b: (0, 0)),
            pl.BlockSpec(memory_space=pl.ANY),
        ],
        out_specs=pl.BlockSpec((s_len, WA), lambda b: (rb0 + b, 0)),
        out_shape=jax.ShapeDtypeStruct(o_buf.shape, o_buf.dtype),
        input_output_aliases={7: 0},
        compiler_params=_cparams(("parallel",)),
        name="attn_sample",
    )(qa, kd, vd, ck, cv, dl, g, o_buf)


def _ret_kernel(q_ref, k_ref, v_ref, gate_ref, r0_ref, o_buf, o_ref, rout_ref, st_ref, dec_ref, *, blk):
    del o_buf
    c = pl.program_id(1)

    @pl.when(c == 0)
    def _():
        st_ref[...] = jnp.zeros_like(st_ref)
        for h in range(HB):
            e = h % 2
            st_ref[h, e * DKB:(e + 1) * DKB, :] = r0_ref[h]
        n_i = lax.broadcasted_iota(jnp.int32, (blk, blk), 0)
        m_i = lax.broadcasted_iota(jnp.int32, (blk, blk), 1)
        diff = jnp.maximum(n_i - m_i, 0).astype(F32)
        for h in range(HB):
            dec_ref[h] = jnp.where(n_i >= m_i, jnp.exp(diff * math.log1p(-2.0 ** (-5.0 - h))), 0.0)

    n_col = lax.broadcasted_iota(jnp.int32, (blk, LANE), 0).astype(F32)
    lane = lax.broadcasted_iota(jnp.int32, (blk, LANE), 1)
    heads = range(HB)
    log_g = [math.log1p(-2.0 ** (-5.0 - h)) for h in heads]
    vsl = [slice(h * DVB, (h + 1) * DVB) for h in heads]
    qb, kb, kd, v = [], [], [], []
    for h in heads:
        p, e = h // 2, h % 2
        sl = slice(p * LANE, (p + 1) * LANE)
        mine = (lane >= DKB) if e else (lane < DKB)
        k = jnp.where(mine, k_ref[:, sl], 0.0)
        qb.append(jnp.where(mine, q_ref[:, sl], 0.0).astype(BF16))
        kb.append(k.astype(BF16))
        kd.append((k * jnp.exp((blk - 1.0 - n_col) * log_g[h])).astype(BF16))
        v.append(v_ref[:, vsl[h]].astype(BF16))
    r_old = [st_ref[h] for h in heads]
    s = [_dot_nt(qb[h], kb[h]) for h in heads]
    cross = [_dot(qb[h], r_old[h].astype(BF16)) for h in heads]
    grow = [_dot_tn(kd[h], v[h]) for h in heads]
    sd = [(s[h] * dec_ref[h]).astype(BF16) for h in heads]
    intra = [_dot(sd[h], v[h]) for h in heads]
    for h in heads:
        st_ref[h] = math.exp(blk * log_g[h]) * r_old[h] + grow[h]
        o = intra[h] + cross[h] * jnp.exp((n_col + 1.0) * log_g[h])
        ms = jnp.mean(o * o, axis=-1, keepdims=True)
        gt = gate_ref[:, vsl[h]]
        o_ref[:, vsl[h]] = (o * lax.rsqrt(ms + RET_EPS) * (gt * _sigmoid(gt))).astype(BF16)

    @pl.when(c == pl.num_programs(1) - 1)
    def _():
        for h in range(HB):
            e = h % 2
            rout_ref[h] = st_ref[h, e * DKB:(e + 1) * DKB, :]


def _retention(qkb, pab, r0, o_buf, nseg, seg_len, row0, blk):
    nc = seg_len // blk
    rb0 = row0 // blk
    vcol = (A_COLS + 2 * HB * DKB) // WB
    row = lambda b, c: rb0 + b * nc + c
    return pl.pallas_call(
        functools.partial(_ret_kernel, blk=blk),
        grid=(nseg, nc),
        in_specs=[
            pl.BlockSpec((blk, HB * DKB), lambda b, c: (row(b, c), 0)),
            pl.BlockSpec((blk, HB * DKB), lambda b, c: (row(b, c), 1)),
            pl.BlockSpec((blk, WB), lambda b, c: (row(b, c), vcol)),
            pl.BlockSpec((blk, WB), lambda b, c: (row(b, c), vcol + 1)),
            pl.BlockSpec((None, HB, DKB, DVB), lambda b, c: (b, 0, 0, 0)),
            pl.BlockSpec(memory_space=pl.ANY),
        ],
        out_specs=[
            pl.BlockSpec((blk, WB), lambda b, c: (row(b, c), 0)),
            pl.BlockSpec((None, HB, DKB, DVB), lambda b, c: (b, 0, 0, 0)),
        ],
        out_shape=[jax.ShapeDtypeStruct(o_buf.shape, o_buf.dtype),
                   jax.ShapeDtypeStruct((nseg, HB, DKB, DVB), F32)],
        input_output_aliases={5: 0},
        scratch_shapes=[pltpu.VMEM((HB, 2 * DKB, DVB), F32), pltpu.VMEM((HB, blk, blk), F32)],
        compiler_params=_cparams(("parallel", "arbitrary")),
        name="retention",
    )(qkb, qkb, pab, pab, r0, o_buf)


def _head_sum(x, ones_bd):
    hi, lo = _split2(x)
    return _dot(hi, ones_bd) + _dot(lo, ones_bd)


def _wkv_kernel(*refs, ck, nsb):
    pc_refs = refs[:nsb]
    (prev_ref, s0_ref, mu_ref, w0_ref, a0_ref, kk_ref, ka_ref, rk_ref, w2_ref, a2_ref, g2_ref,
     lng_ref, lnb_ref, tri_ref, ones_ref, o_ref, sout_ref, last_ref, st_ref, carry_ref) = refs[nsb:]
    c = pl.program_id(1)

    @pl.when(c == 0)
    def _():
        zero = jnp.zeros((NC, NC), F32)
        for g in range(nsb):
            for p in range(HC // 2):
                top = jnp.concatenate([s0_ref[g, 2 * p], zero], axis=1)
                bot = jnp.concatenate([zero, s0_ref[g, 2 * p + 1]], axis=1)
                st_ref[g, p] = jnp.concatenate([top, bot], axis=0)
        carry_ref[...] = prev_ref[...]

    row = lax.broadcasted_iota(jnp.int32, (ck, C_PAD), 0)
    pcs, shifted = [], []
    for g, pc_ref in enumerate(pc_refs):
        pc = pc_ref[...]
        pcs.append(pc)
        shifted.append(jnp.where(row == 0, carry_ref[g], pltpu.roll(pc, 1, 0)))
        carry_ref[g] = pc[ck - 1:ck, :]
        last_ref[g] = pc[ck - 1:ck, :]
    pc_all = jnp.concatenate(pcs, axis=0)
    pcm = pc_all + mu_ref[...] * (jnp.concatenate(shifted, axis=0) - pc_all)
    rc = pcm[:, 0:WC]
    kc = pcm[:, WC:2 * WC]
    vc = pcm[:, 2 * WC:3 * WC]
    lora = pcm[:, 3 * WC:C_PAD]
    z = w0_ref[...] + _dot(jnp.tanh(lora).astype(BF16), w2_ref[...])
    u = -z
    softplus = jnp.maximum(u, 0.0) + jnp.log1p(jnp.exp(-jnp.abs(u)))
    lw = -jnp.exp(-softplus - 0.5)
    a_sig = _sigmoid(a0_ref[...] + _dot(lora.astype(BF16), a2_ref[...]))
    gate = _dot(_sigmoid(lora).astype(BF16), g2_ref[...])
    kk = kc * kk_ref[...]
    kc2 = kc * (1.0 + (a_sig - 1.0) * ka_ref[...])
    rkr = rc * kc2 * rk_ref[...]
    ones_bd = ones_ref[...]

    tri = tri_ref[...]
    l1, l2, l3 = _split3(lw)
    cum = _dot(tri, l1) + (_dot(tri, l2) + _dot(tri, l3))
    total = jnp.concatenate(
        [jnp.broadcast_to(cum[(g + 1) * ck - 1:(g + 1) * ck, :], (ck, WC)) for g in range(nsb)], axis=0)
    e_neg = jnp.exp(-cum)
    e_rem = jnp.exp(total - cum)
    e_exc = jnp.exp(cum - lw)
    r_t = rc * jnp.exp(cum)
    k_t = kc2 * e_neg
    k_p = kc2 * e_rem
    p_tot = jnp.exp(total)

    lane = lax.broadcasted_iota(jnp.int32, (ck, LANE), 1)
    m0 = lane < NC

    def stack(x):
        return jnp.concatenate([jnp.where(m0, x, 0.0), jnp.where(m0, 0.0, x)], axis=0)

    n2 = 2 * ck
    ri = lax.broadcasted_iota(jnp.int32, (n2, n2), 0)
    ci = lax.broadcasted_iota(jnp.int32, (n2, n2), 1)
    same = (ri // ck) == (ci // ck)
    strict = jnp.logical_and(same, ri > ci)
    incl = jnp.logical_and(same, ri >= ci)

    bf = lambda t: t.astype(BF16)
    pairs = range(HC // 2)
    chains = [(g, p) for g in range(nsb) for p in pairs]
    a_s, r_s, b_s, k_s, v_s, bp_s, kp_s, bonus = {}, {}, {}, {}, {}, {}, {}, {}
    for p in pairs:
        sl = slice(p * LANE, (p + 1) * LANE)
        kkt = kk[:, sl]
        kn = kkt / jnp.maximum(jnp.sqrt(_head_sum(kkt * kkt, ones_bd)), 1e-12)
        bn = kn * a_sig[:, sl]
        a_tp = -kn * e_exc[:, sl]
        b_tp = bn * e_neg[:, sl]
        b_pp = bn * e_rem[:, sl]
        bonus_p = _head_sum(rkr[:, sl], ones_bd) * vc[:, sl]
        for g in range(nsb):
            rows = slice(g * ck, (g + 1) * ck)
            a_s[g, p] = bf(stack(a_tp[rows]))
            r_s[g, p] = bf(stack(r_t[rows, sl]))
            b_s[g, p] = bf(stack(b_tp[rows]))
            k_s[g, p] = bf(stack(k_t[rows, sl]))
            v_s[g, p] = _split2(stack(vc[rows, sl]))
            bp_s[g, p] = _split2(stack(b_pp[rows]))
            kp_s[g, p] = _split2(stack(k_p[rows, sl]))
            bonus[g, p] = bonus_p[rows]

    s_old = {gp: st_ref[gp[0], gp[1]] for gp in chains}
    s_b = {gp: bf(s_old[gp]) for gp in chains}
    if n2 % LANE == 0:
        big = {gp: _dot_nt(jnp.concatenate([a_s[gp], r_s[gp]], axis=0),
                           jnp.concatenate([b_s[gp], k_s[gp]], axis=0)) for gp in chains}
        ab = {gp: big[gp][0:n2, 0:n2] for gp in chains}
        ak = {gp: big[gp][0:n2, n2:2 * n2] for gp in chains}
        rb = {gp: big[gp][n2:2 * n2, 0:n2] for gp in chains}
        rk = {gp: big[gp][n2:2 * n2, n2:2 * n2] for gp in chains}
    else:
        ab = {gp: _dot_nt(a_s[gp], b_s[gp]) for gp in chains}
        ak = {gp: _dot_nt(a_s[gp], k_s[gp]) for gp in chains}
        rb = {gp: _dot_nt(r_s[gp], b_s[gp]) for gp in chains}
        rk = {gp: _dot_nt(r_s[gp], k_s[gp]) for gp in chains}
    lk = {gp: bf(jnp.where(strict, ab[gp], 0.0)) for gp in chains}
    l_ak = {gp: bf(jnp.where(strict, ak[gp], 0.0)) for gp in chains}
    l_rb = {gp: bf(jnp.where(incl, rb[gp], 0.0)) for gp in chains}
    l_rk = {gp: bf(jnp.where(incl, rk[gp], 0.0)) for gp in chains}
    if n2 % LANE == 0:
        sv = {gp: jnp.concatenate([bf(s_old[gp].T), v_s[gp][0]], axis=0) for gp in chains}
        x = {gp: _dot(jnp.concatenate([a_s[gp], l_ak[gp]], axis=1), sv[gp]) for gp in chains}
        y_part = {gp: _dot(jnp.concatenate([r_s[gp], l_rk[gp]], axis=1), sv[gp]) for gp in chains}
    else:
        x = {gp: _dot_nt(a_s[gp], s_b[gp]) + _dot(l_ak[gp], v_s[gp][0]) for gp in chains}
        y_part = {gp: _dot_nt(r_s[gp], s_b[gp]) + _dot(l_rk[gp], v_s[gp][0]) for gp in chains}
    span = 1
    while span < ck:
        x = {gp: x[gp] + _dot(lk[gp], bf(x[gp])) for gp in chains}
        span *= 2
        if span < ck:
            lk = {gp: bf(_dot(lk[gp], lk[gp])) for gp in chains}
    x_sp = {gp: _split2(x[gp]) for gp in chains}
    y_st = {gp: y_part[gp] + _dot(l_rb[gp], x_sp[gp][0]) for gp in chains}
    for g, p in chains:
        sl = slice(p * LANE, (p + 1) * LANE)
        decay = p_tot[(g + 1) * ck - 1:(g + 1) * ck, sl]
        uv = tuple(jnp.concatenate([xs, vs], axis=0) for xs, vs in zip(x_sp[g, p], v_s[g, p]))
        bk = tuple(jnp.concatenate([bs, ks], axis=0) for bs, ks in zip(bp_s[g, p], kp_s[g, p]))
        st_ref[g, p] = s_old[g, p] * decay + _mm3(uv, bk, _dot_tn)
    for g, p in chains:
        sl = slice(p * LANE, (p + 1) * LANE)
        rows = slice(g * ck, (g + 1) * ck)
        y = y_st[g, p][0:ck] + y_st[g, p][ck:n2]
        mean = _head_sum(y, ones_bd) * (1.0 / NC)
        d = y - mean
        var = _head_sum(d * d, ones_bd) * (1.0 / NC)
        yn = d * lax.rsqrt(var + GN_EPS_C) * lng_ref[:, sl] + lnb_ref[:, sl]
        o_ref[g, :, sl] = ((yn + bonus[g, p]) * gate[rows, sl]).astype(BF16)

    @pl.when(c == pl.num_programs(1) - 1)
    def _():
        for g in range(nsb):
            for p in range(HC // 2):
                s_pair = st_ref[g, p]
                sout_ref[g, 2 * p] = s_pair[0:NC, 0:NC]
                sout_ref[g, 2 * p + 1] = s_pair[NC:2 * NC, NC:2 * NC]


def _wkv(pc, prev, s0, vecs, mats, lng, lnb, ones_bd, nseg, seg_len, row0, ck, nsb):
    nc = seg_len // ck
    rb0 = row0 // ck
    pos = jnp.arange(nsb * ck)
    tri = ((pos[:, None] // ck == pos[None, :] // ck) & (pos[:, None] >= pos[None, :])).astype(BF16)
    pc_spec = lambda g: pl.BlockSpec((ck, C_PAD), lambda b, c: (rb0 + (b * nsb + g) * nc + c, 0))
    vec = lambda w: pl.BlockSpec((1, w), lambda b, c: (0, 0))
    mat = pl.BlockSpec((LORA_PAD, WC), lambda b, c: (0, 0))
    shift = pl.BlockSpec((nsb, 1, C_PAD), lambda b, c: (b, 0, 0))
    st = pl.BlockSpec((nsb, HC, NC, NC), lambda b, c: (b, 0, 0, 0))
    return pl.pallas_call(
        functools.partial(_wkv_kernel, ck=ck, nsb=nsb),
        grid=(nseg // nsb, nc),
        in_specs=[pc_spec(g) for g in range(nsb)] + [
            shift, st,
            vec(C_PAD), vec(WC), vec(WC), vec(WC), vec(WC), vec(WC),
            mat, mat, mat, vec(WC), vec(WC),
            pl.BlockSpec((nsb * ck, nsb * ck), lambda b, c: (0, 0)),
            pl.BlockSpec((LANE, LANE), lambda b, c: (0, 0)),
        ],
        out_specs=[pl.BlockSpec((nsb, ck, WC), lambda b, c: (b, c, 0)), st, shift],
        out_shape=[jax.ShapeDtypeStruct((nseg, seg_len, WC), BF16),
                   jax.ShapeDtypeStruct((nseg, HC, NC, NC), F32),
                   jax.ShapeDtypeStruct((nseg, 1, C_PAD), F32)],
        scratch_shapes=[pltpu.VMEM((nsb, HC // 2, LANE, LANE), F32), pltpu.VMEM((nsb, 1, C_PAD), F32)],
        compiler_params=_cparams(("parallel", "arbitrary")),
        name="wkv",
    )(*([pc] * nsb), prev, s0, *vecs, *mats, lng, lnb, tri, ones_bd)


def _merge_kernel(xb_ref, oa_ref, ob_ref, oc_ref, ga_ref, gb_ref, gc_ref, wa_ref, wb_ref, wc_ref, o_ref):
    xb = xb_ref[...]
    zs = [_dot(xb, g_ref[...]) for g_ref in (ga_ref, gb_ref, gc_ref)]
    ys = [_dot(b_ref[...], w_ref[...])
          for b_ref, w_ref in ((oa_ref, wa_ref), (ob_ref, wb_ref), (oc_ref, wc_ref))]
    o_ref[...] = (_sigmoid(zs[0]) * ys[0] + _sigmoid(zs[1]) * ys[1]) + _sigmoid(zs[2]) * ys[2]


def _merge(xb, oa, ob, oc, w_gate, wbr):
    n = oa.shape[0]
    tm = _pick(n, (1024, 512, 256, 128, 64, 32, 16, 8))
    tn = 512
    nj = D_MODEL // tn
    bw = oa.shape[1]
    o_spec = pl.BlockSpec((tm, bw), lambda i, j: (i, 0))
    g_spec = lambda k: pl.BlockSpec((D_MODEL, tn), lambda i, j: (0, k * nj + j))
    w_spec = pl.BlockSpec((bw, tn), lambda i, j: (0, j))
    return pl.pallas_call(
        _merge_kernel,
        grid=(n // tm, nj),
        in_specs=[pl.BlockSpec((tm, D_MODEL), lambda i, j: (i, 0)), o_spec, o_spec, o_spec,
                  g_spec(0), g_spec(1), g_spec(2), w_spec, w_spec, w_spec],
        out_specs=pl.BlockSpec((tm, tn), lambda i, j: (i, j)),
        out_shape=jax.ShapeDtypeStruct((n, D_MODEL), F32),
        compiler_params=_cparams(("parallel", "arbitrary")),
        name="merge",
    )(xb, oa, ob, oc, w_gate, w_gate, w_gate, *wbr)


def _outproj_kernel(m_ref, x_ref, w_ref, g_ref, b_ref, o_ref, ob_ref, *, alpha):
    half = m_ref.shape[0] // 2
    for r in (slice(0, half), slice(half, 2 * half)):
        y = alpha * x_ref[r, :] + _dot(m_ref[r, :].astype(BF16), w_ref[...])
        out = _ln_rows(y, g_ref[...], b_ref[...], LN_EPS)
        o_ref[r, :] = out
        ob_ref[r, :] = out.astype(BF16)


def _outproj(merged, x, w, g, b, alpha):
    n = x.shape[0]
    tm = _pick(n, (512, 256, 128, 64, 32))
    row = pl.BlockSpec((tm, D_MODEL), lambda i: (i, 0))
    vec = pl.BlockSpec((1, D_MODEL), lambda i: (0, 0))
    return pl.pallas_call(
        functools.partial(_outproj_kernel, alpha=alpha),
        grid=(n // tm,),
        in_specs=[row, row, pl.BlockSpec((D_MODEL, D_MODEL), lambda i: (0, 0)), vec, vec],
        out_specs=[row, row],
        out_shape=[jax.ShapeDtypeStruct((n, D_MODEL), F32), jax.ShapeDtypeStruct((n, D_MODEL), BF16)],
        compiler_params=_cparams(("parallel",)),
        name="outproj",
    )(merged, x, w, g, b)


def _alloc_kernel(o_ref):
    del o_ref


def _alloc(shape, dtype):
    return pl.pallas_call(
        _alloc_kernel,
        out_specs=pl.BlockSpec(memory_space=pl.ANY),
        out_shape=jax.ShapeDtypeStruct(shape, dtype),
        name="alloc",
    )()


def kernel(x_prompt, x_sample, cache_k, cache_v, state_ret, state_wkv, state_shift, w_in, w_gate,
           w_branch, w_out, ffn_in, ffn_out, ln_g, ln_b, diff_lambda, subln_g, mu_c, w0, w2, a0, a2,
           g2, k_k, k_a, r_k, lnx_g, lnx_b):
    nb, t, _ = x_prompt.shape
    nbs, s_len, _ = x_sample.shape
    depth = w_in.shape[0]
    p_len = cache_k.shape[2]
    n_p = nb * t
    n_s = nbs * s_len
    n = n_p + n_s
    alpha = (2 * depth) ** 0.25

    x = jnp.concatenate([x_prompt.reshape(n_p, D_MODEL), x_sample.reshape(n_s, D_MODEL)], axis=0)
    xb = x.astype(BF16)

    pos = jnp.concatenate([jnp.tile(jnp.arange(t, dtype=jnp.int32), nb),
                           jnp.tile(p_len + jnp.arange(s_len, dtype=jnp.int32), nbs)])
    tabs_a = _rope_tables(pos, ROT_A, ROPE_THETA, DA)
    tabs_b = _rope_tables(pos, DKB, RET_THETA, DKB)

    ck_all, cv_all = cache_k, cache_v

    eye = jnp.arange(LANE) // NC
    ones_bd = (eye[:, None] == eye[None, :]).astype(BF16)
    ck_p = _pick(t, (CHUNK,))
    ck_s = s_len
    nsb_p = _pick(nb, (2, 1))
    nsb_s = _pick(nbs, (4, 2, 1))
    ret_blk = _pick(t, (256, 128, 64))

    outs = {k: [] for k in ("rp", "wp", "sp", "rs", "ws", "ss")}

    w_in_b = w_in.astype(BF16)

    qa = _alloc((n, WA), BF16)
    qkb = _alloc((n, 2 * HB * DKB), F32)
    k_p, v_p = (_alloc((depth * n_p, HA, DVA), F32) for _ in range(2))
    k_s, v_s = (_alloc((depth * n_s, HA, DVA), F32) for _ in range(2))
    oa, ob = _alloc((n, WA), BF16), _alloc((n, WB), BF16)

    for l in range(depth):
        lam_init = 0.8 - 0.6 * math.exp(-0.3 * l)

        def ffn_weights(i):
            return (*_cast_ffn_in(ffn_in, l, i), _cast_ffn_out(ffn_out, l, i))

        x, xb = _ffn(x, xb, *ffn_weights(0), ln_g[l, 0][None], ln_b[l, 0][None], alpha)

        w_c = _cast_wc(w_in_b, l)
        pab = _proj(xb, w_in_b, 1024, name="proj_ab", lead=(l,), m=A_COLS + B_COLS)
        pc = _proj(xb, w_c, C_PAD // 3, name="proj_c")
        w_g = _cast(w_gate, (l,), D_MODEL, N_BRANCH * D_MODEL)
        qa, k_p, v_p, qkb, kd_p, vd_p = _rope(pab, tabs_a, tabs_b, qa, k_p, v_p, qkb, 0, n_p, l * n_p)
        qa, k_s, v_s, qkb, kd_s, vd_s = _rope(pab, tabs_a, tabs_b, qa, k_s, v_s, qkb, n_p, n_s, l * n_s)

        dl = diff_lambda[l]
        sg = subln_g[l][None]
        oa = _attn_prompt(qa, kd_p, vd_p, dl, sg, oa, nb, t, lam_init)
        oa = _attn_sample(qa, kd_s, vd_s, ck_all, cv_all, dl, sg, oa, l, nbs, s_len, n_p, lam_init)

        ob, ret_p = _retention(qkb, pab, jnp.zeros((nb, HB, DKB, DVB), F32), ob, nb, t, 0, ret_blk)
        ob, ret_s = _retention(qkb, pab, state_ret[l], ob, nbs, s_len, n_p, s_len)

        padc = C_PAD - C_COLS
        vecs = (jnp.pad(mu_c[l], (0, padc))[None], w0[l][None], a0[l][None], k_k[l][None],
                k_a[l][None], r_k[l].reshape(1, WC))
        zrow = lambda r0, r1, m: jnp.pad(m.astype(BF16), ((r0, LORA_PAD - r1), (0, 0)))
        mats = (zrow(0, LORA_W, w2[l]), zrow(LORA_W, LORA_W + LORA_A, a2[l]),
                zrow(LORA_W + LORA_A, LORA_W + LORA_A + LORA_G, g2[l]))
        prev_p = jnp.zeros((nb, 1, C_PAD), F32)
        prev_s = jnp.pad(state_shift[l], ((0, 0), (0, 0), (0, padc)))
        lng, lnb = lnx_g[l][None], lnx_b[l][None]
        oc_p, wkv_p, shift_p = _wkv(pc, prev_p, jnp.zeros((nb, HC, NC, NC), F32), vecs, mats,
                                    lng, lnb, ones_bd, nb, t, 0, ck_p, nsb_p)
        oc_s, wkv_s, shift_s = _wkv(pc, prev_s, state_wkv[l], vecs, mats,
                                    lng, lnb, ones_bd, nbs, s_len, n_p, ck_s, nsb_s)
        oc = jnp.concatenate([oc_p.reshape(n_p, WC), oc_s.reshape(n_s, WC)], axis=0)

        w_br = [_cast(w_branch, (l, k), WA, D_MODEL) for k in range(N_BRANCH)]
        merged = _merge(xb, oa, ob, oc, w_g, w_br)
        x, xb = _outproj(merged, x, _cast(w_out, (l,), D_MODEL, D_MODEL), ln_g[l, 1][None],
                         ln_b[l, 1][None], alpha)

        x, xb = _ffn(x, xb, *ffn_weights(1), ln_g[l, 2][None], ln_b[l, 2][None], alpha)

        outs["rp"].append(ret_p)
        outs["rs"].append(ret_s)
        outs["wp"].append(wkv_p)
        outs["ws"].append(wkv_s)
        outs["sp"].append(shift_p[:, :, :C_COLS])
        outs["ss"].append(shift_s[:, :, :C_COLS])

    st = {k: jnp.stack(v) for k, v in outs.items()}
    y_p = x[:n_p].reshape(nb, t, D_MODEL)
    y_s = x[n_p:].reshape(nbs, s_len, D_MODEL)
    kv_p = lambda a: a.reshape(depth, nb, t, HA, DVA)
    kv_s = lambda a: a.reshape(depth, nbs, s_len, HA, DVA)
    return (y_p, y_s, kv_p(k_p), kv_p(v_p), st["rp"], st["wp"], st["sp"],
            kv_s(k_s), kv_s(v_s), st["rs"], st["ws"], st["ss"])
```
